```python
import jax, jax.numpy as jnp
from jax import lax
import numpy as np

D_MODEL = 1024
BATCH = 8
SEQ = 2048
DEPTH = 2
DEC_BATCH = 2
DEC_SEQ = 16384
PAST_LEN = 128

GRID_W = 64
NA_HEADS = 8
NA_HEAD_DIM = 64
NA_WIDTH = NA_HEADS * NA_HEAD_DIM
NA_MAX_ROWS = 8
NA_COLS = 16
RPB_ROWS = 2 * NA_MAX_ROWS - 1
RPB_COLS = 2 * NA_COLS - 1
GLA_HEADS = 4
GLA_DK = 128
GLA_DV = 128
GLA_WIDTH = GLA_HEADS * GLA_DV
GLA_GATE_RANK = 16
GLA_GATE_NORM = 16.0
GLA_CHUNK = 64
N_GROUPS = 4
EXPERTS_PER_GROUP = 4
N_EXPERTS = N_GROUPS * EXPERTS_PER_GROUP
TOP_K_IN_GROUP = 2
D_EXPERT = 512
D_IN = 7 * 512 + 2 * GLA_GATE_RANK + 2 * D_MODEL
EPS = 1e-6
NEG = -1e30

kernel_name = 'hybrid_na_gla_hmoe_encoder'


def _in_splits():
    sizes = [NA_WIDTH, NA_WIDTH, NA_WIDTH, GLA_HEADS * GLA_DK, GLA_HEADS * GLA_DK, GLA_WIDTH, GLA_WIDTH,
             GLA_GATE_RANK, GLA_GATE_RANK, D_MODEL, D_MODEL]
    return [int(s) for s in np.cumsum(sizes)[:-1]]


def rms_norm(x, g):
    xf = x.astype(jnp.float32)
    r = xf * lax.rsqrt(jnp.mean(xf * xf, axis=-1, keepdims=True) + EPS)
    return (r * g.astype(jnp.float32)).astype(x.dtype)


def neighbourhood_attention(q, k, v, rpb):
    B, N, H, Dh = q.shape
    rows = N // GRID_W
    kr = min(NA_MAX_ROWS, rows)
    qg = q.reshape(B, rows, GRID_W, H, Dh)
    kg = k.reshape(B, rows, GRID_W, H, Dh)
    vg = v.reshape(B, rows, GRID_W, H, Dh)
    cols = jnp.arange(GRID_W)
    col_start = jnp.clip(cols - NA_COLS // 2, 0, GRID_W - NA_COLS)
    col_mask = (cols[None, :] >= col_start[:, None]) & (cols[None, :] < col_start[:, None] + NA_COLS)
    col_idx = jnp.clip(cols[None, :] - cols[:, None] + NA_COLS - 1, 0, RPB_COLS - 1)
    rpb_cols = rpb.astype(jnp.float32)[:, :, col_idx]
    scale = Dh ** -0.5

    def one_row(r):
        rs = jnp.clip(r - kr // 2, 0, rows - kr)
        q_r = lax.dynamic_index_in_dim(qg, r, axis=1, keepdims=False).astype(jnp.float32)
        k_blk = lax.dynamic_slice_in_dim(kg, rs, kr, axis=1).astype(jnp.float32)
        v_blk = lax.dynamic_slice_in_dim(vg, rs, kr, axis=1).astype(jnp.float32)
        s = jnp.einsum('bqhd,bkchd->bhqkc', q_r, k_blk) * scale
        row_idx = rs + jnp.arange(kr) - r + NA_MAX_ROWS - 1
        bias = jnp.take(rpb_cols, row_idx, axis=1).transpose(0, 2, 1, 3)
        s = jnp.where(col_mask[:, None, :], s + bias[None], NEG)
        p = jax.nn.softmax(s.reshape(B, H, GRID_W, kr * GRID_W), axis=-1).reshape(B, H, GRID_W, kr, GRID_W)
        return jnp.einsum('bhqkc,bkchd->bqhd', p, v_blk).astype(q.dtype)

    out = lax.map(one_row, jnp.arange(rows))
    return out.transpose(1, 0, 2, 3, 4).reshape(B, N, H * Dh)


def gla_chunked(q, k, v, g):
    B, N, H, Dk = q.shape
    Dv = v.shape[-1]
    C = GLA_CHUNK
    nc = N // C

    def to_chunks(t):
        return t.reshape(B, nc, C, H, t.shape[-1]).transpose(1, 0, 3, 2, 4)

    causal = jnp.tril(jnp.ones((C, C), dtype=bool))

    def step(S, inp):
        qi, ki, vi, gi = inp
        b = jnp.cumsum(gi, axis=-2)
        diff = b[..., :, None, :] - b[..., None, :, :]
        decay = jnp.exp(jnp.where(causal[:, :, None], diff, NEG))
        A = jnp.einsum('bhid,bhjd,bhijd->bhij', qi, ki, decay)
        o_intra = jnp.einsum('bhij,bhje->bhie', A, vi)
        o_inter = jnp.einsum('bhid,bhde->bhie', qi * jnp.exp(b), S)
        b_last = b[..., -1:, :]
        S_new = jnp.exp(b_last[..., 0, :])[..., None] * S + jnp.einsum('bhjd,bhje->bhde', ki * jnp.exp(b_last - b), vi)
        return S_new, o_intra + o_inter

    S0 = jnp.zeros((B, H, Dk, Dv), jnp.float32)
    _, o = lax.scan(step, S0, (to_chunks(q), to_chunks(k), to_chunks(v), to_chunks(g)))
    return o.transpose(1, 0, 3, 2, 4).reshape(B, N, H, Dv)


def mixer_sublayer(x, norm_mix, w_in, q_norm_a, k_norm_a, rpb, w_decay, b_decay, gla_norm, w_o_a, w_o_b, w_out):
    B, N, _ = x.shape
    xn = rms_norm(x, norm_mix)
    h = xn @ w_in
    qa, ka, va, qb, kb, vb, rb, zf, zb, ga, gb = jnp.split(h, _in_splits(), axis=-1)
    qa = rms_norm(qa.reshape(B, N, NA_HEADS, NA_HEAD_DIM), q_norm_a)
    ka = rms_norm(ka.reshape(B, N, NA_HEADS, NA_HEAD_DIM), k_norm_a)
    va = va.reshape(B, N, NA_HEADS, NA_HEAD_DIM)
    ya = neighbourhood_attention(qa, ka, va, rpb) @ w_o_a
    qb = qb.reshape(B, N, GLA_HEADS, GLA_DK).astype(jnp.float32) * (GLA_DK ** -0.5)
    kb = kb.reshape(B, N, GLA_HEADS, GLA_DK).astype(jnp.float32)
    vb = vb.reshape(B, N, GLA_HEADS, GLA_DV).astype(jnp.float32)

    def log_decay(z, d):
        logits = z.astype(jnp.float32) @ w_decay[d].astype(jnp.float32) + b_decay[d].astype(jnp.float32)
        return (jax.nn.log_sigmoid(logits) / GLA_GATE_NORM).reshape(B, N, GLA_HEADS, GLA_DK)

    o_fwd = gla_chunked(qb, kb, vb, log_decay(zf, 0))
    o_bwd = jnp.flip(gla_chunked(jnp.flip(qb, 1), jnp.flip(kb, 1), jnp.flip(vb, 1), jnp.flip(log_decay(zb, 1), 1)), 1)
    o = rms_norm(o_fwd + o_bwd, gla_norm) * jax.nn.silu(rb.astype(jnp.float32)).reshape(B, N, GLA_HEADS, GLA_DV)
    yb = o.reshape(B, N, GLA_WIDTH).astype(x.dtype) @ w_o_b
    y = jax.nn.sigmoid(ga) * ya + jax.nn.sigmoid(gb) * yb
    return x + y @ w_out


def hier_moe(x, w_router_g, b_router_g, w_router_e, b_router_e, w_gate, w_up, w_down):
    B, N, D = x.shape
    t = x.reshape(B * N, D)
    lg = (t @ w_router_g).astype(jnp.float32) + b_router_g.astype(jnp.float32)
    pg = jax.nn.softmax(lg, axis=-1)
    grp = jnp.argmax(pg, axis=-1)
    pg_sel = jnp.max(pg, axis=-1)
    le = ((t @ w_router_e).astype(jnp.float32) + b_router_e.astype(jnp.float32)).reshape(-1, N_GROUPS, EXPERTS_PER_GROUP)
    le_sel = jnp.take_along_axis(le, grp[:, None, None], axis=1)[:, 0]
    pe = jax.nn.softmax(le_sel, axis=-1)
    top_v, top_i = lax.top_k(pe, TOP_K_IN_GROUP)
    top_v = top_v / jnp.sum(top_v, axis=-1, keepdims=True)
    w_grp = jnp.sum(jax.nn.one_hot(top_i, EXPERTS_PER_GROUP, dtype=jnp.float32) * top_v[..., None], axis=1)
    combine = (jax.nn.one_hot(grp, N_GROUPS, dtype=jnp.float32)[:, :, None]
               * (pg_sel[:, None] * w_grp)[:, None, :]).reshape(-1, N_EXPERTS)
    out = jnp.zeros((B * N, D), jnp.float32)
    for e in range(N_EXPERTS):
        hdn = jax.nn.silu(t @ w_gate[e]) * (t @ w_up[e])
        out = out + combine[:, e:e + 1] * (hdn @ w_down[e]).astype(jnp.float32)
    return out.astype(x.dtype).reshape(B, N, D)


def trunk(x, norm_mix, w_in, q_norm_a, k_norm_a, rpb, w_decay, b_decay, gla_norm, w_o_a, w_o_b, w_out,
          norm_ffn, w_router_g, b_router_g, w_router_e, b_router_e, w_gate, w_up, w_down):
    for l in range(DEPTH):
        x = mixer_sublayer(x, norm_mix[l], w_in[l], q_norm_a[l], k_norm_a[l], rpb[l], w_decay[l], b_decay[l],
                           gla_norm[l], w_o_a[l], w_o_b[l], w_out[l])
        x = x + hier_moe(rms_norm(x, norm_ffn[l]), w_router_g[l], b_router_g[l], w_router_e[l], b_router_e[l],
                         w_gate[l], w_up[l], w_down[l])
    return x


def setup_inputs(seed: int = 0) -> dict:
    key = jax.random.key(seed)
    ks = jax.random.split(key, 24)
    f32 = jnp.float32

    def nrm(k, shape, scale):
        return jax.random.normal(k, shape, f32) * scale

    D = D_MODEL
    return {
        'x_prompt': nrm(ks[0], (BATCH, SEQ, D), 1.0),
        'x_sample': nrm(ks[1], (DEC_BATCH, DEC_SEQ, D), 1.0),
        'norm_mix': 1.0 + nrm(ks[2], (DEPTH, D), 0.02),
        'w_in': nrm(ks[3], (DEPTH, D, D_IN), D ** -0.5),
        'q_norm_a': 1.0 + nrm(ks[4], (DEPTH, NA_HEAD_DIM), 0.02),
        'k_norm_a': 1.0 + nrm(ks[5], (DEPTH, NA_HEAD_DIM), 0.02),
        'rpb': nrm(ks[6], (DEPTH, NA_HEADS, RPB_ROWS, RPB_COLS), 0.1),
        'w_decay': nrm(ks[7], (DEPTH, 2, GLA_GATE_RANK, GLA_HEADS * GLA_DK), GLA_GATE_RANK ** -0.5),
        'b_decay': 2.0 + nrm(ks[8], (DEPTH, 2, GLA_HEADS * GLA_DK), 0.5),
        'gla_norm': 1.0 + nrm(ks[9], (DEPTH, GLA_DV), 0.02),
        'w_o_a': nrm(ks[10], (DEPTH, NA_WIDTH, D), NA_WIDTH ** -0.5),
        'w_o_b': nrm(ks[11], (DEPTH, GLA_WIDTH, D), GLA_WIDTH ** -0.5),
        'w_out': nrm(ks[12], (DEPTH, D, D), D ** -0.5),
        'norm_ffn': 1.0 + nrm(ks[13], (DEPTH, D), 0.02),
        'w_router_g': nrm(ks[14], (DEPTH, D, N_GROUPS), D ** -0.5),
        'b_router_g': nrm(ks[15], (DEPTH, N_GROUPS), 0.01),
        'w_router_e': nrm(ks[16], (DEPTH, D, N_EXPERTS), D ** -0.5),
        'b_router_e': nrm(ks[17], (DEPTH, N_EXPERTS), 0.01),
        'w_gate': nrm(ks[18], (DEPTH, N_EXPERTS, D, D_EXPERT), D ** -0.5),
        'w_up': nrm(ks[19], (DEPTH, N_EXPERTS, D, D_EXPERT), D ** -0.5),
        'w_down': nrm(ks[20], (DEPTH, N_EXPERTS, D_EXPERT, D), D_EXPERT ** -0.5),
    }


def reference(x_prompt, x_sample, norm_mix, w_in, q_norm_a, k_norm_a, rpb, w_decay, b_decay, gla_norm,
              w_o_a, w_o_b, w_out, norm_ffn, w_router_g, b_router_g, w_router_e, b_router_e, w_gate, w_up, w_down):
    y_prompt = trunk(x_prompt, norm_mix, w_in, q_norm_a, k_norm_a, rpb, w_decay, b_decay, gla_norm, w_o_a, w_o_b,
                     w_out, norm_ffn, w_router_g, b_router_g, w_router_e, b_router_e, w_gate, w_up, w_down)
    y_sample = trunk(x_sample, norm_mix, w_in, q_norm_a, k_norm_a, rpb, w_decay, b_decay, gla_norm, w_o_a, w_o_b,
                     w_out, norm_ffn, w_router_g, b_router_g, w_router_e, b_router_e, w_gate, w_up, w_down)
    return (y_prompt, y_sample)
```

```python
import functools

import jax
import jax.numpy as jnp
from jax import lax
from jax.experimental import pallas as pl
from jax.experimental.pallas import tpu as pltpu

F32 = jnp.float32
BF16 = jnp.bfloat16

D_MODEL = 1024
GRID_W = 64
NA_HEADS = 8
NA_HEAD_DIM = 64
NA_WIDTH = NA_HEADS * NA_HEAD_DIM
NA_ROWS = 8
NA_COLS = 16
RPB_ROWS = 2 * NA_ROWS - 1
RPB_COLS = 2 * NA_COLS - 1
GLA_HEADS = 4
GLA_DK = 128
GLA_DV = 128
GLA_WIDTH = GLA_HEADS * GLA_DV
GLA_RANK = 16
GLA_GATE_NORM = 16.0
GLA_CHUNK = 64
N_GROUPS = 4
EXPERTS_PER_GROUP = 4
N_EXPERTS = N_GROUPS * EXPERTS_PER_GROUP
D_EXPERT = 512
EPS = 1e-6
NEG = -1e30

LANES = 128
MAIN_WIDTH = 7 * 512
EXP_CLAMP = 80.0
ROUTER_GROUP_LANE0 = 0
ROUTER_EXPERT_LANE0 = N_GROUPS
VMEM_LIMIT = 56 * 1024 * 1024

_NT = (((1,), (1,)), ((), ()))
_TN = (((0,), (0,)), ((), ()))


def _dot(a, b):
    return jnp.dot(a, b, preferred_element_type=F32)


def _split_bf16(x):
    hi = x.astype(BF16)
    lo = (x - hi.astype(F32)).astype(BF16)
    return hi, lo


def _sigmoid(x):
    return 1.0 / (1.0 + jnp.exp(-x))


def _const_spec(shape):
    nd = len(shape)
    return pl.BlockSpec(shape, lambda *_: (0,) * nd, pipeline_mode=pl.Buffered(1))


def _in_proj_kernel(x_ref, nrm_ref, wm_ref, wz_ref, wg_ref, qg_ref, kg_ref, hsum_ref,
                    qa_ref, ka_ref, va_ref, qb_ref, kb_ref, vb_ref, rb_ref, z_ref, ga_ref, gb_ref):
    x = x_ref[...]
    ms = jnp.mean(x * x, axis=-1, keepdims=True)
    xn = (x * lax.rsqrt(ms + EPS) * nrm_ref[...]).astype(BF16)

    def proj(k):
        return _dot(xn, wm_ref[:, k * 512:(k + 1) * 512])

    def head_norm(a, gain_ref):
        ss = _dot((a * a).astype(BF16), hsum_ref[...])
        return a * lax.rsqrt(ss * (1.0 / NA_HEAD_DIM) + EPS) * gain_ref[...]

    qa_ref[...] = head_norm(proj(0), qg_ref).astype(BF16)
    ka_ref[...] = head_norm(proj(1), kg_ref).astype(BF16)
    va_ref[...] = proj(2).astype(BF16)
    qb_ref[...] = (proj(3) * (GLA_DK ** -0.5)).astype(BF16)
    kb_ref[...] = proj(4).astype(BF16)
    vb_ref[...] = proj(5).astype(BF16)
    r = proj(6)
    rb_ref[...] = (r * _sigmoid(r)).astype(BF16)
    z_ref[...] = _dot(xn, wz_ref[...])
    ga_ref[...] = _sigmoid(_dot(xn, wg_ref[:, :D_MODEL])).astype(BF16)
    gb_ref[...] = _sigmoid(_dot(xn, wg_ref[:, D_MODEL:])).astype(BF16)


def _in_proj(x, nrm, wm, wz, wg, qg, kg, hsum, tm):
    t = x.shape[0]
    row = lambda w: pl.BlockSpec((tm, w), lambda i: (i, 0))
    out_w = [512] * 7 + [LANES, D_MODEL, D_MODEL]
    out_dt = [BF16] * 7 + [F32, BF16, BF16]
    return pl.pallas_call(
        _in_proj_kernel,
        grid=(t // tm,),
        in_specs=[row(D_MODEL), _const_spec(nrm.shape), _const_spec(wm.shape), _const_spec(wz.shape),
                  _const_spec(wg.shape), _const_spec(qg.shape), _const_spec(kg.shape), _const_spec(hsum.shape)],
        out_specs=[row(w) for w in out_w],
        out_shape=[jax.ShapeDtypeStruct((t, w), dt) for w, dt in zip(out_w, out_dt)],
        compiler_params=pltpu.CompilerParams(dimension_semantics=("parallel",), vmem_limit_bytes=VMEM_LIMIT),
        name="in_proj",
    )(x, nrm, wm, wz, wg, qg, kg, hsum)


NA_BLOCK_ROWS = 8
NA_BLOCK_TOK = NA_BLOCK_ROWS * GRID_W
NA_WINDOW_TOK = NA_ROWS * GRID_W


def _na_kernel(q_ref, kp_ref, kc_ref, kn_ref, vp_ref, vc_ref, vn_ref, bias_ref, o_ref, kwin, vwin, *, rows):
    i = pl.program_id(1)
    kwin[0:NA_BLOCK_TOK] = kp_ref[...]
    kwin[NA_BLOCK_TOK:2 * NA_BLOCK_TOK] = kc_ref[...]
    kwin[2 * NA_BLOCK_TOK:3 * NA_BLOCK_TOK] = kn_ref[...]
    vwin[0:NA_BLOCK_TOK] = vp_ref[...]
    vwin[NA_BLOCK_TOK:2 * NA_BLOCK_TOK] = vc_ref[...]
    vwin[2 * NA_BLOCK_TOK:3 * NA_BLOCK_TOK] = vn_ref[...]
    low_half = lax.broadcasted_iota(jnp.int32, (GRID_W, LANES), 1) < NA_HEAD_DIM
    for j in range(NA_BLOCK_ROWS):
        r = i * NA_BLOCK_ROWS + j
        rs = jnp.clip(r - NA_ROWS // 2, 0, rows - NA_ROWS)
        bias_off = rs - r + NA_ROWS - 1
        start = pl.multiple_of((rs - i * NA_BLOCK_ROWS + NA_BLOCK_ROWS) * GRID_W, GRID_W)
        for p in range(NA_HEADS // 2):
            cols = slice(p * LANES, (p + 1) * LANES)
            qp = q_ref[j * GRID_W:(j + 1) * GRID_W, cols]
            kp = kwin[pl.ds(start, NA_WINDOW_TOK), cols]
            vp = vwin[pl.ds(start, NA_WINDOW_TOK), cols]
            outs = []
            for half in range(2):
                keep = low_half if half == 0 else jnp.logical_not(low_half)
                qm = jnp.where(keep, qp, jnp.zeros_like(qp))
                s = lax.dot_general(qm, kp, _NT, preferred_element_type=F32)
                s = s + bias_ref[bias_off, 2 * p + half]
                m = jnp.max(s, axis=-1, keepdims=True)
                e = jnp.exp(s - m)
                l = jnp.sum(e, axis=-1, keepdims=True)
                outs.append(_dot(e.astype(BF16), vp) * (1.0 / l))
            o_ref[j * GRID_W:(j + 1) * GRID_W, cols] = jnp.where(low_half, outs[0], outs[1]).astype(BF16)


def _na(q, k, v, bias, batch, n):
    rows = n // GRID_W
    nblk = rows // NA_BLOCK_ROWS
    spec = lambda f: pl.BlockSpec((NA_BLOCK_TOK, NA_WIDTH), lambda b, i: (b * nblk + f(i), 0))
    cur = spec(lambda i: i)
    prev = spec(lambda i: jnp.maximum(i - 1, 0))
    nxt = spec(lambda i: jnp.minimum(i + 1, nblk - 1))
    return pl.pallas_call(
        functools.partial(_na_kernel, rows=rows),
        grid=(batch, nblk),
        in_specs=[cur, prev, cur, nxt, prev, cur, nxt, _const_spec(bias.shape)],
        out_specs=cur,
        out_shape=jax.ShapeDtypeStruct((batch * n, NA_WIDTH), BF16),
        scratch_shapes=[pltpu.VMEM((3 * NA_BLOCK_TOK, NA_WIDTH), BF16),
                        pltpu.VMEM((3 * NA_BLOCK_TOK, NA_WIDTH), BF16)],
        compiler_params=pltpu.CompilerParams(dimension_semantics=("parallel", "parallel"),
                                             vmem_limit_bytes=VMEM_LIMIT),
        name="na",
    )(q, k, k, k, v, v, v, bias)


def _na_bias_table(rpb):
    cols = jnp.arange(GRID_W)
    col_start = jnp.clip(cols - NA_COLS // 2, 0, GRID_W - NA_COLS)
    col_mask = (cols[None, :] >= col_start[:, None]) & (cols[None, :] < col_start[:, None] + NA_COLS)
    col_idx = jnp.clip(cols[None, :] - cols[:, None] + NA_COLS - 1, 0, RPB_COLS - 1)
    masked = jnp.where(col_mask[None, None], rpb.astype(F32)[:, :, col_idx], NEG)
    slabs = [masked[:, o:o + NA_ROWS].transpose(0, 2, 1, 3).reshape(NA_HEADS, GRID_W, NA_WINDOW_TOK)
             for o in range(NA_ROWS)]
    return jnp.stack(slabs)


def _gla_chunk(q, k, v, z, whi, wlo, bias, tri, mask, mid, end, st_ref):
    zhi, zlo = _split_bf16(z)
    logits = _dot(zhi, whi) + _dot(zhi, wlo) + _dot(zlo, whi) + bias
    g = (jnp.minimum(logits, 0.0) - jnp.log(1.0 + jnp.exp(-jnp.abs(logits)))) * (1.0 / GLA_GATE_NORM)
    ghi, glo = _split_bf16(g)
    b = _dot(tri, ghi) + _dot(tri, glo)
    b_mid = b[mid:mid + 1, :]
    b_end = b[end:end + 1, :]
    qe = q.astype(F32) * jnp.exp(jnp.minimum(b - b_mid, EXP_CLAMP))
    ke = k.astype(F32) * jnp.exp(jnp.minimum(b_mid - b, EXP_CLAMP))
    a = lax.dot_general(qe.astype(BF16), ke.astype(BF16), _NT, preferred_element_type=F32)
    a = jnp.where(mask, a, 0.0).astype(BF16)
    q_in = (qe * jnp.exp(b_mid)).astype(BF16)
    k_out = (ke * jnp.exp(b_end - b_mid)).astype(BF16)
    st = st_ref[...]
    o = _dot(a, v) + lax.dot_general(q_in, st.astype(BF16), _NT, preferred_element_type=F32)
    st_ref[...] = st * jnp.exp(b_end) + lax.dot_general(v, k_out, _TN, preferred_element_type=F32)
    return o


def _gla_kernel(qf_ref, kf_ref, vf_ref, zf_ref, qr_ref, kr_ref, vr_ref, zr_ref,
                wfh_ref, wfl_ref, wbh_ref, wbl_ref, bf_ref, bb_ref, of_ref, ob_ref, st_ref, *, nsub):
    @pl.when(pl.program_id(2) == 0)
    def _():
        st_ref[...] = jnp.zeros_like(st_ref)

    c = GLA_CHUNK
    ri = lax.broadcasted_iota(jnp.int32, (c, c), 0)
    ci = lax.broadcasted_iota(jnp.int32, (c, c), 1)
    lower = ci <= ri
    upper = ci >= ri
    tri_f = jnp.where(lower, 1.0, 0.0).astype(BF16)
    tri_b = jnp.where(upper, 1.0, 0.0).astype(BF16)
    for s in range(nsub):
        sf = slice(s * c, (s + 1) * c)
        of_ref[sf, :] = _gla_chunk(qf_ref[sf, :], kf_ref[sf, :], vf_ref[sf, :], zf_ref[sf, :],
                                   wfh_ref[0], wfl_ref[0], bf_ref[0], tri_f, lower, c // 2 - 1, c - 1,
                                   st_ref.at[0]).astype(BF16)
        sb = slice((nsub - 1 - s) * c, (nsub - s) * c)
        ob_ref[sb, :] = _gla_chunk(qr_ref[sb, :], kr_ref[sb, :], vr_ref[sb, :], zr_ref[sb, :],
                                   wbh_ref[0], wbl_ref[0], bb_ref[0], tri_b, upper, c // 2, 0,
                                   st_ref.at[1]).astype(BF16)


def _gla(q, k, v, z, wfh, wfl, wbh, wbl, bf, bb, batch, n, cb):
    nb = n // cb
    fwd = lambda w: pl.BlockSpec((cb, w), lambda b, h, c: (b * nb + c, h))
    bwd = lambda w: pl.BlockSpec((cb, w), lambda b, h, c: (b * nb + nb - 1 - c, h))
    zfwd = pl.BlockSpec((cb, LANES), lambda b, h, c: (b * nb + c, 0))
    zbwd = pl.BlockSpec((cb, LANES), lambda b, h, c: (b * nb + nb - 1 - c, 0))
    wspec = pl.BlockSpec((1, LANES, GLA_DK), lambda b, h, c: (h, 0, 0))
    bspec = pl.BlockSpec((1, 1, GLA_DK), lambda b, h, c: (h, 0, 0))
    out = jax.ShapeDtypeStruct((batch * n, GLA_WIDTH), BF16)
    return pl.pallas_call(
        functools.partial(_gla_kernel, nsub=cb // GLA_CHUNK),
        grid=(batch, GLA_HEADS, nb),
        in_specs=[fwd(GLA_DK), fwd(GLA_DK), fwd(GLA_DV), zfwd, bwd(GLA_DK), bwd(GLA_DK), bwd(GLA_DV), zbwd,
                  wspec, wspec, wspec, wspec, bspec, bspec],
        out_specs=[fwd(GLA_DV), bwd(GLA_DV)],
        out_shape=[out, out],
        scratch_shapes=[pltpu.VMEM((2, GLA_DV, GLA_DK), F32)],
        compiler_params=pltpu.CompilerParams(dimension_semantics=("parallel", "parallel", "arbitrary"),
                                             vmem_limit_bytes=VMEM_LIMIT),
        name="gla",
    )(q, k, v, z, q, k, v, z, wfh, wfl, wbh, wbl, bf, bb)


def _gla_decay_weights(w_decay, b_decay):
    def padded(w, row0):
        w = w.reshape(GLA_RANK, GLA_HEADS, GLA_DK).transpose(1, 0, 2)
        full = jnp.zeros((GLA_HEADS, LANES, GLA_DK), F32).at[:, row0:row0 + GLA_RANK].set(w)
        hi = full.astype(BF16)
        return hi, (full - hi.astype(F32)).astype(BF16)
    wfh, wfl = padded(w_decay[0], 0)
    wbh, wbl = padded(w_decay[1], GLA_RANK)
    bias = b_decay.astype(F32).reshape(2, GLA_HEADS, 1, GLA_DK)
    return wfh, wfl, wbh, wbl, bias[0], bias[1]


def _route(logits):
    lane = lax.broadcasted_iota(jnp.int32, logits.shape, 1)
    lane_f = lane.astype(F32)
    ninf = -jnp.inf

    def first_argmax(vals, vmax):
        return jnp.min(jnp.where(vals == vmax, lane_f, float(LANES)), axis=-1, keepdims=True)

    lg = jnp.where(lane < N_GROUPS, logits, ninf)
    g_max = jnp.max(lg, axis=-1, keepdims=True)
    p_sel = 1.0 / jnp.sum(jnp.exp(lg - g_max), axis=-1, keepdims=True)
    grp = first_argmax(lg, g_max)
    e_lo = ROUTER_EXPERT_LANE0 + EXPERTS_PER_GROUP * grp
    le = jnp.where((lane_f >= e_lo) & (lane_f < e_lo + EXPERTS_PER_GROUP), logits, ninf)
    l1 = jnp.max(le, axis=-1, keepdims=True)
    i1 = first_argmax(le, l1)
    le2 = jnp.where(lane_f == i1, ninf, le)
    l2 = jnp.max(le2, axis=-1, keepdims=True)
    i2 = first_argmax(le2, l2)
    t = jnp.exp(l2 - l1)
    w1 = p_sel / (1.0 + t)
    w2 = w1 * t
    return jnp.where(lane_f == i1, w1, jnp.where(lane_f == i2, w2, 0.0))


def _post_kernel(x_ref, ya_ref, of_ref, ob_ref, rb_ref, ga_ref, gb_ref, woa_ref, wob_ref, wout_ref,
                 gn_ref, nf_ref, wrh_ref, wrl_ref, br_ref, x1_ref, xn_ref, comb_ref):
    o = of_ref[...].astype(F32) + ob_ref[...].astype(F32)
    parts = []
    for h in range(GLA_HEADS):
        oh = o[:, h * GLA_DV:(h + 1) * GLA_DV]
        ms = jnp.mean(oh * oh, axis=-1, keepdims=True)
        parts.append(oh * lax.rsqrt(ms + EPS) * gn_ref[...])
    on = jnp.concatenate(parts, axis=-1) * rb_ref[...].astype(F32)
    yb = _dot(on.astype(BF16), wob_ref[...])
    ya = _dot(ya_ref[...], woa_ref[...])
    y = ga_ref[...].astype(F32) * ya + gb_ref[...].astype(F32) * yb
    x1 = x_ref[...] + _dot(y.astype(BF16), wout_ref[...])
    x1_ref[...] = x1
    ms = jnp.mean(x1 * x1, axis=-1, keepdims=True)
    t = x1 * lax.rsqrt(ms + EPS) * nf_ref[...]
    thi, tlo = _split_bf16(t)
    xn_ref[...] = thi
    logits = _dot(thi, wrh_ref[...]) + _dot(thi, wrl_ref[...]) + _dot(tlo, wrh_ref[...]) + br_ref[...]
    comb_ref[...] = _route(logits)


def _post(x, ya, of, ob, rb, ga, gb, woa, wob, wout, gn, nf, wrh, wrl, br, tm):
    t = x.shape[0]
    row = lambda w: pl.BlockSpec((tm, w), lambda i: (i, 0))
    consts = [woa, wob, wout, gn, nf, wrh, wrl, br]
    return pl.pallas_call(
        _post_kernel,
        grid=(t // tm,),
        in_specs=[row(D_MODEL), row(NA_WIDTH), row(GLA_WIDTH), row(GLA_WIDTH), row(GLA_WIDTH),
                  row(D_MODEL), row(D_MODEL)] + [_const_spec(c.shape) for c in consts],
        out_specs=[row(D_MODEL), row(D_MODEL), row(LANES)],
        out_shape=[jax.ShapeDtypeStruct((t, D_MODEL), F32), jax.ShapeDtypeStruct((t, D_MODEL), BF16),
                   jax.ShapeDtypeStruct((t, LANES), F32)],
        compiler_params=pltpu.CompilerParams(dimension_semantics=("parallel",), vmem_limit_bytes=VMEM_LIMIT),
        name="post",
    )(x, ya, of, ob, rb, ga, gb, *consts)


def _moe_kernel(xn_ref, comb_ref, x1_ref, wg_ref, wu_ref, wd_ref, o_ref, acc_ref):
    e = pl.program_id(1)

    @pl.when(e == 0)
    def _():
        acc_ref[...] = jnp.zeros_like(acc_ref)

    xn = xn_ref[...]
    hg = _dot(xn, wg_ref[0])
    hu = _dot(xn, wu_ref[0])
    lane = lax.broadcasted_iota(jnp.int32, comb_ref.shape, 1)
    w = jnp.sum(jnp.where(lane == e + ROUTER_EXPERT_LANE0, comb_ref[...], 0.0), axis=-1, keepdims=True)
    h = hg * _sigmoid(hg) * hu * w
    acc_ref[...] += _dot(h.astype(BF16), wd_ref[0])

    @pl.when(e == N_EXPERTS - 1)
    def _():
        o_ref[...] = x1_ref[...] + acc_ref[...]


def _moe(xn, comb, x1, wg, wu, wd, tm):
    t = xn.shape[0]
    row = lambda w: pl.BlockSpec((tm, w), lambda i, e: (i, 0))
    return pl.pallas_call(
        _moe_kernel,
        grid=(t // tm, N_EXPERTS),
        in_specs=[row(D_MODEL), row(LANES), row(D_MODEL),
                  pl.BlockSpec((1, D_MODEL, D_EXPERT), lambda i, e: (e, 0, 0)),
                  pl.BlockSpec((1, D_MODEL, D_EXPERT), lambda i, e: (e, 0, 0)),
                  pl.BlockSpec((1, D_EXPERT, D_MODEL), lambda i, e: (e, 0, 0))],
        out_specs=row(D_MODEL),
        out_shape=jax.ShapeDtypeStruct((t, D_MODEL), F32),
        scratch_shapes=[pltpu.VMEM((tm, D_MODEL), F32)],
        compiler_params=pltpu.CompilerParams(dimension_semantics=("parallel", "arbitrary"),
                                             vmem_limit_bytes=VMEM_LIMIT),
        name="moe",
    )(xn, comb, x1, wg, wu, wd)


def _prep_layer(l, norm_mix, w_in, q_norm_a, k_norm_a, rpb, w_decay, b_decay, gla_norm, w_o_a, w_o_b, w_out,
                norm_ffn, w_router_g, b_router_g, w_router_e, b_router_e, w_gate, w_up, w_down):
    w = w_in[l]
    wz = jnp.zeros((D_MODEL, LANES), F32).at[:, :2 * GLA_RANK].set(w[:, MAIN_WIDTH:MAIN_WIDTH + 2 * GLA_RANK])
    head = jnp.arange(NA_WIDTH) // NA_HEAD_DIM
    w_router = jnp.zeros((D_MODEL, LANES), F32)
    w_router = w_router.at[:, ROUTER_GROUP_LANE0:ROUTER_GROUP_LANE0 + N_GROUPS].set(w_router_g[l])
    w_router = w_router.at[:, ROUTER_EXPERT_LANE0:ROUTER_EXPERT_LANE0 + N_EXPERTS].set(w_router_e[l])
    b_router = jnp.zeros((1, LANES), F32)
    b_router = b_router.at[0, ROUTER_GROUP_LANE0:ROUTER_GROUP_LANE0 + N_GROUPS].set(b_router_g[l])
    b_router = b_router.at[0, ROUTER_EXPERT_LANE0:ROUTER_EXPERT_LANE0 + N_EXPERTS].set(b_router_e[l])
    wrh, wrl = _split_bf16(w_router)
    return dict(
        nrm=norm_mix[l].reshape(1, D_MODEL),
        wm=w[:, :MAIN_WIDTH].astype(BF16),
        wz=wz.astype(BF16),
        wg=w[:, MAIN_WIDTH + 2 * GLA_RANK:].astype(BF16),
        qg=(jnp.tile(q_norm_a[l], NA_HEADS) * (NA_HEAD_DIM ** -0.5)).reshape(1, NA_WIDTH),
        kg=jnp.tile(k_norm_a[l], NA_HEADS).reshape(1, NA_WIDTH),
        hsum=(head[:, None] == head[None, :]).astype(BF16),
        bias=_na_bias_table(rpb[l]),
        decay=_gla_decay_weights(w_decay[l], b_decay[l]),
        woa=w_o_a[l].astype(BF16), wob=w_o_b[l].astype(BF16), wout=w_out[l].astype(BF16),
        gn=gla_norm[l].reshape(1, GLA_DV), nf=norm_ffn[l].reshape(1, D_MODEL),
        wrh=wrh, wrl=wrl, br=b_router,
        wgate=w_gate[l].astype(BF16), wup=w_up[l].astype(BF16), wdown=w_down[l].astype(BF16),
    )


def _tile(t, want):
    while t % want:
        want //= 2
    return want


def _trunk(x, layers):
    batch, n, _ = x.shape
    t = batch * n
    x = x.reshape(t, D_MODEL)
    for p in layers:
        qa, ka, va, qb, kb, vb, rb, z, ga, gb = _in_proj(x, p["nrm"], p["wm"], p["wz"], p["wg"], p["qg"], p["kg"],
                                                         p["hsum"], _tile(t, 512))
        ya = _na(qa, ka, va, p["bias"], batch, n)
        of, ob = _gla(qb, kb, vb, z, *p["decay"], batch, n, _tile(n, 256))
        x1, xn, comb = _post(x, ya, of, ob, rb, ga, gb, p["woa"], p["wob"], p["wout"], p["gn"], p["nf"],
                             p["wrh"], p["wrl"], p["br"], _tile(t, 512))
        x = _moe(xn, comb, x1, p["wgate"], p["wup"], p["wdown"], _tile(t, 1024))
    return x.reshape(batch, n, D_MODEL)


def kernel(x_prompt, x_sample, norm_mix, w_in, q_norm_a, k_norm_a, rpb, w_decay, b_decay, gla_norm, w_o_a, w_o_b,
           w_out, norm_ffn, w_router_g, b_router_g, w_router_e, b_router_e, w_gate, w_up, w_down):
    depth = w_in.shape[0]
    layers = [_prep_layer(l, norm_mix, w_in, q_norm_a, k_norm_a, rpb, w_decay, b_decay, gla_norm, w_o_a, w_o_b,
                          w_out, norm_ffn, w_router_g, b_router_g, w_router_e, b_router_e, w_gate, w_up, w_down)
              for l in range(depth)]
    return (_trunk(x_prompt, layers), _trunk(x_sample, layers))
```

```python
import functools

import jax
import jax.numpy as jnp
from jax import lax
from jax.experimental import pallas as pl
from jax.experimental.pallas import tpu as pltpu

F32 = jnp.float32
BF16 = jnp.bfloat16

D_MODEL = 1024
GRID_W = 64
NA_HEADS = 8
NA_HEAD_DIM = 64
NA_WIDTH = NA_HEADS * NA_HEAD_DIM
NA_ROWS = 8
NA_COLS = 16
RPB_ROWS = 2 * NA_ROWS - 1
RPB_COLS = 2 * NA_COLS - 1
GLA_HEADS = 4
GLA_DK = 128
GLA_DV = 128
GLA_WIDTH = GLA_HEADS * GLA_DV
GLA_RANK = 16
GLA_GATE_NORM = 16.0
GLA_CHUNK = 64
GLA_BLOCK_TOK = 256
N_GROUPS = 4
EXPERTS_PER_GROUP = 4
N_EXPERTS = N_GROUPS * EXPERTS_PER_GROUP
D_EXPERT = 512
EPS = 1e-6
NEG = -1e30

LANES = 128
MAIN_WIDTH = 7 * 512
LOG2E = 1.4426950408889634
EXP_CLAMP = 80.0
ROUTER_GROUP_LANE0 = 0
ROUTER_EXPERT_LANE0 = N_GROUPS
VMEM_LIMIT = 56 * 1024 * 1024

_NT = (((1,), (1,)), ((), ()))
_TN = (((0,), (0,)), ((), ()))


def _dot(a, b):
    return jnp.dot(a, b, preferred_element_type=F32)


def _split_bf16(x):
    hi = x.astype(BF16)
    lo = (x - hi.astype(F32)).astype(BF16)
    return hi, lo


def _sigmoid(x):
    return 1.0 / (1.0 + jnp.exp(-x))


def _const_spec(shape):
    nd = len(shape)
    return pl.BlockSpec(shape, lambda *_: (0,) * nd, pipeline_mode=pl.Buffered(1))


def _in_proj_kernel(x_ref, nrm_ref, wm_ref, wz_ref, wg_ref, qg_ref, kg_ref, hsum_ref,
                    qa_ref, ka_ref, va_ref, qb_ref, kb_ref, vb_ref, rb_ref, z_ref, ga_ref, gb_ref):
    x = x_ref[...]
    ms = jnp.mean(x * x, axis=-1, keepdims=True)
    xn = (x * lax.rsqrt(ms + EPS) * nrm_ref[...]).astype(BF16)

    def proj(k):
        return _dot(xn, wm_ref[:, k * 512:(k + 1) * 512])

    def head_norm(a, gain_ref):
        ss = _dot((a * a).astype(BF16), hsum_ref[...])
        return a * lax.rsqrt(ss * (1.0 / NA_HEAD_DIM) + EPS) * gain_ref[...]

    qa_ref[...] = head_norm(proj(0), qg_ref).astype(BF16)
    ka_ref[...] = head_norm(proj(1), kg_ref).astype(BF16)
    va_ref[...] = proj(2).astype(BF16)
    qb_ref[...] = (proj(3) * (GLA_DK ** -0.5)).astype(BF16)
    kb_ref[...] = proj(4).astype(BF16)
    vb_ref[...] = proj(5).astype(BF16)
    r = proj(6)
    rb_ref[...] = (r * _sigmoid(r)).astype(BF16)
    z_ref[...] = _dot(xn, wz_ref[...])
    ga_ref[...] = _sigmoid(_dot(xn, wg_ref[:, :D_MODEL])).astype(BF16)
    gb_ref[...] = _sigmoid(_dot(xn, wg_ref[:, D_MODEL:])).astype(BF16)


def _in_proj(x, nrm, wm, wz, wg, qg, kg, hsum, tm):
    t = x.shape[0]
    row = lambda w: pl.BlockSpec((tm, w), lambda i: (i, 0))
    out_w = [512] * 7 + [LANES, D_MODEL, D_MODEL]
    out_dt = [BF16] * 7 + [F32, BF16, BF16]
    return pl.pallas_call(
        _in_proj_kernel,
        grid=(t // tm,),
        in_specs=[row(D_MODEL), _const_spec(nrm.shape), _const_spec(wm.shape), _const_spec(wz.shape),
                  _const_spec(wg.shape), _const_spec(qg.shape), _const_spec(kg.shape), _const_spec(hsum.shape)],
        out_specs=[row(w) for w in out_w],
        out_shape=[jax.ShapeDtypeStruct((t, w), dt) for w, dt in zip(out_w, out_dt)],
        compiler_params=pltpu.CompilerParams(dimension_semantics=("parallel",), vmem_limit_bytes=VMEM_LIMIT),
        name="in_proj",
    )(x, nrm, wm, wz, wg, qg, kg, hsum)


NA_BLOCK_ROWS = 8
NA_BLOCK_TOK = NA_BLOCK_ROWS * GRID_W
NA_WINDOW_TOK = NA_ROWS * GRID_W


def _na_kernel(q_ref, kp_ref, kc_ref, kn_ref, vp_ref, vc_ref, vn_ref, bias_ref, o_ref, kwin, vwin, s_scr, *, rows):
    i = pl.program_id(1)
    kwin[0:NA_BLOCK_TOK] = kp_ref[...]
    kwin[NA_BLOCK_TOK:2 * NA_BLOCK_TOK] = kc_ref[...]
    kwin[2 * NA_BLOCK_TOK:3 * NA_BLOCK_TOK] = kn_ref[...]
    vwin[0:NA_BLOCK_TOK] = vp_ref[...]
    vwin[NA_BLOCK_TOK:2 * NA_BLOCK_TOK] = vc_ref[...]
    vwin[2 * NA_BLOCK_TOK:3 * NA_BLOCK_TOK] = vn_ref[...]
    low_half = lax.broadcasted_iota(jnp.int32, (GRID_W, LANES), 1) < NA_HEAD_DIM
    pairs = range(NA_HEADS // 2)
    cols = lambda p: slice(p * LANES, (p + 1) * LANES)
    qrows = lambda j: slice(j * GRID_W, (j + 1) * GRID_W)

    def window(j):
        r = i * NA_BLOCK_ROWS + j
        rs = jnp.clip(r - NA_ROWS // 2, 0, rows - NA_ROWS)
        start = pl.multiple_of((rs - i * NA_BLOCK_ROWS + NA_BLOCK_ROWS) * GRID_W, GRID_W)
        return start, rs - r + NA_ROWS - 1

    def scores(j):
        start, bias_off = window(j)
        for p in pairs:
            qp = q_ref[qrows(j), cols(p)]
            zero = jnp.zeros_like(qp)
            q2 = jnp.concatenate([jnp.where(low_half, qp, zero), jnp.where(low_half, zero, qp)], axis=0)
            s = lax.dot_general(q2, kwin[pl.ds(start, NA_WINDOW_TOK), cols(p)], _NT, preferred_element_type=F32)
            s_scr[j % 2, p] = s + bias_ref[bias_off, p]

    def attend(j):
        start, _ = window(j)
        for p in pairs:
            s = s_scr[j % 2, p]
            e = jnp.exp2(s - jnp.max(s, axis=-1, keepdims=True))
            l = jnp.sum(e, axis=-1, keepdims=True)
            pv = _dot(e.astype(BF16), vwin[pl.ds(start, NA_WINDOW_TOK), cols(p)]) * (1.0 / l)
            o_ref[qrows(j), cols(p)] = jnp.where(low_half, pv[:GRID_W], pv[GRID_W:]).astype(BF16)

    scores(0)
    for j in range(NA_BLOCK_ROWS):
        if j + 1 < NA_BLOCK_ROWS:
            scores(j + 1)
        attend(j)


def _na(q, k, v, bias, batch, n):
    rows = n // GRID_W
    nblk = rows // NA_BLOCK_ROWS
    spec = lambda f: pl.BlockSpec((NA_BLOCK_TOK, NA_WIDTH), lambda b, i: (b * nblk + f(i), 0))
    cur = spec(lambda i: i)
    prev = spec(lambda i: jnp.maximum(i - 1, 0))
    nxt = spec(lambda i: jnp.minimum(i + 1, nblk - 1))
    return pl.pallas_call(
        functools.partial(_na_kernel, rows=rows),
        grid=(batch, nblk),
        in_specs=[cur, prev, cur, nxt, prev, cur, nxt, _const_spec(bias.shape)],
        out_specs=cur,
        out_shape=jax.ShapeDtypeStruct((batch * n, NA_WIDTH), BF16),
        scratch_shapes=[pltpu.VMEM((3 * NA_BLOCK_TOK, NA_WIDTH), BF16),
                        pltpu.VMEM((3 * NA_BLOCK_TOK, NA_WIDTH), BF16),
                        pltpu.VMEM((2, NA_HEADS // 2, 2 * GRID_W, NA_WINDOW_TOK), F32)],
        compiler_params=pltpu.CompilerParams(dimension_semantics=("parallel", "parallel"),
                                             vmem_limit_bytes=VMEM_LIMIT),
        name="na",
    )(q, k, k, k, v, v, v, bias)


def _na_bias_table(rpb):
    cols = jnp.arange(GRID_W)
    col_start = jnp.clip(cols - NA_COLS // 2, 0, GRID_W - NA_COLS)
    col_mask = (cols[None, :] >= col_start[:, None]) & (cols[None, :] < col_start[:, None] + NA_COLS)
    col_idx = jnp.clip(cols[None, :] - cols[:, None] + NA_COLS - 1, 0, RPB_COLS - 1)
    masked = jnp.where(col_mask[None, None], rpb.astype(F32)[:, :, col_idx] * LOG2E, NEG)
    slabs = [masked[:, o:o + NA_ROWS].transpose(0, 2, 1, 3).reshape(NA_HEADS // 2, 2 * GRID_W, NA_WINDOW_TOK)
             for o in range(NA_ROWS)]
    return jnp.stack(slabs)


def _gla_block_prepare(q_ref, k_ref, v_ref, z_ref, wh_ref, wl_ref, bias_ref, tri_ref, mask, mid, end,
                       oi_s, qi_s, kv_s, dec_s, nsub):
    zhi, zlo = _split_bf16(z_ref[...])
    logits = _dot(zhi, wh_ref[...]) + _dot(zhi, wl_ref[...]) + _dot(zlo, wh_ref[...]) + bias_ref[...]
    g = (jnp.minimum(logits, 0.0) - jnp.log(1.0 + jnp.exp(-jnp.abs(logits)))) * (1.0 / GLA_GATE_NORM)
    ghi, glo = _split_bf16(g)
    b_all = _dot(tri_ref[...], ghi) + _dot(tri_ref[...], glo)
    pairs = [(s, h) for s in range(nsub) for h in range(GLA_HEADS)]
    rows = lambda s: slice(s * GLA_CHUNK, (s + 1) * GLA_CHUNK)
    cols = lambda h: slice(h * GLA_DK, (h + 1) * GLA_DK)
    qe16, ke16, k_out = [], [], []
    for s in range(nsub):
        b = b_all[rows(s)]
        b_mid = b[mid:mid + 1, :]
        b_end = b[end:end + 1, :]
        qe = q_ref[rows(s), :].astype(F32) * jnp.exp(jnp.minimum(b - b_mid, EXP_CLAMP))
        ke = k_ref[rows(s), :].astype(F32) * jnp.exp(jnp.minimum(b_mid - b, EXP_CLAMP))
        qe16.append(qe.astype(BF16))
        ke16.append(ke.astype(BF16))
        qi_s[rows(s), :] = (qe * jnp.exp(b_mid)).astype(BF16)
        k_out.append((ke * jnp.exp(b_end - b_mid)).astype(BF16))
        dec_s[s] = jnp.broadcast_to(jnp.exp(b_end), (8, GLA_WIDTH))
    a = [lax.dot_general(qe16[s][:, cols(h)], ke16[s][:, cols(h)], _NT, preferred_element_type=F32)
         for s, h in pairs]
    for s, h in pairs:
        kv_s[s, h] = lax.dot_general(v_ref[rows(s), cols(h)], k_out[s][:, cols(h)], _TN,
                                     preferred_element_type=F32)
    for (s, h), a_sh in zip(pairs, a):
        oi_s[rows(s), cols(h)] = _dot(jnp.where(mask, a_sh, 0.0).astype(BF16), v_ref[rows(s), cols(h)])


def _gla_block_scan(steps, oi_s, qi_s, kv_s, dec_s):
    for s, o_ref, st_ref, d in steps:
        rows = slice(s * GLA_CHUNK, (s + 1) * GLA_CHUNK)
        dec = dec_s[d][s]
        states = [st_ref[h] for h in range(GLA_HEADS)]
        inter = [lax.dot_general(qi_s[d][rows, h * GLA_DK:(h + 1) * GLA_DK], states[h].astype(BF16), _NT,
                                 preferred_element_type=F32) for h in range(GLA_HEADS)]
        for h in range(GLA_HEADS):
            cols = slice(h * GLA_DK, (h + 1) * GLA_DK)
            st_ref[h] = states[h] * dec[0:1, cols] + kv_s[d][s, h]
            o_ref[rows, cols] = (oi_s[d][rows, cols] + inter[h]).astype(BF16)


def _gla_kernel(qf_ref, kf_ref, vf_ref, zf_ref, qr_ref, kr_ref, vr_ref, zr_ref,
                wfh_ref, wfl_ref, wbh_ref, wbl_ref, bf_ref, bb_ref, trif_ref, trib_ref, of_ref, ob_ref,
                stf_ref, stb_ref, oif_s, oib_s, qif_s, qib_s, kvf_s, kvb_s, decf_s, decb_s, *, nsub):
    @pl.when(pl.program_id(1) == 0)
    def _():
        stf_ref[...] = jnp.zeros_like(stf_ref)
        stb_ref[...] = jnp.zeros_like(stb_ref)

    c = GLA_CHUNK
    ri = lax.broadcasted_iota(jnp.int32, (c, c), 0)
    ci = lax.broadcasted_iota(jnp.int32, (c, c), 1)
    _gla_block_prepare(qf_ref, kf_ref, vf_ref, zf_ref, wfh_ref, wfl_ref, bf_ref, trif_ref, ci <= ri,
                       c // 2 - 1, c - 1, oif_s, qif_s, kvf_s, decf_s, nsub)
    _gla_block_prepare(qr_ref, kr_ref, vr_ref, zr_ref, wbh_ref, wbl_ref, bb_ref, trib_ref, ci >= ri,
                       c // 2, 0, oib_s, qib_s, kvb_s, decb_s, nsub)
    steps = []
    for s in range(nsub):
        steps += [(s, of_ref, stf_ref, 0), (nsub - 1 - s, ob_ref, stb_ref, 1)]
    _gla_block_scan(steps, (oif_s, oib_s), (qif_s, qib_s), (kvf_s, kvb_s), (decf_s, decb_s))


def _gla(q, k, v, z, wfh, wfl, wbh, wbl, bf, bb, trif, trib, batch, n, cb):
    nb = n // cb
    nsub = cb // GLA_CHUNK
    fwd = lambda w: pl.BlockSpec((cb, w), lambda b, c: (b * nb + c, 0))
    bwd = lambda w: pl.BlockSpec((cb, w), lambda b, c: (b * nb + nb - 1 - c, 0))
    consts = [wfh, wfl, wbh, wbl, bf, bb, trif, trib]
    out = jax.ShapeDtypeStruct((batch * n, GLA_WIDTH), BF16)
    state = pltpu.VMEM((GLA_HEADS, GLA_DV, GLA_DK), F32)
    return pl.pallas_call(
        functools.partial(_gla_kernel, nsub=nsub),
        grid=(batch, nb),
        in_specs=[fwd(GLA_WIDTH), fwd(GLA_WIDTH), fwd(GLA_WIDTH), fwd(LANES),
                  bwd(GLA_WIDTH), bwd(GLA_WIDTH), bwd(GLA_WIDTH), bwd(LANES)] + [_const_spec(c.shape) for c in consts],
        out_specs=[fwd(GLA_WIDTH), bwd(GLA_WIDTH)],
        out_shape=[out, out],
        scratch_shapes=[state, state,
                        pltpu.VMEM((cb, GLA_WIDTH), F32), pltpu.VMEM((cb, GLA_WIDTH), F32),
                        pltpu.VMEM((cb, GLA_WIDTH), BF16), pltpu.VMEM((cb, GLA_WIDTH), BF16),
                        pltpu.VMEM((nsub, GLA_HEADS, GLA_DV, GLA_DK), F32),
                        pltpu.VMEM((nsub, GLA_HEADS, GLA_DV, GLA_DK), F32),
                        pltpu.VMEM((nsub, 8, GLA_WIDTH), F32), pltpu.VMEM((nsub, 8, GLA_WIDTH), F32)],
        compiler_params=pltpu.CompilerParams(dimension_semantics=("parallel", "arbitrary"),
                                             vmem_limit_bytes=VMEM_LIMIT),
        name="gla",
    )(q, k, v, z, q, k, v, z, *consts)


def _gla_decay_weights(w_decay, b_decay, cb):
    def padded(w, row0):
        full = jnp.zeros((LANES, GLA_WIDTH), F32).at[row0:row0 + GLA_RANK].set(w.astype(F32))
        return _split_bf16(full)
    wfh, wfl = padded(w_decay[0], 0)
    wbh, wbl = padded(w_decay[1], GLA_RANK)
    bias = b_decay.astype(F32).reshape(2, 1, GLA_WIDTH)
    tok = jnp.arange(cb)
    same = (tok[:, None] // GLA_CHUNK) == (tok[None, :] // GLA_CHUNK)
    trif = (same & (tok[None, :] <= tok[:, None])).astype(BF16)
    trib = (same & (tok[None, :] >= tok[:, None])).astype(BF16)
    return wfh, wfl, wbh, wbl, bias[0], bias[1], trif, trib


def _route(logits):
    lane = lax.broadcasted_iota(jnp.int32, logits.shape, 1)
    lane_f = lane.astype(F32)
    ninf = -jnp.inf

    def first_argmax(vals, vmax):
        return jnp.min(jnp.where(vals == vmax, lane_f, float(LANES)), axis=-1, keepdims=True)

    lg = jnp.where(lane < N_GROUPS, logits, ninf)
    g_max = jnp.max(lg, axis=-1, keepdims=True)
    p_sel = 1.0 / jnp.sum(jnp.exp(lg - g_max), axis=-1, keepdims=True)
    grp = first_argmax(lg, g_max)
    e_lo = ROUTER_EXPERT_LANE0 + EXPERTS_PER_GROUP * grp
    le = jnp.where((lane_f >= e_lo) & (lane_f < e_lo + EXPERTS_PER_GROUP), logits, ninf)
    l1 = jnp.max(le, axis=-1, keepdims=True)
    i1 = first_argmax(le, l1)
    le2 = jnp.where(lane_f == i1, ninf, le)
    l2 = jnp.max(le2, axis=-1, keepdims=True)
    i2 = first_argmax(le2, l2)
    t = jnp.exp(l2 - l1)
    w1 = p_sel / (1.0 + t)
    w2 = w1 * t
    return jnp.where(lane_f == i1, w1, jnp.where(lane_f == i2, w2, 0.0))


def _post_kernel(x_ref, ya_ref, of_ref, ob_ref, rb_ref, ga_ref, gb_ref, woa_ref, wob_ref, wout_ref,
                 gn_ref, nf_ref, wrh_ref, wrl_ref, br_ref, x1_ref, xn_ref, comb_ref):
    o = of_ref[...].astype(F32) + ob_ref[...].astype(F32)
    parts = []
    for h in range(GLA_HEADS):
        oh = o[:, h * GLA_DV:(h + 1) * GLA_DV]
        ms = jnp.mean(oh * oh, axis=-1, keepdims=True)
        parts.append(oh * lax.rsqrt(ms + EPS) * gn_ref[...])
    on = jnp.concatenate(parts, axis=-1) * rb_ref[...].astype(F32)
    yb = _dot(on.astype(BF16), wob_ref[...])
    ya = _dot(ya_ref[...], woa_ref[...])
    y = ga_ref[...].astype(F32) * ya + gb_ref[...].astype(F32) * yb
    x1 = x_ref[...] + _dot(y.astype(BF16), wout_ref[...])
    x1_ref[...] = x1
    ms = jnp.mean(x1 * x1, axis=-1, keepdims=True)
    t = x1 * lax.rsqrt(ms + EPS) * nf_ref[...]
    thi, tlo = _split_bf16(t)
    xn_ref[...] = thi
    logits = _dot(thi, wrh_ref[...]) + _dot(thi, wrl_ref[...]) + _dot(tlo, wrh_ref[...]) + br_ref[...]
    comb_ref[...] = _route(logits)


def _post(x, ya, of, ob, rb, ga, gb, woa, wob, wout, gn, nf, wrh, wrl, br, tm):
    t = x.shape[0]
    row = lambda w: pl.BlockSpec((tm, w), lambda i: (i, 0))
    consts = [woa, wob, wout, gn, nf, wrh, wrl, br]
    return pl.pallas_call(
        _post_kernel,
        grid=(t // tm,),
        in_specs=[row(D_MODEL), row(NA_WIDTH), row(GLA_WIDTH), row(GLA_WIDTH), row(GLA_WIDTH),
                  row(D_MODEL), row(D_MODEL)] + [_const_spec(c.shape) for c in consts],
        out_specs=[row(D_MODEL), row(D_MODEL), row(LANES)],
        out_shape=[jax.ShapeDtypeStruct((t, D_MODEL), F32), jax.ShapeDtypeStruct((t, D_MODEL), BF16),
                   jax.ShapeDtypeStruct((t, LANES), F32)],
        compiler_params=pltpu.CompilerParams(dimension_semantics=("parallel",), vmem_limit_bytes=VMEM_LIMIT),
        name="post",
    )(x, ya, of, ob, rb, ga, gb, *consts)


def _moe_kernel(xn_ref, comb_ref, x1_ref, wg_ref, wu_ref, wd_ref, o_ref, acc_ref):
    e = pl.program_id(1)

    @pl.when(e == 0)
    def _():
        acc_ref[...] = jnp.zeros_like(acc_ref)

    xn = xn_ref[...]
    hg = _dot(xn, wg_ref[0])
    hu = _dot(xn, wu_ref[0])
    lane = lax.broadcasted_iota(jnp.int32, comb_ref.shape, 1)
    w = jnp.sum(jnp.where(lane == e + ROUTER_EXPERT_LANE0, comb_ref[...], 0.0), axis=-1, keepdims=True)
    h = hg * _sigmoid(hg) * hu * w
    acc_ref[...] += _dot(h.astype(BF16), wd_ref[0])

    @pl.when(e == N_EXPERTS - 1)
    def _():
        o_ref[...] = x1_ref[...] + acc_ref[...]


def _moe(xn, comb, x1, wg, wu, wd, tm):
    t = xn.shape[0]
    row = lambda w: pl.BlockSpec((tm, w), lambda i, e: (i, 0))
    return pl.pallas_call(
        _moe_kernel,
        grid=(t // tm, N_EXPERTS),
        in_specs=[row(D_MODEL), row(LANES), row(D_MODEL),
                  pl.BlockSpec((1, D_MODEL, D_EXPERT), lambda i, e: (e, 0, 0)),
                  pl.BlockSpec((1, D_MODEL, D_EXPERT), lambda i, e: (e, 0, 0)),
                  pl.BlockSpec((1, D_EXPERT, D_MODEL), lambda i, e: (e, 0, 0))],
        out_specs=row(D_MODEL),
        out_shape=jax.ShapeDtypeStruct((t, D_MODEL), F32),
        scratch_shapes=[pltpu.VMEM((tm, D_MODEL), F32)],
        compiler_params=pltpu.CompilerParams(dimension_semantics=("parallel", "arbitrary"),
                                             vmem_limit_bytes=VMEM_LIMIT),
        name="moe",
    )(xn, comb, x1, wg, wu, wd)


def _prep_layer(l, norm_mix, w_in, q_norm_a, k_norm_a, rpb, w_decay, b_decay, gla_norm, w_o_a, w_o_b, w_out,
                norm_ffn, w_router_g, b_router_g, w_router_e, b_router_e, w_gate, w_up, w_down):
    w = w_in[l]
    wz = jnp.zeros((D_MODEL, LANES), F32).at[:, :2 * GLA_RANK].set(w[:, MAIN_WIDTH:MAIN_WIDTH + 2 * GLA_RANK])
    head = jnp.arange(NA_WIDTH) // NA_HEAD_DIM
    w_router = jnp.zeros((D_MODEL, LANES), F32)
    w_router = w_router.at[:, ROUTER_GROUP_LANE0:ROUTER_GROUP_LANE0 + N_GROUPS].set(w_router_g[l])
    w_router = w_router.at[:, ROUTER_EXPERT_LANE0:ROUTER_EXPERT_LANE0 + N_EXPERTS].set(w_router_e[l])
    b_router = jnp.zeros((1, LANES), F32)
    b_router = b_router.at[0, ROUTER_GROUP_LANE0:ROUTER_GROUP_LANE0 + N_GROUPS].set(b_router_g[l])
    b_router = b_router.at[0, ROUTER_EXPERT_LANE0:ROUTER_EXPERT_LANE0 + N_EXPERTS].set(b_router_e[l])
    wrh, wrl = _split_bf16(w_router)
    return dict(
        nrm=norm_mix[l].reshape(1, D_MODEL),
        wm=w[:, :MAIN_WIDTH].astype(BF16),
        wz=wz.astype(BF16),
        wg=w[:, MAIN_WIDTH + 2 * GLA_RANK:].astype(BF16),
        qg=(jnp.tile(q_norm_a[l], NA_HEADS) * (NA_HEAD_DIM ** -0.5 * LOG2E)).reshape(1, NA_WIDTH),
        kg=jnp.tile(k_norm_a[l], NA_HEADS).reshape(1, NA_WIDTH),
        hsum=(head[:, None] == head[None, :]).astype(BF16),
        bias=_na_bias_table(rpb[l]),
        decay=_gla_decay_weights(w_decay[l], b_decay[l], GLA_BLOCK_TOK),
        woa=w_o_a[l].astype(BF16), wob=w_o_b[l].astype(BF16), wout=w_out[l].astype(BF16),
        gn=gla_norm[l].reshape(1, GLA_DV), nf=norm_ffn[l].reshape(1, D_MODEL),
        wrh=wrh, wrl=wrl, br=b_router,
        wgate=w_gate[l].astype(BF16), wup=w_up[l].astype(BF16), wdown=w_down[l].astype(BF16),
    )


def _tile(t, want):
    while t % want:
        want //= 2
    return want


def _trunk(x, layers):
    batch, n, _ = x.shape
    t = batch * n
    x = x.reshape(t, D_MODEL)
    for p in layers:
        qa, ka, va, qb, kb, vb, rb, z, ga, gb = _in_proj(x, p["nrm"], p["wm"], p["wz"], p["wg"], p["qg"], p["kg"],
                                                         p["hsum"], _tile(t, 512))
        ya = _na(qa, ka, va, p["bias"], batch, n)
        of, ob = _gla(qb, kb, vb, z, *p["decay"], batch, n, GLA_BLOCK_TOK)
        x1, xn, comb = _post(x, ya, of, ob, rb, ga, gb, p["woa"], p["wob"], p["wout"], p["gn"], p["nf"],
                             p["wrh"], p["wrl"], p["br"], _tile(t, 512))
        x = _moe(xn, comb, x1, p["wgate"], p["wup"], p["wdown"], _tile(t, 1024))
    return x.reshape(batch, n, D_MODEL)


def kernel(x_prompt, x_sample, norm_mix, w_in, q_norm_a, k_norm_a, rpb, w_decay, b_decay, gla_norm, w_o_a, w_o_b,
           w_out, norm_ffn, w_router_g, b_router_g, w_router_e, b_router_e, w_gate, w_up, w_down):
    depth = w_in.shape[0]
    layers = [_prep_layer(l, norm_mix, w_in, q_norm_a, k_norm_a, rpb, w_decay, b_decay, gla_norm, w_o_a, w_o_b,
                          w_out, norm_ffn, w_router_g, b_router_g, w_router_e, b_router_e, w_gate, w_up, w_down)
              for l in range(depth)]
    return (_trunk(x_prompt, layers), _trunk(x_sample, layers))
```

```python
import functools

import jax
import jax.numpy as jnp
from jax import lax
from jax.experimental import pallas as pl
from jax.experimental.pallas import tpu as pltpu

F32 = jnp.float32
BF16 = jnp.bfloat16

D_MODEL = 1024
GRID_W = 64
NA_HEADS = 8
NA_HEAD_DIM = 64
NA_WIDTH = NA_HEADS * NA_HEAD_DIM
NA_ROWS = 8
NA_COLS = 16
RPB_ROWS = 2 * NA_ROWS - 1
RPB_COLS = 2 * NA_COLS - 1
GLA_HEADS = 4
GLA_DK = 128
GLA_DV = 128
GLA_WIDTH = GLA_HEADS * GLA_DV
GLA_RANK = 16
GLA_GATE_NORM = 16.0
GLA_CHUNK = 64
GLA_BLOCK_TOK = 256
N_GROUPS = 4
EXPERTS_PER_GROUP = 4
N_EXPERTS = N_GROUPS * EXPERTS_PER_GROUP
D_EXPERT = 512
EPS = 1e-6
NEG = -1e30

LANES = 128
MAIN_WIDTH = 7 * 512
MOE_SUB = 256
MOE_GRAN = 16
MOE_LOC_ROWS = 768
MOE_TILE = 2048
MOE_ROW_BLOCK = 384
LOG2E = 1.4426950408889634
EXP_CLAMP = 80.0
ROUTER_GROUP_LANE0 = 0
ROUTER_EXPERT_LANE0 = N_GROUPS
VMEM_LIMIT = 56 * 1024 * 1024

_NT = (((1,), (1,)), ((), ()))
_TN = (((0,), (0,)), ((), ()))


def _dot(a, b):
    return jnp.dot(a, b, preferred_element_type=F32)


def _split_bf16(x):
    hi = x.astype(BF16)
    lo = (x - hi.astype(F32)).astype(BF16)
    return hi, lo


def _sigmoid(x):
    return 1.0 / (1.0 + jnp.exp(-x))


def _const_spec(shape):
    nd = len(shape)
    return pl.BlockSpec(shape, lambda *_: (0,) * nd, pipeline_mode=pl.Buffered(1))


def _in_proj_kernel(x_ref, nrm_ref, wm_ref, wz_ref, wg_ref, qg_ref, kg_ref, hsum_ref,
                    qa_ref, ka_ref, va_ref, qb_ref, kb_ref, vb_ref, rb_ref, z_ref, ga_ref, gb_ref):
    x = x_ref[...]
    ms = jnp.mean(x * x, axis=-1, keepdims=True)
    xn = (x * lax.rsqrt(ms + EPS) * nrm_ref[...]).astype(BF16)

    def proj(k):
        return _dot(xn, wm_ref[:, k * 512:(k + 1) * 512])

    def head_norm(a, gain_ref):
        ss = _dot((a * a).astype(BF16), hsum_ref[...])
        return a * lax.rsqrt(ss * (1.0 / NA_HEAD_DIM) + EPS) * gain_ref[...]

    qa_ref[...] = head_norm(proj(0), qg_ref).astype(BF16)
    ka_ref[...] = head_norm(proj(1), kg_ref).astype(BF16)
    va_ref[...] = proj(2).astype(BF16)
    qb_ref[...] = (proj(3) * (GLA_DK ** -0.5)).astype(BF16)
    kb_ref[...] = proj(4).astype(BF16)
    vb_ref[...] = proj(5).astype(BF16)
    r = proj(6)
    rb_ref[...] = (r * _sigmoid(r)).astype(BF16)
    z_ref[...] = _dot(xn, wz_ref[...])
    ga_ref[...] = _sigmoid(_dot(xn, wg_ref[:, :D_MODEL])).astype(BF16)
    gb_ref[...] = _sigmoid(_dot(xn, wg_ref[:, D_MODEL:])).astype(BF16)


def _in_proj(x, nrm, wm, wz, wg, qg, kg, hsum, tm):
    t = x.shape[0]
    row = lambda w: pl.BlockSpec((tm, w), lambda i: (i, 0))
    out_w = [512] * 7 + [LANES, D_MODEL, D_MODEL]
    out_dt = [BF16] * 7 + [F32, BF16, BF16]
    return pl.pallas_call(
        _in_proj_kernel,
        grid=(t // tm,),
        in_specs=[row(D_MODEL), _const_spec(nrm.shape), _const_spec(wm.shape), _const_spec(wz.shape),
                  _const_spec(wg.shape), _const_spec(qg.shape), _const_spec(kg.shape), _const_spec(hsum.shape)],
        out_specs=[row(w) for w in out_w],
        out_shape=[jax.ShapeDtypeStruct((t, w), dt) for w, dt in zip(out_w, out_dt)],
        compiler_params=pltpu.CompilerParams(dimension_semantics=("parallel",), vmem_limit_bytes=VMEM_LIMIT),
        name="in_proj",
    )(x, nrm, wm, wz, wg, qg, kg, hsum)


NA_BLOCK_ROWS = 8
NA_BLOCK_TOK = NA_BLOCK_ROWS * GRID_W
NA_WINDOW_TOK = NA_ROWS * GRID_W


def _na_kernel(q_ref, kp_ref, kc_ref, kn_ref, vp_ref, vc_ref, vn_ref, bias_ref, o_ref, kwin, vwin, s_scr, *, rows):
    i = pl.program_id(1)
    kwin[0:NA_BLOCK_TOK] = kp_ref[...]
    kwin[NA_BLOCK_TOK:2 * NA_BLOCK_TOK] = kc_ref[...]
    kwin[2 * NA_BLOCK_TOK:3 * NA_BLOCK_TOK] = kn_ref[...]
    vwin[0:NA_BLOCK_TOK] = vp_ref[...]
    vwin[NA_BLOCK_TOK:2 * NA_BLOCK_TOK] = vc_ref[...]
    vwin[2 * NA_BLOCK_TOK:3 * NA_BLOCK_TOK] = vn_ref[...]
    low_half = lax.broadcasted_iota(jnp.int32, (GRID_W, LANES), 1) < NA_HEAD_DIM
    pairs = range(NA_HEADS // 2)
    cols = lambda p: slice(p * LANES, (p + 1) * LANES)
    qrows = lambda j: slice(j * GRID_W, (j + 1) * GRID_W)

    def window(j):
        r = i * NA_BLOCK_ROWS + j
        rs = jnp.clip(r - NA_ROWS // 2, 0, rows - NA_ROWS)
        start = pl.multiple_of((rs - i * NA_BLOCK_ROWS + NA_BLOCK_ROWS) * GRID_W, GRID_W)
        return start, rs - r + NA_ROWS - 1

    def scores(j):
        start, bias_off = window(j)
        for p in pairs:
            qp = q_ref[qrows(j), cols(p)]
            zero = jnp.zeros_like(qp)
            q2 = jnp.concatenate([jnp.where(low_half, qp, zero), jnp.where(low_half, zero, qp)], axis=0)
            s = lax.dot_general(q2, kwin[pl.ds(start, NA_WINDOW_TOK), cols(p)], _NT, preferred_element_type=F32)
            s_scr[j % 2, p] = s + bias_ref[bias_off, p]

    def attend(j):
        start, _ = window(j)
        for p in pairs:
            s = s_scr[j % 2, p]
            e = jnp.exp2(s - jnp.max(s, axis=-1, keepdims=True))
            l = jnp.sum(e, axis=-1, keepdims=True)
            pv = _dot(e.astype(BF16), vwin[pl.ds(start, NA_WINDOW_TOK), cols(p)]) * (1.0 / l)
            o_ref[qrows(j), cols(p)] = jnp.where(low_half, pv[:GRID_W], pv[GRID_W:]).astype(BF16)

    scores(0)
    for j in range(NA_BLOCK_ROWS):
        if j + 1 < NA_BLOCK_ROWS:
            scores(j + 1)
        attend(j)


def _na(q, k, v, bias, batch, n):
    rows = n // GRID_W
    nblk = rows // NA_BLOCK_ROWS
    spec = lambda f: pl.BlockSpec((NA_BLOCK_TOK, NA_WIDTH), lambda b, i: (b * nblk + f(i), 0))
    cur = spec(lambda i: i)
    prev = spec(lambda i: jnp.maximum(i - 1, 0))
    nxt = spec(lambda i: jnp.minimum(i + 1, nblk - 1))
    return pl.pallas_call(
        functools.partial(_na_kernel, rows=rows),
        grid=(batch, nblk),
        in_specs=[cur, prev, cur, nxt, prev, cur, nxt, _const_spec(bias.shape)],
        out_specs=cur,
        out_shape=jax.ShapeDtypeStruct((batch * n, NA_WIDTH), BF16),
        scratch_shapes=[pltpu.VMEM((3 * NA_BLOCK_TOK, NA_WIDTH), BF16),
                        pltpu.VMEM((3 * NA_BLOCK_TOK, NA_WIDTH), BF16),
                        pltpu.VMEM((2, NA_HEADS // 2, 2 * GRID_W, NA_WINDOW_TOK), F32)],
        compiler_params=pltpu.CompilerParams(dimension_semantics=("parallel", "parallel"),
                                             vmem_limit_bytes=VMEM_LIMIT),
        name="na",
    )(q, k, k, k, v, v, v, bias)


def _na_bias_table(rpb):
    cols = jnp.arange(GRID_W)
    col_start = jnp.clip(cols - NA_COLS // 2, 0, GRID_W - NA_COLS)
    col_mask = (cols[None, :] >= col_start[:, None]) & (cols[None, :] < col_start[:, None] + NA_COLS)
    col_idx = jnp.clip(cols[None, :] - cols[:, None] + NA_COLS - 1, 0, RPB_COLS - 1)
    masked = jnp.where(col_mask[None, None], rpb.astype(F32)[:, :, col_idx] * LOG2E, NEG)
    slabs = [masked[:, o:o + NA_ROWS].transpose(0, 2, 1, 3).reshape(NA_HEADS // 2, 2 * GRID_W, NA_WINDOW_TOK)
             for o in range(NA_ROWS)]
    return jnp.stack(slabs)


def _gla_block_prepare(q_ref, k_ref, v_ref, z_ref, wh_ref, wl_ref, bias_ref, tri_ref, mask, mid, end,
                       oi_s, qi_s, kv_s, dec_s, nsub):
    zhi, zlo = _split_bf16(z_ref[...])
    logits = _dot(zhi, wh_ref[...]) + _dot(zhi, wl_ref[...]) + _dot(zlo, wh_ref[...]) + bias_ref[...]
    g = (jnp.minimum(logits, 0.0) - jnp.log(1.0 + jnp.exp(-jnp.abs(logits)))) * (1.0 / GLA_GATE_NORM)
    ghi, glo = _split_bf16(g)
    b_all = _dot(tri_ref[...], ghi) + _dot(tri_ref[...], glo)
    pairs = [(s, h) for s in range(nsub) for h in range(GLA_HEADS)]
    rows = lambda s: slice(s * GLA_CHUNK, (s + 1) * GLA_CHUNK)
    cols = lambda h: slice(h * GLA_DK, (h + 1) * GLA_DK)
    qe16, ke16, k_out = [], [], []
    for s in range(nsub):
        b = b_all[rows(s)]
        b_mid = b[mid:mid + 1, :]
        b_end = b[end:end + 1, :]
        qe = q_ref[rows(s), :].astype(F32) * jnp.exp(jnp.minimum(b - b_mid, EXP_CLAMP))
        ke = k_ref[rows(s), :].astype(F32) * jnp.exp(jnp.minimum(b_mid - b, EXP_CLAMP))
        qe16.append(qe.astype(BF16))
        ke16.append(ke.astype(BF16))
        qi_s[rows(s), :] = (qe * jnp.exp(b_mid)).astype(BF16)
        k_out.append((ke * jnp.exp(b_end - b_mid)).astype(BF16))
        dec_s[s] = jnp.broadcast_to(jnp.exp(b_end), (8, GLA_WIDTH))
    a = [lax.dot_general(qe16[s][:, cols(h)], ke16[s][:, cols(h)], _NT, preferred_element_type=F32)
         for s, h in pairs]
    for s, h in pairs:
        kv_s[s, h] = lax.dot_general(v_ref[rows(s), cols(h)], k_out[s][:, cols(h)], _TN,
                                     preferred_element_type=F32)
    for (s, h), a_sh in zip(pairs, a):
        oi_s[rows(s), cols(h)] = _dot(jnp.where(mask, a_sh, 0.0).astype(BF16), v_ref[rows(s), cols(h)])


def _gla_block_scan(steps, oi_s, qi_s, kv_s, dec_s):
    for s, o_ref, st_ref, d in steps:
        rows = slice(s * GLA_CHUNK, (s + 1) * GLA_CHUNK)
        dec = dec_s[d][s]
        states = [st_ref[h] for h in range(GLA_HEADS)]
        inter = [lax.dot_general(qi_s[d][rows, h * GLA_DK:(h + 1) * GLA_DK], states[h].astype(BF16), _NT,
                                 preferred_element_type=F32) for h in range(GLA_HEADS)]
        for h in range(GLA_HEADS):
            cols = slice(h * GLA_DK, (h + 1) * GLA_DK)
            st_ref[h] = states[h] * dec[0:1, cols] + kv_s[d][s, h]
            o_ref[rows, cols] = (oi_s[d][rows, cols] + inter[h]).astype(BF16)


def _gla_kernel(qf_ref, kf_ref, vf_ref, zf_ref, qr_ref, kr_ref, vr_ref, zr_ref,
                wfh_ref, wfl_ref, wbh_ref, wbl_ref, bf_ref, bb_ref, trif_ref, trib_ref, of_ref, ob_ref,
                stf_ref, stb_ref, oif_s, oib_s, qif_s, qib_s, kvf_s, kvb_s, decf_s, decb_s, *, nsub):
    @pl.when(pl.program_id(1) == 0)
    def _():
        stf_ref[...] = jnp.zeros_like(stf_ref)
        stb_ref[...] = jnp.zeros_like(stb_ref)

    c = GLA_CHUNK
    ri = lax.broadcasted_iota(jnp.int32, (c, c), 0)
    ci = lax.broadcasted_iota(jnp.int32, (c, c), 1)
    _gla_block_prepare(qf_ref, kf_ref, vf_ref, zf_ref, wfh_ref, wfl_ref, bf_ref, trif_ref, ci <= ri,
                       c // 2 - 1, c - 1, oif_s, qif_s, kvf_s, decf_s, nsub)
    _gla_block_prepare(qr_ref, kr_ref, vr_ref, zr_ref, wbh_ref, wbl_ref, bb_ref, trib_ref, ci >= ri,
                       c // 2, 0, oib_s, qib_s, kvb_s, decb_s, nsub)
    steps = []
    for s in range(nsub):
        steps += [(s, of_ref, stf_ref, 0), (nsub - 1 - s, ob_ref, stb_ref, 1)]
    _gla_block_scan(steps, (oif_s, oib_s), (qif_s, qib_s), (kvf_s, kvb_s), (decf_s, decb_s))


def _gla(q, k, v, z, wfh, wfl, wbh, wbl, bf, bb, trif, trib, batch, n, cb):
    nb = n // cb
    nsub = cb // GLA_CHUNK
    fwd = lambda w: pl.BlockSpec((cb, w), lambda b, c: (b * nb + c, 0))
    bwd = lambda w: pl.BlockSpec((cb, w), lambda b, c: (b * nb + nb - 1 - c, 0))
    consts = [wfh, wfl, wbh, wbl, bf, bb, trif, trib]
    out = jax.ShapeDtypeStruct((batch * n, GLA_WIDTH), BF16)
    state = pltpu.VMEM((GLA_HEADS, GLA_DV, GLA_DK), F32)
    return pl.pallas_call(
        functools.partial(_gla_kernel, nsub=nsub),
        grid=(batch, nb),
        in_specs=[fwd(GLA_WIDTH), fwd(GLA_WIDTH), fwd(GLA_WIDTH), fwd(LANES),
                  bwd(GLA_WIDTH), bwd(GLA_WIDTH), bwd(GLA_WIDTH), bwd(LANES)] + [_const_spec(c.shape) for c in consts],
        out_specs=[fwd(GLA_WIDTH), bwd(GLA_WIDTH)],
        out_shape=[out, out],
        scratch_shapes=[state, state,
                        pltpu.VMEM((cb, GLA_WIDTH), F32), pltpu.VMEM((cb, GLA_WIDTH), F32),
                        pltpu.VMEM((cb, GLA_WIDTH), BF16), pltpu.VMEM((cb, GLA_WIDTH), BF16),
                        pltpu.VMEM((nsub, GLA_HEADS, GLA_DV, GLA_DK), F32),
                        pltpu.VMEM((nsub, GLA_HEADS, GLA_DV, GLA_DK), F32),
                        pltpu.VMEM((nsub, 8, GLA_WIDTH), F32), pltpu.VMEM((nsub, 8, GLA_WIDTH), F32)],
        compiler_params=pltpu.CompilerParams(dimension_semantics=("parallel", "arbitrary"),
                                             vmem_limit_bytes=VMEM_LIMIT),
        name="gla",
    )(q, k, v, z, q, k, v, z, *consts)


def _gla_decay_weights(w_decay, b_decay, cb):
    def padded(w, row0):
        full = jnp.zeros((LANES, GLA_WIDTH), F32).at[row0:row0 + GLA_RANK].set(w.astype(F32))
        return _split_bf16(full)
    wfh, wfl = padded(w_decay[0], 0)
    wbh, wbl = padded(w_decay[1], GLA_RANK)
    bias = b_decay.astype(F32).reshape(2, 1, GLA_WIDTH)
    tok = jnp.arange(cb)
    same = (tok[:, None] // GLA_CHUNK) == (tok[None, :] // GLA_CHUNK)
    trif = (same & (tok[None, :] <= tok[:, None])).astype(BF16)
    trib = (same & (tok[None, :] >= tok[:, None])).astype(BF16)
    return wfh, wfl, wbh, wbl, bias[0], bias[1], trif, trib


def _route(logits):
    lane = lax.broadcasted_iota(jnp.int32, logits.shape, 1)
    lane_f = lane.astype(F32)
    ninf = -jnp.inf

    def first_argmax(vals, vmax):
        return jnp.min(jnp.where(vals == vmax, lane_f, float(LANES)), axis=-1, keepdims=True)

    lg = jnp.where(lane < N_GROUPS, logits, ninf)
    g_max = jnp.max(lg, axis=-1, keepdims=True)
    p_sel = 1.0 / jnp.sum(jnp.exp(lg - g_max), axis=-1, keepdims=True)
    grp = first_argmax(lg, g_max)
    e_lo = ROUTER_EXPERT_LANE0 + EXPERTS_PER_GROUP * grp
    le = jnp.where((lane_f >= e_lo) & (lane_f < e_lo + EXPERTS_PER_GROUP), logits, ninf)
    l1 = jnp.max(le, axis=-1, keepdims=True)
    i1 = first_argmax(le, l1)
    le2 = jnp.where(lane_f == i1, ninf, le)
    l2 = jnp.max(le2, axis=-1, keepdims=True)
    i2 = first_argmax(le2, l2)
    t = jnp.exp(l2 - l1)
    w1 = p_sel / (1.0 + t)
    w2 = w1 * t
    return lane_f == i1, lane_f == i2, w1, w2


def _local_sort(sel1, sel2, w1, w2):
    n = sel1.shape[0]
    ri = lax.broadcasted_iota(jnp.int32, (n, n), 0)
    ci = lax.broadcasted_iota(jnp.int32, (n, n), 1)
    before = jnp.where(ci < ri, 1.0, 0.0).astype(BF16)
    li = lax.broadcasted_iota(jnp.int32, (LANES, LANES), 0)
    lj = lax.broadcasted_iota(jnp.int32, (LANES, LANES), 1)
    lower_lanes = jnp.where(li < lj, 1.0, 0.0).astype(BF16)
    oh1 = jnp.where(sel1, 1.0, 0.0)
    oh2 = jnp.where(sel2, 1.0, 0.0)
    cnt1 = jnp.sum(oh1, axis=0, keepdims=True)
    cnt = cnt1 + jnp.sum(oh2, axis=0, keepdims=True)
    gran = jnp.floor((cnt + (MOE_GRAN - 1)) * (1.0 / MOE_GRAN))
    offg = _dot(jnp.broadcast_to(gran, (8, LANES)).astype(BF16), lower_lanes)[0:1]
    pick = lambda sel, vals: jnp.sum(jnp.where(sel, vals, 0.0), axis=-1, keepdims=True)
    pos1 = pick(sel1, _dot(before, oh1.astype(BF16)) + MOE_GRAN * offg)
    pos2 = pick(sel2, _dot(before, oh2.astype(BF16)) + cnt1 + MOE_GRAN * offg)
    lane = lax.broadcasted_iota(jnp.int32, (n, LANES), 1)
    route = jnp.where(lane == 0, pos1, jnp.where(lane == 1, pos2, jnp.where(lane == 2, w1,
                                                                             jnp.where(lane == 3, w2, 0.0))))
    sub = lax.broadcasted_iota(jnp.int32, (8, LANES), 0)
    meta = jnp.where(sub == 0, offg, jnp.where(sub == 1, gran, 0.0)).astype(jnp.int32)
    return route, meta


def _post_kernel(x_ref, ya_ref, of_ref, ob_ref, rb_ref, ga_ref, gb_ref, woa_ref, wob_ref, wout_ref,
                 gn_ref, nf_ref, wrh_ref, wrl_ref, br_ref, x1_ref, xn_ref, route_ref, meta_ref):
    o = of_ref[...].astype(F32) + ob_ref[...].astype(F32)
    parts = []
    for h in range(GLA_HEADS):
        oh = o[:, h * GLA_DV:(h + 1) * GLA_DV]
        ms = jnp.mean(oh * oh, axis=-1, keepdims=True)
        parts.append(oh * lax.rsqrt(ms + EPS) * gn_ref[...])
    on = jnp.concatenate(parts, axis=-1) * rb_ref[...].astype(F32)
    yb = _dot(on.astype(BF16), wob_ref[...])
    ya = _dot(ya_ref[...], woa_ref[...])
    y = ga_ref[...].astype(F32) * ya + gb_ref[...].astype(F32) * yb
    x1 = x_ref[...] + _dot(y.astype(BF16), wout_ref[...])
    x1_ref[...] = x1
    ms = jnp.mean(x1 * x1, axis=-1, keepdims=True)
    t = x1 * lax.rsqrt(ms + EPS) * nf_ref[...]
    thi, tlo = _split_bf16(t)
    xn_ref[...] = thi
    logits = _dot(thi, wrh_ref[...]) + _dot(thi, wrl_ref[...]) + _dot(tlo, wrh_ref[...]) + br_ref[...]
    sel1, sel2, w1, w2 = _route(logits)
    for u in range(x1.shape[0] // MOE_SUB):
        rows = slice(u * MOE_SUB, (u + 1) * MOE_SUB)
        route_ref[rows, :], meta_ref[u] = _local_sort(sel1[rows], sel2[rows], w1[rows], w2[rows])


def _post(x, ya, of, ob, rb, ga, gb, woa, wob, wout, gn, nf, wrh, wrl, br, tm):
    t = x.shape[0]
    row = lambda w: pl.BlockSpec((tm, w), lambda i: (i, 0))
    consts = [woa, wob, wout, gn, nf, wrh, wrl, br]
    return pl.pallas_call(
        _post_kernel,
        grid=(t // tm,),
        in_specs=[row(D_MODEL), row(NA_WIDTH), row(GLA_WIDTH), row(GLA_WIDTH), row(GLA_WIDTH),
                  row(D_MODEL), row(D_MODEL)] + [_const_spec(c.shape) for c in consts],
        out_specs=[row(D_MODEL), row(D_MODEL), row(LANES),
                   pl.BlockSpec((tm // MOE_SUB, 8, LANES), lambda i: (i, 0, 0))],
        out_shape=[jax.ShapeDtypeStruct((t, D_MODEL), F32), jax.ShapeDtypeStruct((t, D_MODEL), BF16),
                   jax.ShapeDtypeStruct((t, LANES), F32),
                   jax.ShapeDtypeStruct((t // MOE_SUB, 8, LANES), jnp.int32)],
        compiler_params=pltpu.CompilerParams(dimension_semantics=("parallel",), vmem_limit_bytes=VMEM_LIMIT),
        name="post",
    )(x, ya, of, ob, rb, ga, gb, *consts)


def _moe_kernel(meta_ref, xn_ref, route_ref, x1_ref, wg_ref, wu_ref, wd_ref, o_ref, loc, stage, *, nsub):
    i = pl.program_id(0)
    e = pl.program_id(1)
    sub_rows = lambda u: slice(u * MOE_SUB, (u + 1) * MOE_SUB)
    gran_rows = lambda g: pl.ds(pl.multiple_of(g * MOE_GRAN, MOE_GRAN), MOE_GRAN)
    meta = lambda u, k: meta_ref[((i * nsub + u) * 2 + k) * N_EXPERTS + e]

    @pl.when(e == 0)
    def _dispatch():
        stage[...] = jnp.zeros_like(stage)
        sel = lax.broadcasted_iota(jnp.int32, (8, LANES), 0) == lax.broadcasted_iota(jnp.int32, (8, LANES), 1)
        sel = jnp.where(sel, 1.0, 0.0).astype(BF16)
        row_id = lax.broadcasted_iota(jnp.int32, (MOE_LOC_ROWS, MOE_SUB), 0).astype(F32)
        for u in range(nsub):
            rhi, rlo = _split_bf16(route_ref[sub_rows(u), :])
            pos_t = (lax.dot_general(sel, rhi, _NT, preferred_element_type=F32)
                     + lax.dot_general(sel, rlo, _NT, preferred_element_type=F32))
            onehot = jnp.where(row_id == pos_t[0:1], 1.0, 0.0) + jnp.where(row_id == pos_t[1:2], 1.0, 0.0)
            loc[u] = _dot(onehot.astype(BF16), xn_ref[sub_rows(u), :]).astype(BF16)

    def move_granules(to_stage):
        dst = 0
        for u in range(nsub):
            off, n = meta(u, 0), meta(u, 1)

            def body(g, carry, u=u, off=off, dst=dst):
                if to_stage:
                    stage[gran_rows(dst + g), :] = loc[u, gran_rows(off + g), :]
                else:
                    loc[u, gran_rows(off + g), :] = stage[gran_rows(dst + g), :]
                return carry

            lax.fori_loop(0, n, body, 0)
            dst = dst + n
        return dst

    total = move_granules(True)

    def block(b, carry):
        rows = pl.ds(pl.multiple_of(b * MOE_ROW_BLOCK, MOE_ROW_BLOCK), MOE_ROW_BLOCK)
        xs = stage[rows, :]
        hg = _dot(xs, wg_ref[0])
        h = (hg * _sigmoid(hg) * _dot(xs, wu_ref[0])).astype(BF16)
        stage[rows, :] = _dot(h, wd_ref[0]).astype(BF16)
        return carry

    lax.fori_loop(0, (total * MOE_GRAN + MOE_ROW_BLOCK - 1) // MOE_ROW_BLOCK, block, 0)
    move_granules(False)

    @pl.when(e == N_EXPERTS - 1)
    def _combine():
        col_id = lax.broadcasted_iota(jnp.int32, (MOE_SUB, MOE_LOC_ROWS), 1).astype(F32)
        for u in range(nsub):
            r = route_ref[sub_rows(u), :]
            pick = jnp.concatenate([jnp.where(col_id == r[:, 0:1], 1.0, 0.0).astype(BF16),
                                    jnp.where(col_id == r[:, 1:2], 1.0, 0.0).astype(BF16)], axis=0)
            y = _dot(pick, loc[u])
            o_ref[sub_rows(u), :] = x1_ref[sub_rows(u), :] + r[:, 2:3] * y[:MOE_SUB] + r[:, 3:4] * y[MOE_SUB:]


def _moe(meta, xn, route, x1, wg, wu, wd, tb):
    t = xn.shape[0]
    nsub = tb // MOE_SUB
    row = lambda w, **kw: pl.BlockSpec((tb, w), lambda i, e, m: (i, 0), **kw)
    wspec = lambda a, b: pl.BlockSpec((1, a, b), lambda i, e, m: (e, 0, 0))
    return pl.pallas_call(
        functools.partial(_moe_kernel, nsub=nsub),
        grid_spec=pltpu.PrefetchScalarGridSpec(
            num_scalar_prefetch=1,
            grid=(t // tb, N_EXPERTS),
            in_specs=[row(D_MODEL), row(LANES), row(D_MODEL, pipeline_mode=pl.Buffered(1)),
                      wspec(D_MODEL, D_EXPERT), wspec(D_MODEL, D_EXPERT), wspec(D_EXPERT, D_MODEL)],
            out_specs=row(D_MODEL, pipeline_mode=pl.Buffered(1)),
            scratch_shapes=[pltpu.VMEM((nsub, MOE_LOC_ROWS, D_MODEL), BF16),
                            pltpu.VMEM((pl.cdiv(tb, MOE_ROW_BLOCK) * MOE_ROW_BLOCK, D_MODEL), BF16)]),
        out_shape=jax.ShapeDtypeStruct((t, D_MODEL), F32),
        compiler_params=pltpu.CompilerParams(dimension_semantics=("parallel", "arbitrary"),
                                             vmem_limit_bytes=VMEM_LIMIT),
        name="moe",
    )(meta, xn, route, x1, wg, wu, wd)


def _prep_layer(l, norm_mix, w_in, q_norm_a, k_norm_a, rpb, w_decay, b_decay, gla_norm, w_o_a, w_o_b, w_out,
                norm_ffn, w_router_g, b_router_g, w_router_e, b_router_e, w_gate, w_up, w_down):
    w = w_in[l]
    wz = jnp.zeros((D_MODEL, LANES), F32).at[:, :2 * GLA_RANK].set(w[:, MAIN_WIDTH:MAIN_WIDTH + 2 * GLA_RANK])
    head = jnp.arange(NA_WIDTH) // NA_HEAD_DIM
    w_router = jnp.zeros((D_MODEL, LANES), F32)
    w_router = w_router.at[:, ROUTER_GROUP_LANE0:ROUTER_GROUP_LANE0 + N_GROUPS].set(w_router_g[l])
    w_router = w_router.at[:, ROUTER_EXPERT_LANE0:ROUTER_EXPERT_LANE0 + N_EXPERTS].set(w_router_e[l])
    b_router = jnp.zeros((1, LANES), F32)
    b_router = b_router.at[0, ROUTER_GROUP_LANE0:ROUTER_GROUP_LANE0 + N_GROUPS].set(b_router_g[l])
    b_router = b_router.at[0, ROUTER_EXPERT_LANE0:ROUTER_EXPERT_LANE0 + N_EXPERTS].set(b_router_e[l])
    wrh, wrl = _split_bf16(w_router)
    return dict(
        nrm=norm_mix[l].reshape(1, D_MODEL),
        wm=w[:, :MAIN_WIDTH].astype(BF16),
        wz=wz.astype(BF16),
        wg=w[:, MAIN_WIDTH + 2 * GLA_RANK:].astype(BF16),
        qg=(jnp.tile(q_norm_a[l], NA_HEADS) * (NA_HEAD_DIM ** -0.5 * LOG2E)).reshape(1, NA_WIDTH),
        kg=jnp.tile(k_norm_a[l], NA_HEADS).reshape(1, NA_WIDTH),
        hsum=(head[:, None] == head[None, :]).astype(BF16),
        bias=_na_bias_table(rpb[l]),
        decay=_gla_decay_weights(w_decay[l], b_decay[l], GLA_BLOCK_TOK),
        woa=w_o_a[l].astype(BF16), wob=w_o_b[l].astype(BF16), wout=w_out[l].astype(BF16),
        gn=gla_norm[l].reshape(1, GLA_DV), nf=norm_ffn[l].reshape(1, D_MODEL),
        wrh=wrh, wrl=wrl, br=b_router,
        wgate=w_gate[l].astype(BF16), wup=w_up[l].astype(BF16), wdown=w_down[l].astype(BF16),
    )


def _tile(t, want):
    while t % want:
        want //= 2
    return want


def _trunk(x, layers):
    batch, n, _ = x.shape
    t = batch * n
    x = x.reshape(t, D_MODEL)
    for p in layers:
        qa, ka, va, qb, kb, vb, rb, z, ga, gb = _in_proj(x, p["nrm"], p["wm"], p["wz"], p["wg"], p["qg"], p["kg"],
                                                         p["hsum"], _tile(t, 512))
        ya = _na(qa, ka, va, p["bias"], batch, n)
        of, ob = _gla(qb, kb, vb, z, *p["decay"], batch, n, GLA_BLOCK_TOK)
        x1, xn, route, meta = _post(x, ya, of, ob, rb, ga, gb, p["woa"], p["wob"], p["wout"], p["gn"], p["nf"],
                                    p["wrh"], p["wrl"], p["br"], _tile(t, 512))
        meta = meta[:, :2, ROUTER_EXPERT_LANE0:ROUTER_EXPERT_LANE0 + N_EXPERTS].reshape(-1)
        x = _moe(meta, xn, route, x1, p["wgate"], p["wup"], p["wdown"], _tile(t, MOE_TILE))
    return x.reshape(batch, n, D_MODEL)


def kernel(x_prompt, x_sample, norm_mix, w_in, q_norm_a, k_norm_a, rpb, w_decay, b_decay, gla_norm, w_o_a, w_o_b,
           w_out, norm_ffn, w_router_g, b_router_g, w_router_e, b_router_e, w_gate, w_up, w_down):
    depth = w_in.shape[0]
    layers = [_prep_layer(l, norm_mix, w_in, q_norm_a, k_norm_a, rpb, w_decay, b_decay, gla_norm, w_o_a, w_o_b,
                          w_out, norm_ffn, w_router_g, b_router_g, w_router_e, b_router_e, w_gate, w_up, w_down)
              for l in range(depth)]
    return (_trunk(x_prompt, layers), _trunk(x_sample, layers))
```

```python
import functools

import jax
import jax.numpy as jnp
import numpy as np
from jax import lax
from jax.experimental import pallas as pl
from jax.experimental.pallas import tpu as pltpu

F32 = jnp.float32
BF16 = jnp.bfloat16

D_MODEL = 1024
GRID_W = 64
NA_HEADS = 8
NA_HEAD_DIM = 64
NA_WIDTH = NA_HEADS * NA_HEAD_DIM
NA_ROWS = 8
NA_COLS = 16
RPB_ROWS = 2 * NA_ROWS - 1
RPB_COLS = 2 * NA_COLS - 1
GLA_HEADS = 4
GLA_DK = 128
GLA_DV = 128
GLA_WIDTH = GLA_HEADS * GLA_DV
GLA_RANK = 16
GLA_GATE_NORM = 16.0
GLA_CHUNK = 64
GLA_BLOCK_TOK = 512
N_GROUPS = 4
EXPERTS_PER_GROUP = 4
N_EXPERTS = N_GROUPS * EXPERTS_PER_GROUP
D_EXPERT = 512
EPS = 1e-6
NEG = -1e30

LANES = 128
IN_Z_COL0 = 7 * 512
IN_GATE_COL0 = IN_Z_COL0 + LANES
MOE_SUB = 256
MOE_GRAN = 16
MOE_LOC_ROWS = 768
MOE_TILE = 2048
MOE_ROW_BLOCK = 384
LOG2E = 1.4426950408889634
EXP_CLAMP = 80.0
ROUTER_GROUP_LANE0 = 0
ROUTER_EXPERT_LANE0 = N_GROUPS
VMEM_LIMIT = 56 * 1024 * 1024

_NT = (((1,), (1,)), ((), ()))
_TN = (((0,), (0,)), ((), ()))


def _dot(a, b):
    return jnp.dot(a, b, preferred_element_type=F32)


def _split_bf16(x):
    hi = x.astype(BF16)
    lo = (x - hi.astype(F32)).astype(BF16)
    return hi, lo


def _sigmoid(x):
    return 1.0 / (1.0 + jnp.exp(-x))


def _const_spec(stacked, l):
    tail = stacked.shape[1:]
    return pl.BlockSpec((None,) + tail, lambda *_: (l,) + (0,) * len(tail), pipeline_mode=pl.Buffered(1))


def _in_proj_kernel(x_ref, nrm_ref, w_ref, qg_ref, kg_ref, hsum_ref,
                    qa_ref, ka_ref, va_ref, qb_ref, kb_ref, vb_ref, rb_ref, z_ref, ga_ref, gb_ref):
    x = x_ref[...]
    ms = jnp.mean(x * x, axis=-1, keepdims=True)
    xn = (x * lax.rsqrt(ms + EPS) * nrm_ref[...]).astype(BF16)

    def proj(lo, width=512):
        return _dot(xn, w_ref[:, lo:lo + width])

    def head_norm(a, gain_ref):
        ss = _dot((a * a).astype(BF16), hsum_ref[...])
        return a * lax.rsqrt(ss * (1.0 / NA_HEAD_DIM) + EPS) * gain_ref[...]

    qa = proj(0)
    ka = proj(512)
    va_ref[...] = proj(1024).astype(BF16)
    qa_ref[...] = head_norm(qa, qg_ref).astype(BF16)
    qb_ref[...] = (proj(1536) * (GLA_DK ** -0.5)).astype(BF16)
    ka_ref[...] = head_norm(ka, kg_ref).astype(BF16)
    kb_ref[...] = proj(2048).astype(BF16)
    vb_ref[...] = proj(2560).astype(BF16)
    r = proj(3072)
    rb_ref[...] = (r * _sigmoid(r)).astype(BF16)
    z_ref[...] = proj(IN_Z_COL0, LANES)
    ga_ref[...] = _sigmoid(proj(IN_GATE_COL0, D_MODEL)).astype(BF16)
    gb_ref[...] = _sigmoid(proj(IN_GATE_COL0 + D_MODEL, D_MODEL)).astype(BF16)


def _in_proj(x, l, nrm, w, qg, kg, hsum, tm):
    t = x.shape[0]
    row = lambda w: pl.BlockSpec((tm, w), lambda i: (i, 0))
    out_w = [512] * 7 + [LANES, D_MODEL, D_MODEL]
    out_dt = [BF16] * 7 + [F32, BF16, BF16]
    consts = [nrm, w, qg, kg, hsum]
    return pl.pallas_call(
        _in_proj_kernel,
        grid=(t // tm,),
        in_specs=[row(D_MODEL)] + [_const_spec(c, l) for c in consts],
        out_specs=[row(w) for w in out_w],
        out_shape=[jax.ShapeDtypeStruct((t, w), dt) for w, dt in zip(out_w, out_dt)],
        compiler_params=pltpu.CompilerParams(dimension_semantics=("parallel",), vmem_limit_bytes=VMEM_LIMIT),
        name="in_proj",
    )(x, *consts)


NA_BLOCK_ROWS = 8
NA_BLOCK_TOK = NA_BLOCK_ROWS * GRID_W
NA_WINDOW_TOK = NA_ROWS * GRID_W


def _na_kernel(q_ref, kp_ref, kc_ref, kn_ref, vp_ref, vc_ref, vn_ref, bias_ref, o_ref, kwin, vwin, s_scr, *, rows):
    i = pl.program_id(1)
    kwin[0:NA_BLOCK_TOK] = kp_ref[...]
    kwin[NA_BLOCK_TOK:2 * NA_BLOCK_TOK] = kc_ref[...]
    kwin[2 * NA_BLOCK_TOK:3 * NA_BLOCK_TOK] = kn_ref[...]
    vwin[0:NA_BLOCK_TOK] = vp_ref[...]
    vwin[NA_BLOCK_TOK:2 * NA_BLOCK_TOK] = vc_ref[...]
    vwin[2 * NA_BLOCK_TOK:3 * NA_BLOCK_TOK] = vn_ref[...]
    low_half = lax.broadcasted_iota(jnp.int32, (GRID_W, LANES), 1) < NA_HEAD_DIM
    pairs = range(NA_HEADS // 2)
    cols = lambda p: slice(p * LANES, (p + 1) * LANES)
    qrows = lambda j: slice(j * GRID_W, (j + 1) * GRID_W)

    def window(j):
        r = i * NA_BLOCK_ROWS + j
        rs = jnp.clip(r - NA_ROWS // 2, 0, rows - NA_ROWS)
        start = pl.multiple_of((rs - i * NA_BLOCK_ROWS + NA_BLOCK_ROWS) * GRID_W, GRID_W)
        return start, rs - r + NA_ROWS - 1

    def scores(j):
        start, bias_off = window(j)
        for p in pairs:
            qp = q_ref[qrows(j), cols(p)]
            zero = jnp.zeros_like(qp)
            q2 = jnp.concatenate([jnp.where(low_half, qp, zero), jnp.where(low_half, zero, qp)], axis=0)
            s = lax.dot_general(q2, kwin[pl.ds(start, NA_WINDOW_TOK), cols(p)], _NT, preferred_element_type=F32)
            s_scr[j % 2, p] = s + bias_ref[bias_off, p]

    def attend(j):
        start, _ = window(j)
        for p in pairs:
            s = s_scr[j % 2, p]
            e = jnp.exp2(s - jnp.max(s, axis=-1, keepdims=True))
            l = jnp.sum(e, axis=-1, keepdims=True)
            pv = _dot(e.astype(BF16), vwin[pl.ds(start, NA_WINDOW_TOK), cols(p)]) * (1.0 / l)
            o_ref[qrows(j), cols(p)] = jnp.where(low_half, pv[:GRID_W], pv[GRID_W:]).astype(BF16)

    scores(0)
    for j in range(NA_BLOCK_ROWS):
        if j + 1 < NA_BLOCK_ROWS:
            scores(j + 1)
        attend(j)


def _na(q, k, v, l, bias, batch, n):
    rows = n // GRID_W
    nblk = rows // NA_BLOCK_ROWS
    spec = lambda f: pl.BlockSpec((NA_BLOCK_TOK, NA_WIDTH), lambda b, i: (b * nblk + f(i), 0))
    cur = spec(lambda i: i)
    prev = spec(lambda i: jnp.maximum(i - 1, 0))
    nxt = spec(lambda i: jnp.minimum(i + 1, nblk - 1))
    return pl.pallas_call(
        functools.partial(_na_kernel, rows=rows),
        grid=(batch, nblk),
        in_specs=[cur, prev, cur, nxt, prev, cur, nxt, _const_spec(bias, l)],
        out_specs=cur,
        out_shape=jax.ShapeDtypeStruct((batch * n, NA_WIDTH), BF16),
        scratch_shapes=[pltpu.VMEM((3 * NA_BLOCK_TOK, NA_WIDTH), BF16),
                        pltpu.VMEM((3 * NA_BLOCK_TOK, NA_WIDTH), BF16),
                        pltpu.VMEM((2, NA_HEADS // 2, 2 * GRID_W, NA_WINDOW_TOK), F32)],
        compiler_params=pltpu.CompilerParams(dimension_semantics=("parallel", "parallel"),
                                             vmem_limit_bytes=VMEM_LIMIT),
        name="na",
    )(q, k, k, k, v, v, v, bias)


def _na_bias_table(rpb):
    cols = np.arange(GRID_W)
    col_start = np.clip(cols - NA_COLS // 2, 0, GRID_W - NA_COLS)
    col_mask = (cols[None, :] >= col_start[:, None]) & (cols[None, :] < col_start[:, None] + NA_COLS)
    col_idx = np.clip(cols[None, :] - cols[:, None] + NA_COLS - 1, 0, RPB_COLS - 1)
    onehot = (col_idx[None] == np.arange(RPB_COLS)[:, None, None]).astype(np.float32)
    bias = jnp.einsum("lhrk,kqc->lhqrc", rpb.astype(F32) * LOG2E, onehot, precision=lax.Precision.HIGHEST)
    masked = jnp.where(col_mask[:, None, :], bias, NEG)
    depth = rpb.shape[0]
    slabs = [masked[:, :, :, o:o + NA_ROWS].reshape(depth, NA_HEADS // 2, 2 * GRID_W, NA_WINDOW_TOK)
             for o in range(NA_ROWS)]
    return jnp.stack(slabs, axis=1)


def _gla_chunk_rows(s):
    return slice(s * GLA_CHUNK, (s + 1) * GLA_CHUNK)


def _gla_head_cols(h):
    return slice(h * GLA_DK, (h + 1) * GLA_DK)


def _gla_log_decay(d):
    logits = _dot(d["z"][...].astype(BF16), d["w"][...]) + d["bias"][...]
    return (jnp.minimum(logits, 0.0) - jnp.log(1.0 + jnp.exp(-jnp.abs(logits)))) * (1.0 / GLA_GATE_NORM)


def _gla_cumulate(d, g, nsub):
    tri = jnp.where(d["mask"], 1.0, 0.0).astype(BF16)
    ghi, glo = _split_bf16(g)
    return [_dot(tri, ghi[_gla_chunk_rows(s)]) + _dot(tri, glo[_gla_chunk_rows(s)]) for s in range(nsub)]


def _gla_rescale(d, b_chunks):
    mid, end = d["mid"], d["end"]
    scaled = []
    for s, b in enumerate(b_chunks):
        rows = _gla_chunk_rows(s)
        b_mid = b[mid:mid + 1, :]
        b_end = b[end:end + 1, :]
        qe = d["q"][rows, :] * jnp.exp(jnp.minimum(b - b_mid, EXP_CLAMP)).astype(BF16)
        ke = d["k"][rows, :] * jnp.exp(jnp.minimum(b_mid - b, EXP_CLAMP)).astype(BF16)
        d["qi"][rows, :] = qe * jnp.exp(b_mid).astype(BF16)
        d["dec"][s] = jnp.broadcast_to(jnp.exp(b_end), (8, GLA_WIDTH))
        k_out = ke * jnp.exp(b_end - b_mid).astype(BF16)
        scaled.append((qe, ke, k_out))
    return scaled


def _gla_products(d, scaled):
    pairs = [(s, h) for s in range(len(scaled)) for h in range(GLA_HEADS)]
    v = lambda s, h: d["v"][_gla_chunk_rows(s), _gla_head_cols(h)]
    a = [lax.dot_general(scaled[s][0][:, _gla_head_cols(h)], scaled[s][1][:, _gla_head_cols(h)], _NT,
                         preferred_element_type=F32) for s, h in pairs]
    for s, h in pairs:
        d["kv"][s, h] = lax.dot_general(v(s, h), scaled[s][2][:, _gla_head_cols(h)], _TN,
                                        preferred_element_type=F32)
    for (s, h), a_sh in zip(pairs, a):
        d["oi"][_gla_chunk_rows(s), _gla_head_cols(h)] = _dot(jnp.where(d["mask"], a_sh, 0.0).astype(BF16), v(s, h))


def _gla_scan_step(d, s):
    rows = _gla_chunk_rows(s)
    dec = d["dec"][s]
    states = [d["st"][h] for h in range(GLA_HEADS)]
    inter = [lax.dot_general(d["qi"][rows, _gla_head_cols(h)], states[h].astype(BF16), _NT,
                             preferred_element_type=F32) for h in range(GLA_HEADS)]
    for h in range(GLA_HEADS):
        cols = _gla_head_cols(h)
        d["st"][h] = states[h] * dec[0:1, cols] + d["kv"][s, h]
        d["o"][rows, cols] = (d["oi"][rows, cols] + inter[h]).astype(BF16)


def _gla_kernel(qf_ref, kf_ref, vf_ref, zf_ref, qr_ref, kr_ref, vr_ref, zr_ref, wf_ref, wb_ref, bf_ref, bb_ref,
                of_ref, ob_ref, stf_ref, stb_ref, oif_s, oib_s, qif_s, qib_s, kvf_s, kvb_s, decf_s, decb_s, *, nsub):
    @pl.when(pl.program_id(1) == 0)
    def _():
        stf_ref[...] = jnp.zeros_like(stf_ref)
        stb_ref[...] = jnp.zeros_like(stb_ref)

    c = GLA_CHUNK
    ri = lax.broadcasted_iota(jnp.int32, (c, c), 0)
    ci = lax.broadcasted_iota(jnp.int32, (c, c), 1)
    fwd = dict(q=qf_ref, k=kf_ref, v=vf_ref, z=zf_ref, w=wf_ref, bias=bf_ref, o=of_ref, st=stf_ref, oi=oif_s,
               qi=qif_s, kv=kvf_s, dec=decf_s, mask=ci <= ri, mid=c // 2 - 1, end=c - 1)
    bwd = dict(q=qr_ref, k=kr_ref, v=vr_ref, z=zr_ref, w=wb_ref, bias=bb_ref, o=ob_ref, st=stb_ref, oi=oib_s,
               qi=qib_s, kv=kvb_s, dec=decb_s, mask=ci >= ri, mid=c // 2, end=0)
    g = [_gla_log_decay(d) for d in (fwd, bwd)]
    b = [_gla_cumulate(d, g_d, nsub) for d, g_d in zip((fwd, bwd), g)]
    scaled = [_gla_rescale(d, b_d) for d, b_d in zip((fwd, bwd), b)]
    for d, scaled_d in zip((fwd, bwd), scaled):
        _gla_products(d, scaled_d)
    for s in range(nsub):
        _gla_scan_step(fwd, s)
        _gla_scan_step(bwd, nsub - 1 - s)


def _gla(q, k, v, z, l, wf, wb, bf, bb, batch, n, cb):
    nb = n // cb
    nsub = cb // GLA_CHUNK
    fwd = lambda w: pl.BlockSpec((cb, w), lambda b, c: (b * nb + c, 0))
    bwd = lambda w: pl.BlockSpec((cb, w), lambda b, c: (b * nb + nb - 1 - c, 0))
    consts = [wf, wb, bf, bb]
    out = jax.ShapeDtypeStruct((batch * n, GLA_WIDTH), BF16)
    state = pltpu.VMEM((GLA_HEADS, GLA_DV, GLA_DK), F32)
    return pl.pallas_call(
        functools.partial(_gla_kernel, nsub=nsub),
        grid=(batch, nb),
        in_specs=[fwd(GLA_WIDTH), fwd(GLA_WIDTH), fwd(GLA_WIDTH), fwd(LANES),
                  bwd(GLA_WIDTH), bwd(GLA_WIDTH), bwd(GLA_WIDTH), bwd(LANES)] + [_const_spec(c, l) for c in consts],
        out_specs=[fwd(GLA_WIDTH), bwd(GLA_WIDTH)],
        out_shape=[out, out],
        scratch_shapes=[state, state,
                        pltpu.VMEM((cb, GLA_WIDTH), F32), pltpu.VMEM((cb, GLA_WIDTH), F32),
                        pltpu.VMEM((cb, GLA_WIDTH), BF16), pltpu.VMEM((cb, GLA_WIDTH), BF16),
                        pltpu.VMEM((nsub, GLA_HEADS, GLA_DV, GLA_DK), F32),
                        pltpu.VMEM((nsub, GLA_HEADS, GLA_DV, GLA_DK), F32),
                        pltpu.VMEM((nsub, 8, GLA_WIDTH), F32), pltpu.VMEM((nsub, 8, GLA_WIDTH), F32)],
        compiler_params=pltpu.CompilerParams(dimension_semantics=("parallel", "arbitrary"),
                                             vmem_limit_bytes=VMEM_LIMIT),
        name="gla",
    )(q, k, v, z, q, k, v, z, *consts)


def _route(logits):
    lane = lax.broadcasted_iota(jnp.int32, logits.shape, 1)
    lane_f = lane.astype(F32)
    ninf = -jnp.inf

    def first_argmax(vals, vmax):
        return jnp.min(jnp.where(vals == vmax, lane_f, float(LANES)), axis=-1, keepdims=True)

    lg = jnp.where(lane < N_GROUPS, logits, ninf)
    g_max = jnp.max(lg, axis=-1, keepdims=True)
    p_sel = 1.0 / jnp.sum(jnp.exp(lg - g_max), axis=-1, keepdims=True)
    grp = first_argmax(lg, g_max)
    e_lo = ROUTER_EXPERT_LANE0 + EXPERTS_PER_GROUP * grp
    le = jnp.where((lane_f >= e_lo) & (lane_f < e_lo + EXPERTS_PER_GROUP), logits, ninf)
    l1 = jnp.max(le, axis=-1, keepdims=True)
    i1 = first_argmax(le, l1)
    le2 = jnp.where(lane_f == i1, ninf, le)
    l2 = jnp.max(le2, axis=-1, keepdims=True)
    i2 = first_argmax(le2, l2)
    t = jnp.exp(l2 - l1)
    w1 = p_sel / (1.0 + t)
    w2 = w1 * t
    return lane_f == i1, lane_f == i2, w1, w2


def _local_sort(sel1, sel2, w1, w2):
    n = sel1.shape[0]
    ri = lax.broadcasted_iota(jnp.int32, (n, n), 0)
    ci = lax.broadcasted_iota(jnp.int32, (n, n), 1)
    before = jnp.where(ci < ri, 1.0, 0.0).astype(BF16)
    li = lax.broadcasted_iota(jnp.int32, (LANES, LANES), 0)
    lj = lax.broadcasted_iota(jnp.int32, (LANES, LANES), 1)
    lower_lanes = jnp.where(li < lj, 1.0, 0.0).astype(BF16)
    oh1 = jnp.where(sel1, 1.0, 0.0)
    oh2 = jnp.where(sel2, 1.0, 0.0)
    cnt1 = jnp.sum(oh1, axis=0, keepdims=True)
    cnt = cnt1 + jnp.sum(oh2, axis=0, keepdims=True)
    gran = jnp.floor((cnt + (MOE_GRAN - 1)) * (1.0 / MOE_GRAN))
    offg = _dot(jnp.broadcast_to(gran, (8, LANES)).astype(BF16), lower_lanes)[0:1]
    pick = lambda sel, vals: jnp.sum(jnp.where(sel, vals, 0.0), axis=-1, keepdims=True)
    pos1 = pick(sel1, _dot(before, oh1.astype(BF16)) + MOE_GRAN * offg)
    pos2 = pick(sel2, _dot(before, oh2.astype(BF16)) + cnt1 + MOE_GRAN * offg)
    lane = lax.broadcasted_iota(jnp.int32, (n, LANES), 1)
    route = jnp.where(lane == 0, pos1, jnp.where(lane == 1, pos2, jnp.where(lane == 2, w1,
                                                                             jnp.where(lane == 3, w2, 0.0))))
    to_off = jnp.where((li == lj + ROUTER_EXPERT_LANE0) & (lj < N_EXPERTS), 1.0, 0.0).astype(BF16)
    to_len = jnp.where(li == lj + ROUTER_EXPERT_LANE0 - N_EXPERTS, 1.0, 0.0).astype(BF16)
    meta = (_dot(jnp.broadcast_to(offg, (8, LANES)).astype(BF16), to_off)
            + _dot(jnp.broadcast_to(gran, (8, LANES)).astype(BF16), to_len))
    return route, meta.astype(jnp.int32)


def _post_kernel(x_ref, ya_ref, of_ref, ob_ref, rb_ref, ga_ref, gb_ref, woa_ref, wob_ref, wout_ref,
                 gn_ref, nf_ref, wrh_ref, wrl_ref, br_ref, x1_ref, xn_ref, route_ref, meta_ref):
    subs = [slice(u * MOE_SUB, (u + 1) * MOE_SUB) for u in range(x_ref.shape[0] // MOE_SUB)]

    def gla_out(rows):
        o = of_ref[rows, :].astype(F32) + ob_ref[rows, :].astype(F32)
        parts = []
        for h in range(GLA_HEADS):
            oh = o[:, h * GLA_DV:(h + 1) * GLA_DV]
            ms = jnp.mean(oh * oh, axis=-1, keepdims=True)
            parts.append(oh * lax.rsqrt(ms + EPS) * gn_ref[...])
        return (jnp.concatenate(parts, axis=-1) * rb_ref[rows, :].astype(F32)).astype(BF16)

    ya = [_dot(ya_ref[rows, :], woa_ref[...]) for rows in subs]
    yb = [_dot(gla_out(rows), wob_ref[...]) for rows in subs]
    y = [ga_ref[rows, :] * ya_u.astype(BF16) + gb_ref[rows, :] * yb_u.astype(BF16)
         for rows, ya_u, yb_u in zip(subs, ya, yb)]
    x1 = [x_ref[rows, :] + _dot(y_u, wout_ref[...]) for rows, y_u in zip(subs, y)]
    t_parts = []
    for rows, x1_u in zip(subs, x1):
        x1_ref[rows, :] = x1_u
        ms = jnp.mean(x1_u * x1_u, axis=-1, keepdims=True)
        thi, tlo = _split_bf16(x1_u * lax.rsqrt(ms + EPS) * nf_ref[...])
        xn_ref[rows, :] = thi
        t_parts.append((thi, tlo))
    logits = [_dot(thi, wrh_ref[...]) + _dot(thi, wrl_ref[...]) + _dot(tlo, wrh_ref[...]) + br_ref[...]
              for thi, tlo in t_parts]
    for u, (rows, logits_u) in enumerate(zip(subs, logits)):
        route_ref[rows, :], meta_ref[u] = _local_sort(*_route(logits_u))


def _post(x, ya, of, ob, rb, ga, gb, l, woa, wob, wout, gn, nf, wrh, wrl, br, tm):
    t = x.shape[0]
    row = lambda w: pl.BlockSpec((tm, w), lambda i: (i, 0))
    consts = [woa, wob, wout, gn, nf, wrh, wrl, br]
    return pl.pallas_call(
        _post_kernel,
        grid=(t // tm,),
        in_specs=[row(D_MODEL), row(NA_WIDTH), row(GLA_WIDTH), row(GLA_WIDTH), row(GLA_WIDTH),
                  row(D_MODEL), row(D_MODEL)] + [_const_spec(c, l) for c in consts],
        out_specs=[row(D_MODEL), row(D_MODEL), row(LANES),
                   pl.BlockSpec((tm // MOE_SUB, 8, LANES), lambda i: (i, 0, 0))],
        out_shape=[jax.ShapeDtypeStruct((t, D_MODEL), F32), jax.ShapeDtypeStruct((t, D_MODEL), BF16),
                   jax.ShapeDtypeStruct((t, LANES), F32),
                   jax.ShapeDtypeStruct((t // MOE_SUB, 8, LANES), jnp.int32)],
        compiler_params=pltpu.CompilerParams(dimension_semantics=("parallel",), vmem_limit_bytes=VMEM_LIMIT),
        name="post",
    )(x, ya, of, ob, rb, ga, gb, *consts)


def _moe_kernel(meta_ref, xn_ref, route_ref, x1_ref, wg_ref, wu_ref, wd_ref, o_ref, loc, stage, *, nsub):
    i = pl.program_id(0)
    e = pl.program_id(1)
    sub_rows = lambda u: slice(u * MOE_SUB, (u + 1) * MOE_SUB)
    gran_rows = lambda g: pl.ds(pl.multiple_of(g * MOE_GRAN, MOE_GRAN), MOE_GRAN)
    meta = lambda u, k: meta_ref[((i * nsub + u) * 2 + k) * N_EXPERTS + e]

    @pl.when(e == 0)
    def _dispatch():
        stage[...] = jnp.zeros_like(stage)
        sel = lax.broadcasted_iota(jnp.int32, (8, LANES), 0) == lax.broadcasted_iota(jnp.int32, (8, LANES), 1)
        sel = jnp.where(sel, 1.0, 0.0).astype(BF16)
        row_id = lax.broadcasted_iota(jnp.int32, (MOE_LOC_ROWS, MOE_SUB), 0).astype(F32)
        for u in range(nsub):
            rhi, rlo = _split_bf16(route_ref[sub_rows(u), :])
            pos_t = (lax.dot_general(sel, rhi, _NT, preferred_element_type=F32)
                     + lax.dot_general(sel, rlo, _NT, preferred_element_type=F32))
            onehot = jnp.where(row_id == pos_t[0:1], 1.0, 0.0) + jnp.where(row_id == pos_t[1:2], 1.0, 0.0)
            loc[u] = _dot(onehot.astype(BF16), xn_ref[sub_rows(u), :]).astype(BF16)

    def move_granules(to_stage):
        dst = 0
        for u in range(nsub):
            off, n = meta(u, 0), meta(u, 1)

            def body(g, carry, u=u, off=off, dst=dst):
                if to_stage:
                    stage[gran_rows(dst + g), :] = loc[u, gran_rows(off + g), :]
                else:
                    loc[u, gran_rows(off + g), :] = stage[gran_rows(dst + g), :]
                return carry

            lax.fori_loop(0, n, body, 0)
            dst = dst + n
        return dst

    total = move_granules(True)

    def block(b, carry):
        rows = pl.ds(pl.multiple_of(b * MOE_ROW_BLOCK, MOE_ROW_BLOCK), MOE_ROW_BLOCK)
        xs = stage[rows, :]
        hg = _dot(xs, wg_ref[0])
        h = (hg * _sigmoid(hg) * _dot(xs, wu_ref[0])).astype(BF16)
        stage[rows, :] = _dot(h, wd_ref[0]).astype(BF16)
        return carry

    lax.fori_loop(0, (total * MOE_GRAN + MOE_ROW_BLOCK - 1) // MOE_ROW_BLOCK, block, 0)
    move_granules(False)

    @pl.when(e == N_EXPERTS - 1)
    def _combine():
        col_id = lax.broadcasted_iota(jnp.int32, (MOE_SUB, MOE_LOC_ROWS), 1).astype(F32)
        for u in range(nsub):
            r = route_ref[sub_rows(u), :]
            pick = jnp.concatenate([jnp.where(col_id == r[:, 0:1], 1.0, 0.0).astype(BF16),
                                    jnp.where(col_id == r[:, 1:2], 1.0, 0.0).astype(BF16)], axis=0)
            y = _dot(pick, loc[u])
            o_ref[sub_rows(u), :] = x1_ref[sub_rows(u), :] + r[:, 2:3] * y[:MOE_SUB] + r[:, 3:4] * y[MOE_SUB:]


def _moe(meta, xn, route, x1, l, wg, wu, wd, tb):
    t = xn.shape[0]
    nsub = tb // MOE_SUB
    row = lambda w, **kw: pl.BlockSpec((tb, w), lambda i, e, m: (i, 0), **kw)
    wspec = lambda a, b: pl.BlockSpec((None, 1, a, b), lambda i, e, m: (l, e, 0, 0))
    return pl.pallas_call(
        functools.partial(_moe_kernel, nsub=nsub),
        grid_spec=pltpu.PrefetchScalarGridSpec(
            num_scalar_prefetch=1,
            grid=(t // tb, N_EXPERTS),
            in_specs=[row(D_MODEL), row(LANES), row(D_MODEL, pipeline_mode=pl.Buffered(1)),
                      wspec(D_MODEL, D_EXPERT), wspec(D_MODEL, D_EXPERT), wspec(D_EXPERT, D_MODEL)],
            out_specs=row(D_MODEL, pipeline_mode=pl.Buffered(1)),
            scratch_shapes=[pltpu.VMEM((nsub, MOE_LOC_ROWS, D_MODEL), BF16),
                            pltpu.VMEM((pl.cdiv(tb, MOE_ROW_BLOCK) * MOE_ROW_BLOCK, D_MODEL), BF16)]),
        out_shape=jax.ShapeDtypeStruct((t, D_MODEL), F32),
        compiler_params=pltpu.CompilerParams(dimension_semantics=("parallel", "arbitrary"),
                                             vmem_limit_bytes=VMEM_LIMIT),
        name="moe",
    )(meta, xn, route, x1, wg, wu, wd)


def _prep_params(norm_mix, w_in, q_norm_a, k_norm_a, rpb, w_decay, b_decay, gla_norm, w_o_a, w_o_b, w_out,
                 norm_ffn, w_router_g, b_router_g, w_router_e, b_router_e, w_gate, w_up, w_down):
    depth = w_in.shape[0]
    f32 = lambda a: a.astype(F32)
    pad_lanes = lambda a, before, total: jnp.pad(f32(a), [(0, 0)] * (a.ndim - 1) + [(before, total - before - a.shape[-1])])
    w_aligned = jnp.concatenate([w_in[:, :, :IN_Z_COL0 + 2 * GLA_RANK],
                                 jnp.zeros((depth, D_MODEL, IN_GATE_COL0 - IN_Z_COL0 - 2 * GLA_RANK), w_in.dtype),
                                 w_in[:, :, IN_Z_COL0 + 2 * GLA_RANK:]], axis=-1).astype(BF16)
    head = np.arange(NA_WIDTH) // NA_HEAD_DIM
    hsum = jnp.asarray(np.broadcast_to(head[:, None] == head[None, :], (depth, NA_WIDTH, NA_WIDTH)), BF16)
    w_router = pad_lanes(jnp.concatenate([w_router_g, w_router_e], axis=-1), ROUTER_GROUP_LANE0, LANES)
    b_router = pad_lanes(jnp.concatenate([b_router_g, b_router_e], axis=-1), ROUTER_GROUP_LANE0, LANES)
    wrh, wrl = _split_bf16(w_router)
    wdec = lambda d: jnp.pad(f32(w_decay[:, d]), ((0, 0), (d * GLA_RANK, LANES - (d + 1) * GLA_RANK), (0, 0))).astype(BF16)
    return dict(
        in_proj=(f32(norm_mix)[:, None, :], w_aligned,
                 jnp.tile(f32(q_norm_a), (1, NA_HEADS))[:, None, :] * (NA_HEAD_DIM ** -0.5 * LOG2E),
                 jnp.tile(f32(k_norm_a), (1, NA_HEADS))[:, None, :], hsum),
        na=(_na_bias_table(rpb),),
        gla=(wdec(0), wdec(1), f32(b_decay)[:, 0:1, :], f32(b_decay)[:, 1:2, :]),
        post=(w_o_a.astype(BF16), w_o_b.astype(BF16), w_out.astype(BF16), f32(gla_norm)[:, None, :],
              f32(norm_ffn)[:, None, :], wrh, wrl, b_router[:, None, :]),
        moe=(w_gate.astype(BF16), w_up.astype(BF16), w_down.astype(BF16)),
    )


def _tile(t, want):
    while t % want:
        want //= 2
    return want


def _trunk(x, p, depth):
    batch, n, _ = x.shape
    t = batch * n
    x = x.reshape(t, D_MODEL)
    for l in range(depth):
        qa, ka, va, qb, kb, vb, rb, z, ga, gb = _in_proj(x, l, *p["in_proj"], _tile(t, 512))
        ya = _na(qa, ka, va, l, *p["na"], batch, n)
        of, ob = _gla(qb, kb, vb, z, l, *p["gla"], batch, n, GLA_BLOCK_TOK)
        x1, xn, route, meta = _post(x, ya, of, ob, rb, ga, gb, l, *p["post"], _tile(t, 512))
        meta = meta[:, 0, :2 * N_EXPERTS].reshape(-1)
        x = _moe(meta, xn, route, x1, l, *p["moe"], _tile(t, MOE_TILE))
    return x.reshape(batch, n, D_MODEL)


def kernel(x_prompt, x_sample, norm_mix, w_in, q_norm_a, k_norm_a, rpb, w_decay, b_decay, gla_norm, w_o_a, w_o_b,
           w_out, norm_ffn, w_router_g, b_router_g, w_router_e, b_router_e, w_gate, w_up, w_down):
    p = _prep_params(norm_mix, w_in, q_norm_a, k_norm_a, rpb, w_decay, b_decay, gla_norm, w_o_a, w_o_b, w_out,
                     norm_ffn, w_router_g, b_router_g, w_router_e, b_router_e, w_gate, w_up, w_down)
    depth = w_in.shape[0]
    return (_trunk(x_prompt, p, depth), _trunk(x_sample, p, depth))
```

```python
import functools

import jax
import jax.numpy as jnp
import numpy as np
from jax import lax
from jax.experimental import pallas as pl
from jax.experimental.pallas import tpu as pltpu

F32 = jnp.float32
BF16 = jnp.bfloat16

D_MODEL = 1024
GRID_W = 64
NA_HEADS = 8
NA_HEAD_DIM = 64
NA_WIDTH = NA_HEADS * NA_HEAD_DIM
NA_ROWS = 8
NA_COLS = 16
RPB_ROWS = 2 * NA_ROWS - 1
RPB_COLS = 2 * NA_COLS - 1
GLA_HEADS = 4
GLA_DK = 128
GLA_DV = 128
GLA_WIDTH = GLA_HEADS * GLA_DV
GLA_RANK = 16
GLA_GATE_NORM = 16.0
GLA_CHUNK = 64
GLA_BLOCK_TOK = 512
N_GROUPS = 4
EXPERTS_PER_GROUP = 4
N_EXPERTS = N_GROUPS * EXPERTS_PER_GROUP
D_EXPERT = 512
EPS = 1e-6
NEG = -1e30

LANES = 128
IN_Z_COL0 = 7 * 512
IN_GATE_COL0 = IN_Z_COL0 + LANES
MOE_SUB = 256
MOE_GRAN = 16
MOE_LOC_ROWS = 768
MOE_TILE = 2048
MOE_ROW_BLOCK = 384
LOG2E = 1.4426950408889634
EXP_CLAMP = 80.0
ROUTER_GROUP_LANE0 = 0
ROUTER_EXPERT_LANE0 = N_GROUPS
VMEM_LIMIT = 56 * 1024 * 1024

_NT = (((1,), (1,)), ((), ()))
_TN = (((0,), (0,)), ((), ()))


def _dot(a, b):
    return jnp.dot(a, b, preferred_element_type=F32)


def _split_bf16(x):
    hi = x.astype(BF16)
    lo = (x - hi.astype(F32)).astype(BF16)
    return hi, lo


def _sigmoid(x):
    return 1.0 / (1.0 + jnp.exp(-x))


def _const_spec(stacked, l):
    tail = stacked.shape[1:]
    return pl.BlockSpec((None,) + tail, lambda *_: (l,) + (0,) * len(tail), pipeline_mode=pl.Buffered(1))


def _in_proj_kernel(x_ref, nrm_ref, w_ref, qg_ref, kg_ref, wdec_ref, bdec_ref,
                    qa_ref, ka_ref, va_ref, qb_ref, kb_ref, vb_ref, rb_ref, gf_ref, gr_ref, ga_ref, gb_ref):
    x = x_ref[...]
    ms = jnp.mean(x * x, axis=-1, keepdims=True)
    xn = (x * lax.rsqrt(ms + EPS) * nrm_ref[...]).astype(BF16)
    low_half = lax.broadcasted_iota(jnp.int32, (x.shape[0], LANES), 1) < NA_HEAD_DIM

    def proj(lo, width=512):
        return _dot(xn, w_ref[:, lo:lo + width])

    def head_norm(a, gain_ref):
        parts = []
        for p in range(NA_WIDTH // LANES):
            sq = a[:, p * LANES:(p + 1) * LANES]
            sq = sq * sq
            s_lo = jnp.sum(jnp.where(low_half, sq, 0.0), axis=-1, keepdims=True)
            s_hi = jnp.sum(jnp.where(low_half, 0.0, sq), axis=-1, keepdims=True)
            parts.append(jnp.where(low_half, s_lo, s_hi))
        ss = jnp.concatenate(parts, axis=-1)
        return a * lax.rsqrt(ss * (1.0 / NA_HEAD_DIM) + EPS) * gain_ref[...]

    z = proj(IN_Z_COL0, LANES).astype(BF16)
    qa = proj(0)
    ka = proj(512)
    va_ref[...] = proj(1024).astype(BF16)
    qa_ref[...] = head_norm(qa, qg_ref).astype(BF16)
    qb_ref[...] = (proj(1536) * (GLA_DK ** -0.5)).astype(BF16)
    ka_ref[...] = head_norm(ka, kg_ref).astype(BF16)
    kb_ref[...] = proj(2048).astype(BF16)
    vb_ref[...] = proj(2560).astype(BF16)
    r = proj(3072)
    logits = [_dot(z, wdec_ref[d]) + bdec_ref[d] for d in range(2)]
    rb_ref[...] = (r * _sigmoid(r)).astype(BF16)
    ga = proj(IN_GATE_COL0, D_MODEL)
    for lg, g_ref in zip(logits, (gf_ref, gr_ref)):
        g_ref[...] = (jnp.minimum(lg, 0.0) - jnp.log(1.0 + jnp.exp(-jnp.abs(lg)))) * (1.0 / GLA_GATE_NORM)
    gb = proj(IN_GATE_COL0 + D_MODEL, D_MODEL)
    ga_ref[...] = _sigmoid(ga).astype(BF16)
    gb_ref[...] = _sigmoid(gb).astype(BF16)


def _in_proj(x, l, nrm, w, qg, kg, wdec, bdec, tm):
    t = x.shape[0]
    row = lambda w: pl.BlockSpec((tm, w), lambda i: (i, 0))
    out_w = [512] * 7 + [GLA_WIDTH, GLA_WIDTH, D_MODEL, D_MODEL]
    out_dt = [BF16] * 7 + [F32, F32, BF16, BF16]
    consts = [nrm, w, qg, kg, wdec, bdec]
    return pl.pallas_call(
        _in_proj_kernel,
        grid=(t // tm,),
        in_specs=[row(D_MODEL)] + [_const_spec(c, l) for c in consts],
        out_specs=[row(w) for w in out_w],
        out_shape=[jax.ShapeDtypeStruct((t, w), dt) for w, dt in zip(out_w, out_dt)],
        compiler_params=pltpu.CompilerParams(dimension_semantics=("parallel",), vmem_limit_bytes=VMEM_LIMIT),
        name="in_proj",
    )(x, *consts)


NA_BLOCK_ROWS = 8
NA_BLOCK_TOK = NA_BLOCK_ROWS * GRID_W
NA_WINDOW_TOK = NA_ROWS * GRID_W


def _na_kernel(q_ref, kp_ref, kc_ref, kn_ref, vp_ref, vc_ref, vn_ref, bias_ref, o_ref, kwin, vwin, s_scr, *, rows):
    i = pl.program_id(1)
    kwin[0:NA_BLOCK_TOK] = kp_ref[...]
    kwin[NA_BLOCK_TOK:2 * NA_BLOCK_TOK] = kc_ref[...]
    kwin[2 * NA_BLOCK_TOK:3 * NA_BLOCK_TOK] = kn_ref[...]
    vwin[0:NA_BLOCK_TOK] = vp_ref[...]
    vwin[NA_BLOCK_TOK:2 * NA_BLOCK_TOK] = vc_ref[...]
    vwin[2 * NA_BLOCK_TOK:3 * NA_BLOCK_TOK] = vn_ref[...]
    low_half = lax.broadcasted_iota(jnp.int32, (GRID_W, LANES), 1) < NA_HEAD_DIM
    pairs = range(NA_HEADS // 2)
    cols = lambda p: slice(p * LANES, (p + 1) * LANES)
    qrows = lambda j: slice(j * GRID_W, (j + 1) * GRID_W)

    def window(j):
        r = i * NA_BLOCK_ROWS + j
        rs = jnp.clip(r - NA_ROWS // 2, 0, rows - NA_ROWS)
        start = pl.multiple_of((rs - i * NA_BLOCK_ROWS + NA_BLOCK_ROWS) * GRID_W, GRID_W)
        return start, rs - r + NA_ROWS - 1

    def scores(j):
        start, bias_off = window(j)
        for p in pairs:
            qp = q_ref[qrows(j), cols(p)]
            zero = jnp.zeros_like(qp)
            q2 = jnp.concatenate([jnp.where(low_half, qp, zero), jnp.where(low_half, zero, qp)], axis=0)
            s = lax.dot_general(q2, kwin[pl.ds(start, NA_WINDOW_TOK), cols(p)], _NT, preferred_element_type=F32)
            s_scr[j % 2, p] = s + bias_ref[bias_off, p]

    def attend(j):
        start, _ = window(j)
        for p in pairs:
            s = s_scr[j % 2, p]
            e = jnp.exp2(s - jnp.max(s, axis=-1, keepdims=True))
            l = jnp.sum(e, axis=-1, keepdims=True)
            pv = _dot(e.astype(BF16), vwin[pl.ds(start, NA_WINDOW_TOK), cols(p)]) * (1.0 / l)
            o_ref[qrows(j), cols(p)] = jnp.where(low_half, pv[:GRID_W], pv[GRID_W:]).astype(BF16)

    scores(0)
    for j in range(NA_BLOCK_ROWS):
        if j + 1 < NA_BLOCK_ROWS:
            scores(j + 1)
        attend(j)


def _na(q, k, v, l, bias, batch, n):
    rows = n // GRID_W
    nblk = rows // NA_BLOCK_ROWS
    spec = lambda f: pl.BlockSpec((NA_BLOCK_TOK, NA_WIDTH), lambda b, i: (b * nblk + f(i), 0))
    cur = spec(lambda i: i)
    prev = spec(lambda i: jnp.maximum(i - 1, 0))
    nxt = spec(lambda i: jnp.minimum(i + 1, nblk - 1))
    return pl.pallas_call(
        functools.partial(_na_kernel, rows=rows),
        grid=(batch, nblk),
        in_specs=[cur, prev, cur, nxt, prev, cur, nxt, _const_spec(bias, l)],
        out_specs=cur,
        out_shape=jax.ShapeDtypeStruct((batch * n, NA_WIDTH), BF16),
        scratch_shapes=[pltpu.VMEM((3 * NA_BLOCK_TOK, NA_WIDTH), BF16),
                        pltpu.VMEM((3 * NA_BLOCK_TOK, NA_WIDTH), BF16),
                        pltpu.VMEM((2, NA_HEADS // 2, 2 * GRID_W, NA_WINDOW_TOK), F32)],
        compiler_params=pltpu.CompilerParams(dimension_semantics=("parallel", "parallel"),
                                             vmem_limit_bytes=VMEM_LIMIT),
        name="na",
    )(q, k, k, k, v, v, v, bias)


def _na_bias_table(rpb):
    cols = np.arange(GRID_W)
    col_start = np.clip(cols - NA_COLS // 2, 0, GRID_W - NA_COLS)
    col_mask = (cols[None, :] >= col_start[:, None]) & (cols[None, :] < col_start[:, None] + NA_COLS)
    col_idx = np.clip(cols[None, :] - cols[:, None] + NA_COLS - 1, 0, RPB_COLS - 1)
    onehot = (col_idx[None] == np.arange(RPB_COLS)[:, None, None]).astype(np.float32)
    bias = jnp.einsum("lhrk,kqc->lhqrc", rpb.astype(F32) * LOG2E, onehot, precision=lax.Precision.HIGHEST)
    masked = jnp.where(col_mask[:, None, :], bias, NEG)
    depth = rpb.shape[0]
    slabs = [masked[:, :, :, o:o + NA_ROWS].reshape(depth, NA_HEADS // 2, 2 * GRID_W, NA_WINDOW_TOK)
             for o in range(NA_ROWS)]
    return jnp.stack(slabs, axis=1)


def _gla_chunk_rows(s):
    return slice(s * GLA_CHUNK, (s + 1) * GLA_CHUNK)


def _gla_head_cols(h):
    return slice(h * GLA_DK, (h + 1) * GLA_DK)


def _gla_cumulate(d, g, nsub):
    tri = jnp.where(d["mask"], 1.0, 0.0).astype(BF16)
    ghi, glo = _split_bf16(g)
    return [_dot(tri, ghi[_gla_chunk_rows(s)]) + _dot(tri, glo[_gla_chunk_rows(s)]) for s in range(nsub)]


def _gla_rescale(d, b_chunks):
    mid, end = d["mid"], d["end"]
    scaled = []
    for s, b in enumerate(b_chunks):
        rows = _gla_chunk_rows(s)
        b_mid = b[mid:mid + 1, :]
        b_end = b[end:end + 1, :]
        qe = d["q"][rows, :] * jnp.exp(jnp.minimum(b - b_mid, EXP_CLAMP)).astype(BF16)
        ke = d["k"][rows, :] * jnp.exp(jnp.minimum(b_mid - b, EXP_CLAMP)).astype(BF16)
        d["qi"][rows, :] = qe * jnp.exp(b_mid).astype(BF16)
        d["dec"][s] = jnp.broadcast_to(jnp.exp(b_end), (8, GLA_WIDTH))
        k_out = ke * jnp.exp(b_end - b_mid).astype(BF16)
        scaled.append((qe, ke, k_out))
    return scaled


def _gla_products(d, scaled):
    pairs = [(s, h) for s in range(len(scaled)) for h in range(GLA_HEADS)]
    v = lambda s, h: d["v"][_gla_chunk_rows(s), _gla_head_cols(h)]
    a = [lax.dot_general(scaled[s][0][:, _gla_head_cols(h)], scaled[s][1][:, _gla_head_cols(h)], _NT,
                         preferred_element_type=F32) for s, h in pairs]
    for s, h in pairs:
        d["kv"][s, h] = lax.dot_general(v(s, h), scaled[s][2][:, _gla_head_cols(h)], _TN,
                                        preferred_element_type=F32)
    for (s, h), a_sh in zip(pairs, a):
        d["oi"][_gla_chunk_rows(s), _gla_head_cols(h)] = _dot(jnp.where(d["mask"], a_sh, 0.0).astype(BF16), v(s, h))


def _gla_scan_step(d, s):
    rows = _gla_chunk_rows(s)
    dec = d["dec"][s]
    states = [d["st"][h] for h in range(GLA_HEADS)]
    inter = [lax.dot_general(d["qi"][rows, _gla_head_cols(h)], states[h].astype(BF16), _NT,
                             preferred_element_type=F32) for h in range(GLA_HEADS)]
    for h in range(GLA_HEADS):
        cols = _gla_head_cols(h)
        d["st"][h] = states[h] * dec[0:1, cols] + d["kv"][s, h]
        d["o"][rows, cols] = (d["oi"][rows, cols] + inter[h]).astype(BF16)


def _gla_kernel(qf_ref, kf_ref, vf_ref, gf_ref, qr_ref, kr_ref, vr_ref, gr_ref,
                of_ref, ob_ref, stf_ref, stb_ref, oif_s, oib_s, qif_s, qib_s, kvf_s, kvb_s, decf_s, decb_s, *, nsub):
    @pl.when(pl.program_id(1) == 0)
    def _():
        stf_ref[...] = jnp.zeros_like(stf_ref)
        stb_ref[...] = jnp.zeros_like(stb_ref)

    c = GLA_CHUNK
    ri = lax.broadcasted_iota(jnp.int32, (c, c), 0)
    ci = lax.broadcasted_iota(jnp.int32, (c, c), 1)
    fwd = dict(q=qf_ref, k=kf_ref, v=vf_ref, g=gf_ref, o=of_ref, st=stf_ref, oi=oif_s,
               qi=qif_s, kv=kvf_s, dec=decf_s, mask=ci <= ri, mid=c // 2 - 1, end=c - 1)
    bwd = dict(q=qr_ref, k=kr_ref, v=vr_ref, g=gr_ref, o=ob_ref, st=stb_ref, oi=oib_s,
               qi=qib_s, kv=kvb_s, dec=decb_s, mask=ci >= ri, mid=c // 2, end=0)
    b = [_gla_cumulate(d, d["g"][...], nsub) for d in (fwd, bwd)]
    scaled = [_gla_rescale(d, b_d) for d, b_d in zip((fwd, bwd), b)]
    for d, scaled_d in zip((fwd, bwd), scaled):
        _gla_products(d, scaled_d)
    for s in range(nsub):
        _gla_scan_step(fwd, s)
        _gla_scan_step(bwd, nsub - 1 - s)


def _gla(q, k, v, gf, gr, batch, n, cb):
    nb = n // cb
    nsub = cb // GLA_CHUNK
    fwd = lambda w: pl.BlockSpec((cb, w), lambda b, c: (b * nb + c, 0))
    bwd = lambda w: pl.BlockSpec((cb, w), lambda b, c: (b * nb + nb - 1 - c, 0))
    out = jax.ShapeDtypeStruct((batch * n, GLA_WIDTH), BF16)
    state = pltpu.VMEM((GLA_HEADS, GLA_DV, GLA_DK), F32)
    return pl.pallas_call(
        functools.partial(_gla_kernel, nsub=nsub),
        grid=(batch, nb),
        in_specs=[fwd(GLA_WIDTH)] * 4 + [bwd(GLA_WIDTH)] * 4,
        out_specs=[fwd(GLA_WIDTH), bwd(GLA_WIDTH)],
        out_shape=[out, out],
        scratch_shapes=[state, state,
                        pltpu.VMEM((cb, GLA_WIDTH), F32), pltpu.VMEM((cb, GLA_WIDTH), F32),
                        pltpu.VMEM((cb, GLA_WIDTH), BF16), pltpu.VMEM((cb, GLA_WIDTH), BF16),
                        pltpu.VMEM((nsub, GLA_HEADS, GLA_DV, GLA_DK), F32),
                        pltpu.VMEM((nsub, GLA_HEADS, GLA_DV, GLA_DK), F32),
                        pltpu.VMEM((nsub, 8, GLA_WIDTH), F32), pltpu.VMEM((nsub, 8, GLA_WIDTH), F32)],
        compiler_params=pltpu.CompilerParams(dimension_semantics=("parallel", "arbitrary"),
                                             vmem_limit_bytes=VMEM_LIMIT),
        name="gla",
    )(q, k, v, gf, q, k, v, gr)


def _route(logits):
    lane = lax.broadcasted_iota(jnp.int32, logits.shape, 1)
    lane_f = lane.astype(F32)
    ninf = -jnp.inf

    def first_argmax(vals, vmax):
        return jnp.min(jnp.where(vals == vmax, lane_f, float(LANES)), axis=-1, keepdims=True)

    lg = jnp.where(lane < N_GROUPS, logits, ninf)
    g_max = jnp.max(lg, axis=-1, keepdims=True)
    p_sel = 1.0 / jnp.sum(jnp.exp(lg - g_max), axis=-1, keepdims=True)
    grp = first_argmax(lg, g_max)
    e_lo = ROUTER_EXPERT_LANE0 + EXPERTS_PER_GROUP * grp
    le = jnp.where((lane_f >= e_lo) & (lane_f < e_lo + EXPERTS_PER_GROUP), logits, ninf)
    l1 = jnp.max(le, axis=-1, keepdims=True)
    i1 = first_argmax(le, l1)
    le2 = jnp.where(lane_f == i1, ninf, le)
    l2 = jnp.max(le2, axis=-1, keepdims=True)
    i2 = first_argmax(le2, l2)
    t = jnp.exp(l2 - l1)
    w1 = p_sel / (1.0 + t)
    w2 = w1 * t
    return lane_f == i1, lane_f == i2, w1, w2


def _local_sort(sel1, sel2, w1, w2):
    n = sel1.shape[0]
    ri = lax.broadcasted_iota(jnp.int32, (n, n), 0)
    ci = lax.broadcasted_iota(jnp.int32, (n, n), 1)
    before = jnp.where(ci < ri, 1.0, 0.0).astype(BF16)
    li = lax.broadcasted_iota(jnp.int32, (LANES, LANES), 0)
    lj = lax.broadcasted_iota(jnp.int32, (LANES, LANES), 1)
    lower_lanes = jnp.where(li < lj, 1.0, 0.0).astype(BF16)
    oh1 = jnp.where(sel1, 1.0, 0.0)
    oh2 = jnp.where(sel2, 1.0, 0.0)
    cnt1 = jnp.sum(oh1, axis=0, keepdims=True)
    cnt = cnt1 + jnp.sum(oh2, axis=0, keepdims=True)
    gran = jnp.floor((cnt + (MOE_GRAN - 1)) * (1.0 / MOE_GRAN))
    offg = _dot(jnp.broadcast_to(gran, (8, LANES)).astype(BF16), lower_lanes)[0:1]
    pick = lambda sel, vals: jnp.sum(jnp.where(sel, vals, 0.0), axis=-1, keepdims=True)
    pos1 = pick(sel1, _dot(before, oh1.astype(BF16)) + MOE_GRAN * offg)
    pos2 = pick(sel2, _dot(before, oh2.astype(BF16)) + cnt1 + MOE_GRAN * offg)
    lane = lax.broadcasted_iota(jnp.int32, (n, LANES), 1)
    route = jnp.where(lane == 0, pos1, jnp.where(lane == 1, pos2, jnp.where(lane == 2, w1,
                                                                             jnp.where(lane == 3, w2, 0.0))))
    to_off = jnp.where((li == lj + ROUTER_EXPERT_LANE0) & (lj < N_EXPERTS), 1.0, 0.0).astype(BF16)
    to_len = jnp.where(li == lj + ROUTER_EXPERT_LANE0 - N_EXPERTS, 1.0, 0.0).astype(BF16)
    meta = (_dot(jnp.broadcast_to(offg, (8, LANES)).astype(BF16), to_off)
            + _dot(jnp.broadcast_to(gran, (8, LANES)).astype(BF16), to_len))
    return route, meta.astype(jnp.int32)


def _post_kernel(x_ref, ya_ref, of_ref, ob_ref, rb_ref, ga_ref, gb_ref, woa_ref, wob_ref, wout_ref,
                 gn_ref, nf_ref, wrh_ref, wrl_ref, br_ref, x1_ref, xn_ref, route_ref, meta_ref):
    nsub = x_ref.shape[0] // MOE_SUB
    subs = [slice(u * MOE_SUB, (u + 1) * MOE_SUB) for u in range(nsub)]
    val = [dict() for _ in range(nsub)]

    def branches(u):
        rows = subs[u]
        o = of_ref[rows, :].astype(F32) + ob_ref[rows, :].astype(F32)
        parts = []
        for h in range(GLA_HEADS):
            oh = o[:, h * GLA_DV:(h + 1) * GLA_DV]
            ms = jnp.mean(oh * oh, axis=-1, keepdims=True)
            parts.append(oh * lax.rsqrt(ms + EPS) * gn_ref[...])
        on = (jnp.concatenate(parts, axis=-1) * rb_ref[rows, :].astype(F32)).astype(BF16)
        val[u]["ya"] = _dot(ya_ref[rows, :], woa_ref[...])
        val[u]["yb"] = _dot(on, wob_ref[...])

    def merge(u):
        rows = subs[u]
        y = ga_ref[rows, :] * val[u]["ya"].astype(BF16) + gb_ref[rows, :] * val[u]["yb"].astype(BF16)
        val[u]["x1"] = x_ref[rows, :] + _dot(y, wout_ref[...])

    def norm_logits(u):
        rows = subs[u]
        x1 = val[u]["x1"]
        x1_ref[rows, :] = x1
        ms = jnp.mean(x1 * x1, axis=-1, keepdims=True)
        thi, tlo = _split_bf16(x1 * lax.rsqrt(ms + EPS) * nf_ref[...])
        xn_ref[rows, :] = thi
        val[u]["logits"] = (_dot(thi, wrh_ref[...]) + _dot(thi, wrl_ref[...]) + _dot(tlo, wrh_ref[...])
                            + br_ref[...])

    def route(u):
        route_ref[subs[u], :], meta_ref[u] = _local_sort(*_route(val[u]["logits"]))

    stages = [branches, merge, norm_logits, route]
    for step in range(nsub + len(stages) - 1):
        for u in range(nsub):
            if 0 <= step - u < len(stages):
                stages[step - u](u)


def _post(x, ya, of, ob, rb, ga, gb, l, woa, wob, wout, gn, nf, wrh, wrl, br, tm):
    t = x.shape[0]
    row = lambda w: pl.BlockSpec((tm, w), lambda i: (i, 0))
    consts = [woa, wob, wout, gn, nf, wrh, wrl, br]
    return pl.pallas_call(
        _post_kernel,
        grid=(t // tm,),
        in_specs=[row(D_MODEL), row(NA_WIDTH), row(GLA_WIDTH), row(GLA_WIDTH), row(GLA_WIDTH),
                  row(D_MODEL), row(D_MODEL)] + [_const_spec(c, l) for c in consts],
        out_specs=[row(D_MODEL), row(D_MODEL), row(LANES),
                   pl.BlockSpec((tm // MOE_SUB, 8, LANES), lambda i: (i, 0, 0))],
        out_shape=[jax.ShapeDtypeStruct((t, D_MODEL), F32), jax.ShapeDtypeStruct((t, D_MODEL), BF16),
                   jax.ShapeDtypeStruct((t, LANES), F32),
                   jax.ShapeDtypeStruct((t // MOE_SUB, 8, LANES), jnp.int32)],
        compiler_params=pltpu.CompilerParams(dimension_semantics=("parallel",), vmem_limit_bytes=VMEM_LIMIT),
        name="post",
    )(x, ya, of, ob, rb, ga, gb, *consts)


def _moe_kernel(meta_ref, xn_ref, route_ref, x1_ref, wg_ref, wu_ref, wd_ref, o_ref, loc, stage, *, nsub):
    i = pl.program_id(0)
    e = pl.program_id(1)
    sub_rows = lambda u: slice(u * MOE_SUB, (u + 1) * MOE_SUB)
    gran_rows = lambda g: pl.ds(pl.multiple_of(g * MOE_GRAN, MOE_GRAN), MOE_GRAN)
    meta = lambda u, k: meta_ref[((i * nsub + u) * 2 + k) * N_EXPERTS + e]

    @pl.when(e == 0)
    def _dispatch():
        stage[...] = jnp.zeros_like(stage)
        sel = lax.broadcasted_iota(jnp.int32, (8, LANES), 0) == lax.broadcasted_iota(jnp.int32, (8, LANES), 1)
        sel = jnp.where(sel, 1.0, 0.0).astype(BF16)
        row_id = lax.broadcasted_iota(jnp.int32, (MOE_LOC_ROWS, MOE_SUB), 0).astype(F32)
        for u in range(nsub):
            rhi, rlo = _split_bf16(route_ref[sub_rows(u), :])
            pos_t = (lax.dot_general(sel, rhi, _NT, preferred_element_type=F32)
                     + lax.dot_general(sel, rlo, _NT, preferred_element_type=F32))
            onehot = jnp.where(row_id == pos_t[0:1], 1.0, 0.0) + jnp.where(row_id == pos_t[1:2], 1.0, 0.0)
            loc[u] = _dot(onehot.astype(BF16), xn_ref[sub_rows(u), :]).astype(BF16)

    def move_granules(to_stage):
        dst = 0
        for u in range(nsub):
            off, n = meta(u, 0), meta(u, 1)

            def body(g, carry, u=u, off=off, dst=dst):
                if to_stage:
                    stage[gran_rows(dst + g), :] = loc[u, gran_rows(off + g), :]
                else:
                    loc[u, gran_rows(off + g), :] = stage[gran_rows(dst + g), :]
                return carry

            lax.fori_loop(0, n, body, 0)
            dst = dst + n
        return dst

    total = move_granules(True)

    def block(b, carry):
        rows = pl.ds(pl.multiple_of(b * MOE_ROW_BLOCK, MOE_ROW_BLOCK), MOE_ROW_BLOCK)
        xs = stage[rows, :]
        hg = _dot(xs, wg_ref[0])
        h = (hg * _sigmoid(hg) * _dot(xs, wu_ref[0])).astype(BF16)
        stage[rows, :] = _dot(h, wd_ref[0]).astype(BF16)
        return carry

    lax.fori_loop(0, (total * MOE_GRAN + MOE_ROW_BLOCK - 1) // MOE_ROW_BLOCK, block, 0)
    move_granules(False)

    @pl.when(e == N_EXPERTS - 1)
    def _combine():
        col_id = lax.broadcasted_iota(jnp.int32, (MOE_SUB, MOE_LOC_ROWS), 1).astype(F32)
        for u in range(nsub):
            r = route_ref[sub_rows(u), :]
            pick = jnp.where(col_id == r[:, 0:1], r[:, 2:3], jnp.where(col_id == r[:, 1:2], r[:, 3:4], 0.0))
            o_ref[sub_rows(u), :] = x1_ref[sub_rows(u), :] + _dot(pick.astype(BF16), loc[u])


def _moe(meta, xn, route, x1, l, wg, wu, wd, tb):
    t = xn.shape[0]
    nsub = tb // MOE_SUB
    row = lambda w, **kw: pl.BlockSpec((tb, w), lambda i, e, m: (i, 0), **kw)
    wspec = lambda a, b: pl.BlockSpec((None, 1, a, b), lambda i, e, m: (l, e, 0, 0))
    return pl.pallas_call(
        functools.partial(_moe_kernel, nsub=nsub),
        grid_spec=pltpu.PrefetchScalarGridSpec(
            num_scalar_prefetch=1,
            grid=(t // tb, N_EXPERTS),
            in_specs=[row(D_MODEL), row(LANES), row(D_MODEL, pipeline_mode=pl.Buffered(1)),
                      wspec(D_MODEL, D_EXPERT), wspec(D_MODEL, D_EXPERT), wspec(D_EXPERT, D_MODEL)],
            out_specs=row(D_MODEL, pipeline_mode=pl.Buffered(1)),
            scratch_shapes=[pltpu.VMEM((nsub, MOE_LOC_ROWS, D_MODEL), BF16),
                            pltpu.VMEM((pl.cdiv(tb, MOE_ROW_BLOCK) * MOE_ROW_BLOCK, D_MODEL), BF16)]),
        out_shape=jax.ShapeDtypeStruct((t, D_MODEL), F32),
        compiler_params=pltpu.CompilerParams(dimension_semantics=("parallel", "arbitrary"),
                                             vmem_limit_bytes=VMEM_LIMIT),
        name="moe",
    )(meta, xn, route, x1, wg, wu, wd)


def _prep_params(norm_mix, w_in, q_norm_a, k_norm_a, rpb, w_decay, b_decay, gla_norm, w_o_a, w_o_b, w_out,
                 norm_ffn, w_router_g, b_router_g, w_router_e, b_router_e, w_gate, w_up, w_down):
    depth = w_in.shape[0]
    f32 = lambda a: a.astype(F32)
    pad_lanes = lambda a, before, total: jnp.pad(f32(a), [(0, 0)] * (a.ndim - 1) + [(before, total - before - a.shape[-1])])
    w_aligned = jnp.concatenate([w_in[:, :, :IN_Z_COL0 + 2 * GLA_RANK],
                                 jnp.zeros((depth, D_MODEL, IN_GATE_COL0 - IN_Z_COL0 - 2 * GLA_RANK), w_in.dtype),
                                 w_in[:, :, IN_Z_COL0 + 2 * GLA_RANK:]], axis=-1).astype(BF16)
    w_router = pad_lanes(jnp.concatenate([w_router_g, w_router_e], axis=-1), ROUTER_GROUP_LANE0, LANES)
    b_router = pad_lanes(jnp.concatenate([b_router_g, b_router_e], axis=-1), ROUTER_GROUP_LANE0, LANES)
    wrh, wrl = _split_bf16(w_router)
    wdec = jnp.stack([jnp.pad(f32(w_decay[:, d]), ((0, 0), (d * GLA_RANK, LANES - (d + 1) * GLA_RANK), (0, 0)))
                      for d in range(2)], axis=1).astype(BF16)
    return dict(
        in_proj=(f32(norm_mix)[:, None, :], w_aligned,
                 jnp.tile(f32(q_norm_a), (1, NA_HEADS))[:, None, :] * (NA_HEAD_DIM ** -0.5 * LOG2E),
                 jnp.tile(f32(k_norm_a), (1, NA_HEADS))[:, None, :], wdec, f32(b_decay)[:, :, None, :]),
        na=(_na_bias_table(rpb),),
        post=(w_o_a.astype(BF16), w_o_b.astype(BF16), w_out.astype(BF16), f32(gla_norm)[:, None, :],
              f32(norm_ffn)[:, None, :], wrh, wrl, b_router[:, None, :]),
        moe=(w_gate.astype(BF16), w_up.astype(BF16), w_down.astype(BF16)),
    )


def _tile(t, want):
    while t % want:
        want //= 2
    return want


def _trunk(x, p, depth):
    batch, n, _ = x.shape
    t = batch * n
    x = x.reshape(t, D_MODEL)
    for l in range(depth):
        qa, ka, va, qb, kb, vb, rb, gf, gr, ga, gb = _in_proj(x, l, *p["in_proj"], _tile(t, 512))
        ya = _na(qa, ka, va, l, *p["na"], batch, n)
        of, ob = _gla(qb, kb, vb, gf, gr, batch, n, GLA_BLOCK_TOK)
        x1, xn, route, meta = _post(x, ya, of, ob, rb, ga, gb, l, *p["post"], _tile(t, 1024))
        meta = meta[:, 0, :2 * N_EXPERTS].reshape(-1)
        x = _moe(meta, xn, route, x1, l, *p["moe"], _tile(t, MOE_TILE))
    return x.reshape(batch, n, D_MODEL)


def kernel(x_prompt, x_sample, norm_mix, w_in, q_norm_a, k_norm_a, rpb, w_decay, b_decay, gla_norm, w_o_a, w_o_b,
           w_out, norm_ffn, w_router_g, b_router_g, w_router_e, b_router_e, w_gate, w_up, w_down):
    p = _prep_params(norm_mix, w_in, q_norm_a, k_norm_a, rpb, w_decay, b_decay, gla_norm, w_o_a, w_o_b, w_out,
                     norm_ffn, w_router_g, b_router_g, w_router_e, b_router_e, w_gate, w_up, w_down)
    depth = w_in.shape[0]
    return (_trunk(x_prompt, p, depth), _trunk(x_sample, p, depth))
```

```python
import functools

import jax
import jax.numpy as jnp
import numpy as np
from jax import lax
from jax.experimental import pallas as pl
from jax.experimental.pallas import tpu as pltpu

F32 = jnp.float32
BF16 = jnp.bfloat16

D_MODEL = 1024
GRID_W = 64
NA_HEADS = 8
NA_HEAD_DIM = 64
NA_WIDTH = NA_HEADS * NA_HEAD_DIM
NA_ROWS = 8
NA_COLS = 16
RPB_ROWS = 2 * NA_ROWS - 1
RPB_COLS = 2 * NA_COLS - 1
GLA_HEADS = 4
GLA_DK = 128
GLA_DV = 128
GLA_WIDTH = GLA_HEADS * GLA_DV
GLA_RANK = 16
GLA_GATE_NORM = 16.0
GLA_CHUNK = 64
GLA_BLOCK_TOK = 512
N_GROUPS = 4
EXPERTS_PER_GROUP = 4
N_EXPERTS = N_GROUPS * EXPERTS_PER_GROUP
D_EXPERT = 512
EPS = 1e-6
NEG = -1e30

LANES = 128
IN_Z_COL0 = 7 * 512
IN_GATE_COL0 = IN_Z_COL0 + LANES
MOE_SUB = 256
MOE_GRAN = 16
MOE_LOC_ROWS = 768
MOE_TILE = 2048
MOE_EXPERTS_PER_STEP = 2
MOE_ROW_BLOCK = 384
LOG2E = 1.4426950408889634
EXP_CLAMP = 80.0
ROUTER_GROUP_LANE0 = 0
ROUTER_EXPERT_LANE0 = N_GROUPS
VMEM_LIMIT = 56 * 1024 * 1024

_NT = (((1,), (1,)), ((), ()))
_TN = (((0,), (0,)), ((), ()))


def _dot(a, b):
    return jnp.dot(a, b, preferred_element_type=F32)


def _split_bf16(x):
    hi = x.astype(BF16)
    lo = (x - hi.astype(F32)).astype(BF16)
    return hi, lo


def _sigmoid(x):
    return 1.0 / (1.0 + jnp.exp(-x))


def _const_spec(stacked, l):
    tail = stacked.shape[1:]
    return pl.BlockSpec((None,) + tail, lambda *_: (l,) + (0,) * len(tail), pipeline_mode=pl.Buffered(1))


def _in_proj_kernel(x_ref, nrm_ref, w_ref, qg_ref, kg_ref, wdec_ref, bdec_ref,
                    qa_ref, ka_ref, va_ref, qb_ref, kb_ref, vb_ref, rb_ref, gf_ref, gr_ref, ga_ref, gb_ref):
    x = x_ref[...]
    ms = jnp.mean(x * x, axis=-1, keepdims=True)
    xn = (x * lax.rsqrt(ms + EPS) * nrm_ref[...]).astype(BF16)
    low_half = lax.broadcasted_iota(jnp.int32, (x.shape[0], LANES), 1) < NA_HEAD_DIM

    def proj(lo, width=512):
        return _dot(xn, w_ref[:, lo:lo + width])

    def head_norm(a, gain_ref):
        parts = []
        for p in range(NA_WIDTH // LANES):
            sq = a[:, p * LANES:(p + 1) * LANES]
            sq = sq * sq
            s_lo = jnp.sum(jnp.where(low_half, sq, 0.0), axis=-1, keepdims=True)
            s_hi = jnp.sum(jnp.where(low_half, 0.0, sq), axis=-1, keepdims=True)
            parts.append(jnp.where(low_half, s_lo, s_hi))
        ss = jnp.concatenate(parts, axis=-1)
        return a * lax.rsqrt(ss * (1.0 / NA_HEAD_DIM) + EPS) * gain_ref[...]

    z = proj(IN_Z_COL0, LANES).astype(BF16)
    ga_ref[...] = _sigmoid(proj(IN_GATE_COL0, D_MODEL)).astype(BF16)
    gb_ref[...] = _sigmoid(proj(IN_GATE_COL0 + D_MODEL, D_MODEL)).astype(BF16)
    for d, g_ref in enumerate((gf_ref, gr_ref)):
        lg = _dot(z, wdec_ref[d]) + bdec_ref[d]
        g_ref[...] = (jnp.minimum(lg, 0.0) - jnp.log(1.0 + jnp.exp(-jnp.abs(lg)))) * (1.0 / GLA_GATE_NORM)
    r = proj(3072)
    rb_ref[...] = (r * _sigmoid(r)).astype(BF16)
    qa = proj(0)
    ka = proj(512)
    qb_ref[...] = (proj(1536) * (GLA_DK ** -0.5)).astype(BF16)
    qa_ref[...] = head_norm(qa, qg_ref).astype(BF16)
    kb_ref[...] = proj(2048).astype(BF16)
    ka_ref[...] = head_norm(ka, kg_ref).astype(BF16)
    va_ref[...] = proj(1024).astype(BF16)
    vb_ref[...] = proj(2560).astype(BF16)


def _in_proj(x, l, nrm, w, qg, kg, wdec, bdec, tm):
    t = x.shape[0]
    row = lambda w: pl.BlockSpec((tm, w), lambda i: (i, 0))
    out_w = [512] * 7 + [GLA_WIDTH, GLA_WIDTH, D_MODEL, D_MODEL]
    out_dt = [BF16] * 7 + [F32, F32, BF16, BF16]
    consts = [nrm, w, qg, kg, wdec, bdec]
    return pl.pallas_call(
        _in_proj_kernel,
        grid=(t // tm,),
        in_specs=[row(D_MODEL)] + [_const_spec(c, l) for c in consts],
        out_specs=[row(w) for w in out_w],
        out_shape=[jax.ShapeDtypeStruct((t, w), dt) for w, dt in zip(out_w, out_dt)],
        compiler_params=pltpu.CompilerParams(dimension_semantics=("parallel",), vmem_limit_bytes=VMEM_LIMIT),
        name="in_proj",
    )(x, *consts)


NA_BLOCK_ROWS = 8
NA_BLOCK_TOK = NA_BLOCK_ROWS * GRID_W
NA_WINDOW_TOK = NA_ROWS * GRID_W


def _na_kernel(q_ref, kp_ref, kc_ref, kn_ref, vp_ref, vc_ref, vn_ref, bias_ref, o_ref, kwin, vwin, s_scr, *, rows):
    i = pl.program_id(1)
    kwin[0:NA_BLOCK_TOK] = kp_ref[...]
    kwin[NA_BLOCK_TOK:2 * NA_BLOCK_TOK] = kc_ref[...]
    kwin[2 * NA_BLOCK_TOK:3 * NA_BLOCK_TOK] = kn_ref[...]
    vwin[0:NA_BLOCK_TOK] = vp_ref[...]
    vwin[NA_BLOCK_TOK:2 * NA_BLOCK_TOK] = vc_ref[...]
    vwin[2 * NA_BLOCK_TOK:3 * NA_BLOCK_TOK] = vn_ref[...]
    low_half = lax.broadcasted_iota(jnp.int32, (GRID_W, LANES), 1) < NA_HEAD_DIM
    pairs = range(NA_HEADS // 2)
    cols = lambda p: slice(p * LANES, (p + 1) * LANES)
    qrows = lambda j: slice(j * GRID_W, (j + 1) * GRID_W)

    def window(j):
        r = i * NA_BLOCK_ROWS + j
        rs = jnp.clip(r - NA_ROWS // 2, 0, rows - NA_ROWS)
        start = pl.multiple_of((rs - i * NA_BLOCK_ROWS + NA_BLOCK_ROWS) * GRID_W, GRID_W)
        return start, rs - r + NA_ROWS - 1

    def scores(j):
        start, bias_off = window(j)
        for p in pairs:
            qp = q_ref[qrows(j), cols(p)]
            zero = jnp.zeros_like(qp)
            q2 = jnp.concatenate([jnp.where(low_half, qp, zero), jnp.where(low_half, zero, qp)], axis=0)
            s = lax.dot_general(q2, kwin[pl.ds(start, NA_WINDOW_TOK), cols(p)], _NT, preferred_element_type=F32)
            s_scr[j % 2, p] = s + bias_ref[bias_off, p]

    def attend(j):
        start, _ = window(j)
        for p in pairs:
            s = s_scr[j % 2, p]
            e = jnp.exp2(s - jnp.max(s, axis=-1, keepdims=True))
            l = jnp.sum(e, axis=-1, keepdims=True)
            pv = _dot(e.astype(BF16), vwin[pl.ds(start, NA_WINDOW_TOK), cols(p)]) * (1.0 / l)
            o_ref[qrows(j), cols(p)] = jnp.where(low_half, pv[:GRID_W], pv[GRID_W:]).astype(BF16)

    scores(0)
    for j in range(NA_BLOCK_ROWS):
        if j + 1 < NA_BLOCK_ROWS:
            scores(j + 1)
        attend(j)


def _na(q, k, v, l, bias, batch, n):
    rows = n // GRID_W
    nblk = rows // NA_BLOCK_ROWS
    spec = lambda f: pl.BlockSpec((NA_BLOCK_TOK, NA_WIDTH), lambda b, i: (b * nblk + f(i), 0))
    cur = spec(lambda i: i)
    prev = spec(lambda i: jnp.maximum(i - 1, 0))
    nxt = spec(lambda i: jnp.minimum(i + 1, nblk - 1))
    return pl.pallas_call(
        functools.partial(_na_kernel, rows=rows),
        grid=(batch, nblk),
        in_specs=[cur, prev, cur, nxt, prev, cur, nxt, _const_spec(bias, l)],
        out_specs=cur,
        out_shape=jax.ShapeDtypeStruct((batch * n, NA_WIDTH), BF16),
        scratch_shapes=[pltpu.VMEM((3 * NA_BLOCK_TOK, NA_WIDTH), BF16),
                        pltpu.VMEM((3 * NA_BLOCK_TOK, NA_WIDTH), BF16),
                        pltpu.VMEM((2, NA_HEADS // 2, 2 * GRID_W, NA_WINDOW_TOK), F32)],
        compiler_params=pltpu.CompilerParams(dimension_semantics=("parallel", "parallel"),
                                             vmem_limit_bytes=VMEM_LIMIT),
        name="na",
    )(q, k, k, k, v, v, v, bias)


def _na_bias_table(rpb):
    cols = np.arange(GRID_W)
    col_start = np.clip(cols - NA_COLS // 2, 0, GRID_W - NA_COLS)
    col_mask = (cols[None, :] >= col_start[:, None]) & (cols[None, :] < col_start[:, None] + NA_COLS)
    col_idx = np.clip(cols[None, :] - cols[:, None] + NA_COLS - 1, 0, RPB_COLS - 1)
    onehot = (col_idx[None] == np.arange(RPB_COLS)[:, None, None]).astype(np.float32)
    bias = jnp.einsum("lhrk,kqc->lhqrc", rpb.astype(F32) * LOG2E, onehot, precision=lax.Precision.HIGHEST)
    masked = jnp.where(col_mask[:, None, :], bias, NEG)
    depth = rpb.shape[0]
    slabs = [masked[:, :, :, o:o + NA_ROWS].reshape(depth, NA_HEADS // 2, 2 * GRID_W, NA_WINDOW_TOK)
             for o in range(NA_ROWS)]
    return jnp.stack(slabs, axis=1)


def _gla_chunk_rows(s):
    return slice(s * GLA_CHUNK, (s + 1) * GLA_CHUNK)


def _gla_head_cols(h):
    return slice(h * GLA_DK, (h + 1) * GLA_DK)


def _gla_cumulate(d, g, nsub):
    tri = jnp.where(d["mask"], 1.0, 0.0).astype(BF16)
    ghi, glo = _split_bf16(g)
    return [_dot(tri, ghi[_gla_chunk_rows(s)]) + _dot(tri, glo[_gla_chunk_rows(s)]) for s in range(nsub)]


def _gla_rescale(d, b_chunks):
    mid, end = d["mid"], d["end"]
    scaled = []
    for s, b in enumerate(b_chunks):
        rows = _gla_chunk_rows(s)
        b_mid = b[mid:mid + 1, :]
        b_end = b[end:end + 1, :]
        qe = d["q"][rows, :] * jnp.exp(jnp.minimum(b - b_mid, EXP_CLAMP)).astype(BF16)
        ke = d["k"][rows, :] * jnp.exp(jnp.minimum(b_mid - b, EXP_CLAMP)).astype(BF16)
        d["qi"][rows, :] = qe * jnp.exp(b_mid).astype(BF16)
        d["dec"][s] = jnp.broadcast_to(jnp.exp(b_end), (8, GLA_WIDTH))
        k_out = ke * jnp.exp(b_end - b_mid).astype(BF16)
        scaled.append((qe, ke, k_out))
    return scaled


def _gla_products(d, scaled):
    pairs = [(s, h) for s in range(len(scaled)) for h in range(GLA_HEADS)]
    v = lambda s, h: d["v"][_gla_chunk_rows(s), _gla_head_cols(h)]
    a = [lax.dot_general(scaled[s][0][:, _gla_head_cols(h)], scaled[s][1][:, _gla_head_cols(h)], _NT,
                         preferred_element_type=F32) for s, h in pairs]
    for s, h in pairs:
        d["kv"][s, h] = lax.dot_general(v(s, h), scaled[s][2][:, _gla_head_cols(h)], _TN,
                                        preferred_element_type=F32)
    for (s, h), a_sh in zip(pairs, a):
        d["oi"][_gla_chunk_rows(s), _gla_head_cols(h)] = _dot(jnp.where(d["mask"], a_sh, 0.0).astype(BF16), v(s, h))


def _gla_scan_step(d, s):
    rows = _gla_chunk_rows(s)
    dec = d["dec"][s]
    states = [d["st"][h] for h in range(GLA_HEADS)]
    inter = [lax.dot_general(d["qi"][rows, _gla_head_cols(h)], states[h].astype(BF16), _NT,
                             preferred_element_type=F32) for h in range(GLA_HEADS)]
    for h in range(GLA_HEADS):
        cols = _gla_head_cols(h)
        d["st"][h] = states[h] * dec[0:1, cols] + d["kv"][s, h]
        d["o"][rows, cols] = (d["oi"][rows, cols] + inter[h]).astype(BF16)


def _gla_kernel(qf_ref, kf_ref, vf_ref, gf_ref, qr_ref, kr_ref, vr_ref, gr_ref,
                of_ref, ob_ref, stf_ref, stb_ref, oif_s, oib_s, qif_s, qib_s, kvf_s, kvb_s, decf_s, decb_s, *, nsub):
    @pl.when(pl.program_id(1) == 0)
    def _():
        stf_ref[...] = jnp.zeros_like(stf_ref)
        stb_ref[...] = jnp.zeros_like(stb_ref)

    c = GLA_CHUNK
    ri = lax.broadcasted_iota(jnp.int32, (c, c), 0)
    ci = lax.broadcasted_iota(jnp.int32, (c, c), 1)
    fwd = dict(q=qf_ref, k=kf_ref, v=vf_ref, g=gf_ref, o=of_ref, st=stf_ref, oi=oif_s,
               qi=qif_s, kv=kvf_s, dec=decf_s, mask=ci <= ri, mid=c // 2 - 1, end=c - 1)
    bwd = dict(q=qr_ref, k=kr_ref, v=vr_ref, g=gr_ref, o=ob_ref, st=stb_ref, oi=oib_s,
               qi=qib_s, kv=kvb_s, dec=decb_s, mask=ci >= ri, mid=c // 2, end=0)
    b = [_gla_cumulate(d, d["g"][...], nsub) for d in (fwd, bwd)]
    scaled = [_gla_rescale(d, b_d) for d, b_d in zip((fwd, bwd), b)]
    for d, scaled_d in zip((fwd, bwd), scaled):
        _gla_products(d, scaled_d)
    for s in range(nsub):
        _gla_scan_step(fwd, s)
        _gla_scan_step(bwd, nsub - 1 - s)


def _gla(q, k, v, gf, gr, batch, n, cb):
    nb = n // cb
    nsub = cb // GLA_CHUNK
    fwd = lambda w: pl.BlockSpec((cb, w), lambda b, c: (b * nb + c, 0))
    bwd = lambda w: pl.BlockSpec((cb, w), lambda b, c: (b * nb + nb - 1 - c, 0))
    out = jax.ShapeDtypeStruct((batch * n, GLA_WIDTH), BF16)
    state = pltpu.VMEM((GLA_HEADS, GLA_DV, GLA_DK), F32)
    return pl.pallas_call(
        functools.partial(_gla_kernel, nsub=nsub),
        grid=(batch, nb),
        in_specs=[fwd(GLA_WIDTH)] * 4 + [bwd(GLA_WIDTH)] * 4,
        out_specs=[fwd(GLA_WIDTH), bwd(GLA_WIDTH)],
        out_shape=[out, out],
        scratch_shapes=[state, state,
                        pltpu.VMEM((cb, GLA_WIDTH), F32), pltpu.VMEM((cb, GLA_WIDTH), F32),
                        pltpu.VMEM((cb, GLA_WIDTH), BF16), pltpu.VMEM((cb, GLA_WIDTH), BF16),
                        pltpu.VMEM((nsub, GLA_HEADS, GLA_DV, GLA_DK), F32),
                        pltpu.VMEM((nsub, GLA_HEADS, GLA_DV, GLA_DK), F32),
                        pltpu.VMEM((nsub, 8, GLA_WIDTH), F32), pltpu.VMEM((nsub, 8, GLA_WIDTH), F32)],
        compiler_params=pltpu.CompilerParams(dimension_semantics=("parallel", "arbitrary"),
                                             vmem_limit_bytes=VMEM_LIMIT),
        name="gla",
    )(q, k, v, gf, q, k, v, gr)


def _route(logits):
    lane = lax.broadcasted_iota(jnp.int32, logits.shape, 1)
    lane_f = lane.astype(F32)
    ninf = -jnp.inf

    def first_argmax(vals, vmax):
        return jnp.min(jnp.where(vals == vmax, lane_f, float(LANES)), axis=-1, keepdims=True)

    lg = jnp.where(lane < N_GROUPS, logits, ninf)
    g_max = jnp.max(lg, axis=-1, keepdims=True)
    p_sel = 1.0 / jnp.sum(jnp.exp(lg - g_max), axis=-1, keepdims=True)
    grp = first_argmax(lg, g_max)
    e_lo = ROUTER_EXPERT_LANE0 + EXPERTS_PER_GROUP * grp
    le = jnp.where((lane_f >= e_lo) & (lane_f < e_lo + EXPERTS_PER_GROUP), logits, ninf)
    l1 = jnp.max(le, axis=-1, keepdims=True)
    i1 = first_argmax(le, l1)
    le2 = jnp.where(lane_f == i1, ninf, le)
    l2 = jnp.max(le2, axis=-1, keepdims=True)
    i2 = first_argmax(le2, l2)
    t = jnp.exp(l2 - l1)
    w1 = p_sel / (1.0 + t)
    w2 = w1 * t
    return lane_f == i1, lane_f == i2, w1, w2


def _local_sort(sel1, sel2, w1, w2):
    n = sel1.shape[0]
    ri = lax.broadcasted_iota(jnp.int32, (n, n), 0)
    ci = lax.broadcasted_iota(jnp.int32, (n, n), 1)
    before = jnp.where(ci < ri, 1.0, 0.0).astype(BF16)
    li = lax.broadcasted_iota(jnp.int32, (LANES, LANES), 0)
    lj = lax.broadcasted_iota(jnp.int32, (LANES, LANES), 1)
    lower_lanes = jnp.where(li < lj, 1.0, 0.0).astype(BF16)
    oh1 = jnp.where(sel1, 1.0, 0.0)
    oh2 = jnp.where(sel2, 1.0, 0.0)
    cnt1 = jnp.sum(oh1, axis=0, keepdims=True)
    cnt = cnt1 + jnp.sum(oh2, axis=0, keepdims=True)
    gran = jnp.floor((cnt + (MOE_GRAN - 1)) * (1.0 / MOE_GRAN))
    offg = _dot(jnp.broadcast_to(gran, (8, LANES)).astype(BF16), lower_lanes)[0:1]
    pick = lambda sel, vals: jnp.sum(jnp.where(sel, vals, 0.0), axis=-1, keepdims=True)
    pos1 = pick(sel1, _dot(before, oh1.astype(BF16)) + MOE_GRAN * offg)
    pos2 = pick(sel2, _dot(before, oh2.astype(BF16)) + cnt1 + MOE_GRAN * offg)
    lane = lax.broadcasted_iota(jnp.int32, (n, LANES), 1)
    route = jnp.where(lane == 0, pos1, jnp.where(lane == 1, pos2, jnp.where(lane == 2, w1,
                                                                             jnp.where(lane == 3, w2, 0.0))))
    to_off = jnp.where((li == lj + ROUTER_EXPERT_LANE0) & (lj < N_EXPERTS), 1.0, 0.0).astype(BF16)
    to_len = jnp.where(li == lj + ROUTER_EXPERT_LANE0 - N_EXPERTS, 1.0, 0.0).astype(BF16)
    meta = (_dot(jnp.broadcast_to(offg, (8, LANES)).astype(BF16), to_off)
            + _dot(jnp.broadcast_to(gran, (8, LANES)).astype(BF16), to_len))
    return route, meta.astype(jnp.int32)


def _post_kernel(x_ref, ya_ref, of_ref, ob_ref, rb_ref, ga_ref, gb_ref, woa_ref, wob_ref, wout_ref,
                 gn_ref, nf_ref, wr_ref, br_ref, x1_ref, xn_ref, route_ref, meta_ref):
    nsub = x_ref.shape[0] // MOE_SUB
    subs = [slice(u * MOE_SUB, (u + 1) * MOE_SUB) for u in range(nsub)]
    val = [dict() for _ in range(nsub)]

    def branches(u):
        rows = subs[u]
        o = of_ref[rows, :].astype(F32) + ob_ref[rows, :].astype(F32)
        parts = []
        for h in range(GLA_HEADS):
            oh = o[:, h * GLA_DV:(h + 1) * GLA_DV]
            ms = jnp.mean(oh * oh, axis=-1, keepdims=True)
            parts.append(oh * lax.rsqrt(ms + EPS) * gn_ref[...])
        on = (jnp.concatenate(parts, axis=-1) * rb_ref[rows, :].astype(F32)).astype(BF16)
        val[u]["ya"] = _dot(ya_ref[rows, :], woa_ref[...])
        val[u]["yb"] = _dot(on, wob_ref[...])

    def merge(u):
        rows = subs[u]
        y = ga_ref[rows, :] * val[u]["ya"].astype(BF16) + gb_ref[rows, :] * val[u]["yb"].astype(BF16)
        val[u]["x1"] = x_ref[rows, :] + _dot(y, wout_ref[...])

    def norm_logits(u):
        rows = subs[u]
        x1 = val[u]["x1"]
        x1_ref[rows, :] = x1
        ms = jnp.mean(x1 * x1, axis=-1, keepdims=True)
        thi, tlo = _split_bf16(x1 * lax.rsqrt(ms + EPS) * nf_ref[...])
        xn_ref[rows, :] = thi
        hi = _dot(thi, wr_ref[...])
        val[u]["logits"] = hi[:, :LANES] + hi[:, LANES:] + _dot(tlo, wr_ref[:, :LANES]) + br_ref[...]

    def route(u):
        route_ref[subs[u], :], meta_ref[u] = _local_sort(*_route(val[u]["logits"]))

    stages = [branches, merge, norm_logits, route]
    for step in range(nsub + len(stages) - 1):
        for u in range(nsub):
            if 0 <= step - u < len(stages):
                stages[step - u](u)


def _post(x, ya, of, ob, rb, ga, gb, l, woa, wob, wout, gn, nf, wr, br, tm):
    t = x.shape[0]
    row = lambda w: pl.BlockSpec((tm, w), lambda i: (i, 0))
    consts = [woa, wob, wout, gn, nf, wr, br]
    return pl.pallas_call(
        _post_kernel,
        grid=(t // tm,),
        in_specs=[row(D_MODEL), row(NA_WIDTH), row(GLA_WIDTH), row(GLA_WIDTH), row(GLA_WIDTH),
                  row(D_MODEL), row(D_MODEL)] + [_const_spec(c, l) for c in consts],
        out_specs=[row(D_MODEL), row(D_MODEL), row(LANES),
                   pl.BlockSpec((tm // MOE_SUB, 8, LANES), lambda i: (i, 0, 0))],
        out_shape=[jax.ShapeDtypeStruct((t, D_MODEL), F32), jax.ShapeDtypeStruct((t, D_MODEL), BF16),
                   jax.ShapeDtypeStruct((t, LANES), F32),
                   jax.ShapeDtypeStruct((t // MOE_SUB, 8, LANES), jnp.int32)],
        compiler_params=pltpu.CompilerParams(dimension_semantics=("parallel",), vmem_limit_bytes=VMEM_LIMIT),
        name="post",
    )(x, ya, of, ob, rb, ga, gb, *consts)


def _moe_kernel(meta_ref, xn_ref, route_ref, x1_ref, wg_ref, wu_ref, wd_ref, o_ref, loc, stage, *, nsub):
    i = pl.program_id(0)
    step = pl.program_id(1)
    sub_rows = lambda u: slice(u * MOE_SUB, (u + 1) * MOE_SUB)
    gran_rows = lambda g: pl.ds(pl.multiple_of(g * MOE_GRAN, MOE_GRAN), MOE_GRAN)

    @pl.when(step == 0)
    def _dispatch():
        stage[...] = jnp.zeros_like(stage)
        sel = lax.broadcasted_iota(jnp.int32, (8, LANES), 0) == lax.broadcasted_iota(jnp.int32, (8, LANES), 1)
        sel = jnp.where(sel, 1.0, 0.0).astype(BF16)
        row_id = lax.broadcasted_iota(jnp.int32, (MOE_LOC_ROWS, MOE_SUB), 0).astype(F32)
        for u in range(nsub):
            rhi, rlo = _split_bf16(route_ref[sub_rows(u), :])
            pos_t = (lax.dot_general(sel, rhi, _NT, preferred_element_type=F32)
                     + lax.dot_general(sel, rlo, _NT, preferred_element_type=F32))
            onehot = jnp.where(row_id == pos_t[0:1], 1.0, 0.0) + jnp.where(row_id == pos_t[1:2], 1.0, 0.0)
            loc[u] = _dot(onehot.astype(BF16), xn_ref[sub_rows(u), :]).astype(BF16)

    def move_granules(e, to_stage):
        meta = lambda u, k: meta_ref[((i * nsub + u) * 2 + k) * N_EXPERTS + e]
        dst = 0
        for u in range(nsub):
            off, n = meta(u, 0), meta(u, 1)

            def body(g, carry, u=u, off=off, dst=dst):
                if to_stage:
                    stage[gran_rows(dst + g), :] = loc[u, gran_rows(off + g), :]
                else:
                    loc[u, gran_rows(off + g), :] = stage[gran_rows(dst + g), :]
                return carry

            lax.fori_loop(0, n, body, 0)
            dst = dst + n
        return dst

    for k in range(MOE_EXPERTS_PER_STEP):
        e = step * MOE_EXPERTS_PER_STEP + k
        total = move_granules(e, True)

        def block(b, carry, k=k):
            rows = pl.ds(pl.multiple_of(b * MOE_ROW_BLOCK, MOE_ROW_BLOCK), MOE_ROW_BLOCK)
            xs = stage[rows, :]
            hg = _dot(xs, wg_ref[k])
            h = (hg * _sigmoid(hg) * _dot(xs, wu_ref[k])).astype(BF16)
            stage[rows, :] = _dot(h, wd_ref[k]).astype(BF16)
            return carry

        lax.fori_loop(0, (total * MOE_GRAN + MOE_ROW_BLOCK - 1) // MOE_ROW_BLOCK, block, 0)
        move_granules(e, False)

    @pl.when(step == N_EXPERTS // MOE_EXPERTS_PER_STEP - 1)
    def _combine():
        col_id = lax.broadcasted_iota(jnp.int32, (MOE_SUB, MOE_LOC_ROWS), 1).astype(F32)
        for u in range(nsub):
            r = route_ref[sub_rows(u), :]
            pick = jnp.where(col_id == r[:, 0:1], r[:, 2:3], jnp.where(col_id == r[:, 1:2], r[:, 3:4], 0.0))
            o_ref[sub_rows(u), :] = x1_ref[sub_rows(u), :] + _dot(pick.astype(BF16), loc[u])


def _moe(meta, xn, route, x1, l, wg, wu, wd, tb):
    t = xn.shape[0]
    nsub = tb // MOE_SUB
    row = lambda w, **kw: pl.BlockSpec((tb, w), lambda i, e, m: (i, 0), **kw)
    wspec = lambda a, b: pl.BlockSpec((None, MOE_EXPERTS_PER_STEP, a, b), lambda i, s, m: (l, s, 0, 0))
    return pl.pallas_call(
        functools.partial(_moe_kernel, nsub=nsub),
        grid_spec=pltpu.PrefetchScalarGridSpec(
            num_scalar_prefetch=1,
            grid=(t // tb, N_EXPERTS // MOE_EXPERTS_PER_STEP),
            in_specs=[row(D_MODEL), row(LANES), row(D_MODEL, pipeline_mode=pl.Buffered(1)),
                      wspec(D_MODEL, D_EXPERT), wspec(D_MODEL, D_EXPERT), wspec(D_EXPERT, D_MODEL)],
            out_specs=row(D_MODEL, pipeline_mode=pl.Buffered(1)),
            scratch_shapes=[pltpu.VMEM((nsub, MOE_LOC_ROWS, D_MODEL), BF16),
                            pltpu.VMEM((pl.cdiv(tb, MOE_ROW_BLOCK) * MOE_ROW_BLOCK, D_MODEL), BF16)]),
        out_shape=jax.ShapeDtypeStruct((t, D_MODEL), F32),
        compiler_params=pltpu.CompilerParams(dimension_semantics=("parallel", "arbitrary"),
                                             vmem_limit_bytes=VMEM_LIMIT),
        name="moe",
    )(meta, xn, route, x1, wg, wu, wd)


def _prep_params(norm_mix, w_in, q_norm_a, k_norm_a, rpb, w_decay, b_decay, gla_norm, w_o_a, w_o_b, w_out,
                 norm_ffn, w_router_g, b_router_g, w_router_e, b_router_e, w_gate, w_up, w_down):
    depth = w_in.shape[0]
    f32 = lambda a: a.astype(F32)
    pad_lanes = lambda a, before, total: jnp.pad(f32(a), [(0, 0)] * (a.ndim - 1) + [(before, total - before - a.shape[-1])])
    w_aligned = jnp.concatenate([w_in[:, :, :IN_Z_COL0 + 2 * GLA_RANK],
                                 jnp.zeros((depth, D_MODEL, IN_GATE_COL0 - IN_Z_COL0 - 2 * GLA_RANK), w_in.dtype),
                                 w_in[:, :, IN_Z_COL0 + 2 * GLA_RANK:]], axis=-1).astype(BF16)
    w_router = pad_lanes(jnp.concatenate([w_router_g, w_router_e], axis=-1), ROUTER_GROUP_LANE0, LANES)
    b_router = pad_lanes(jnp.concatenate([b_router_g, b_router_e], axis=-1), ROUTER_GROUP_LANE0, LANES)
    w_router = jnp.concatenate(_split_bf16(w_router), axis=-1)
    wdec = jnp.stack([jnp.pad(f32(w_decay[:, d]), ((0, 0), (d * GLA_RANK, LANES - (d + 1) * GLA_RANK), (0, 0)))
                      for d in range(2)], axis=1).astype(BF16)
    return dict(
        in_proj=(f32(norm_mix)[:, None, :], w_aligned,
                 jnp.tile(f32(q_norm_a), (1, NA_HEADS))[:, None, :] * (NA_HEAD_DIM ** -0.5 * LOG2E),
                 jnp.tile(f32(k_norm_a), (1, NA_HEADS))[:, None, :], wdec, f32(b_decay)[:, :, None, :]),
        na=(_na_bias_table(rpb),),
        post=(w_o_a.astype(BF16), w_o_b.astype(BF16), w_out.astype(BF16), f32(gla_norm)[:, None, :],
              f32(norm_ffn)[:, None, :], w_router, b_router[:, None, :]),
        moe=(w_gate.astype(BF16), w_up.astype(BF16), w_down.astype(BF16)),
    )


def _tile(t, want):
    while t % want:
        want //= 2
    return want


def _trunk(x, p, depth):
    batch, n, _ = x.shape
    t = batch * n
    x = x.reshape(t, D_MODEL)
    for l in range(depth):
        qa, ka, va, qb, kb, vb, rb, gf, gr, ga, gb = _in_proj(x, l, *p["in_proj"], _tile(t, 512))
        ya = _na(qa, ka, va, l, *p["na"], batch, n)
        of, ob = _gla(qb, kb, vb, gf, gr, batch, n, GLA_BLOCK_TOK)
        x1, xn, route, meta = _post(x, ya, of, ob, rb, ga, gb, l, *p["post"], _tile(t, 1024))
        meta = meta[:, 0, :2 * N_EXPERTS].reshape(-1)
        x = _moe(meta, xn, route, x1, l, *p["moe"], _tile(t, MOE_TILE))
    return x.reshape(batch, n, D_MODEL)


def kernel(x_prompt, x_sample, norm_mix, w_in, q_norm_a, k_norm_a, rpb, w_decay, b_decay, gla_norm, w_o_a, w_o_b,
           w_out, norm_ffn, w_router_g, b_router_g, w_router_e, b_router_e, w_gate, w_up, w_down):
    p = _prep_params(norm_mix, w_in, q_norm_a, k_norm_a, rpb, w_decay, b_decay, gla_norm, w_o_a, w_o_b, w_out,
                     norm_ffn, w_router_g, b_router_g, w_router_e, b_router_e, w_gate, w_up, w_down)
    depth = w_in.shape[0]
    return (_trunk(x_prompt, p, depth), _trunk(x_sample, p, depth))
```

```python
import functools

import jax
import jax.numpy as jnp
import numpy as np
from jax import lax
from jax.experimental import pallas as pl
from jax.experimental.pallas import tpu as pltpu

F32 = jnp.float32
BF16 = jnp.bfloat16

D_MODEL = 1024
GRID_W = 64
NA_HEADS = 8
NA_HEAD_DIM = 64
NA_WIDTH = NA_HEADS * NA_HEAD_DIM
NA_ROWS = 8
NA_COLS = 16
RPB_ROWS = 2 * NA_ROWS - 1
RPB_COLS = 2 * NA_COLS - 1
GLA_HEADS = 4
GLA_DK = 128
GLA_DV = 128
GLA_WIDTH = GLA_HEADS * GLA_DV
GLA_RANK = 16
GLA_GATE_NORM = 16.0
GLA_CHUNK = 64
GLA_BLOCK_TOK = 512
N_GROUPS = 4
EXPERTS_PER_GROUP = 4
N_EXPERTS = N_GROUPS * EXPERTS_PER_GROUP
D_EXPERT = 512
EPS = 1e-6
NEG = -1e30

LANES = 128
IN_Z_COL0 = 7 * 512
IN_GATE_COL0 = IN_Z_COL0 + LANES
MOE_SUB = 256
MOE_GRAN = 16
MOE_LOC_ROWS = 768
MOE_TILE = 2048
MOE_FAST_GRAN = 4
MOE_LOC_SLACK = (MOE_FAST_GRAN + 1) * MOE_GRAN
MOE_EXPERTS_PER_STEP = 2
MOE_ROW_BLOCK = 384
LOG2E = 1.4426950408889634
EXP_CLAMP = 80.0
ROUTER_GROUP_LANE0 = 0
ROUTER_EXPERT_LANE0 = N_GROUPS
VMEM_LIMIT = 56 * 1024 * 1024

_NT = (((1,), (1,)), ((), ()))
_TN = (((0,), (0,)), ((), ()))


def _dot(a, b):
    return jnp.dot(a, b, preferred_element_type=F32)


def _split_bf16(x):
    hi = x.astype(BF16)
    lo = (x - hi.astype(F32)).astype(BF16)
    return hi, lo


def _sigmoid(x):
    return 1.0 / (1.0 + jnp.exp(-x))


def _const_spec(stacked, l):
    tail = stacked.shape[1:]
    return pl.BlockSpec((None,) + tail, lambda *_: (l,) + (0,) * len(tail), pipeline_mode=pl.Buffered(1))


def _in_proj_kernel(x_ref, nrm_ref, w_ref, qg_ref, kg_ref, wdec_ref, bdec_ref,
                    qa_ref, ka_ref, va_ref, qb_ref, kb_ref, vb_ref, rb_ref, gf_ref, gr_ref, ga_ref, gb_ref):
    x = x_ref[...]
    ms = jnp.mean(x * x, axis=-1, keepdims=True)
    xn = (x * lax.rsqrt(ms + EPS) * nrm_ref[...]).astype(BF16)
    low_half = lax.broadcasted_iota(jnp.int32, (x.shape[0], LANES), 1) < NA_HEAD_DIM

    def proj(lo, width=512):
        return _dot(xn, w_ref[:, lo:lo + width])

    def head_norm(a, gain_ref):
        parts = []
        for p in range(NA_WIDTH // LANES):
            sq = a[:, p * LANES:(p + 1) * LANES]
            sq = sq * sq
            s_lo = jnp.sum(jnp.where(low_half, sq, 0.0), axis=-1, keepdims=True)
            s_hi = jnp.sum(jnp.where(low_half, 0.0, sq), axis=-1, keepdims=True)
            parts.append(jnp.where(low_half, s_lo, s_hi))
        ss = jnp.concatenate(parts, axis=-1)
        return a * lax.rsqrt(ss * (1.0 / NA_HEAD_DIM) + EPS) * gain_ref[...]

    z = proj(IN_Z_COL0, LANES).astype(BF16)
    ga_ref[...] = _sigmoid(proj(IN_GATE_COL0, D_MODEL)).astype(BF16)
    gb_ref[...] = _sigmoid(proj(IN_GATE_COL0 + D_MODEL, D_MODEL)).astype(BF16)
    for d, g_ref in enumerate((gf_ref, gr_ref)):
        lg = _dot(z, wdec_ref[d]) + bdec_ref[d]
        g_ref[...] = (jnp.minimum(lg, 0.0) - jnp.log(1.0 + jnp.exp(-jnp.abs(lg)))) * (1.0 / GLA_GATE_NORM)
    r = proj(3072)
    rb_ref[...] = (r * _sigmoid(r)).astype(BF16)
    qa = proj(0)
    ka = proj(512)
    qb_ref[...] = (proj(1536) * (GLA_DK ** -0.5)).astype(BF16)
    qa_ref[...] = head_norm(qa, qg_ref).astype(BF16)
    kb_ref[...] = proj(2048).astype(BF16)
    ka_ref[...] = head_norm(ka, kg_ref).astype(BF16)
    va_ref[...] = proj(1024).astype(BF16)
    vb_ref[...] = proj(2560).astype(BF16)


def _in_proj(x, l, nrm, w, qg, kg, wdec, bdec, tm):
    t = x.shape[0]
    row = lambda w: pl.BlockSpec((tm, w), lambda i: (i, 0))
    out_w = [512] * 7 + [GLA_WIDTH, GLA_WIDTH, D_MODEL, D_MODEL]
    out_dt = [BF16] * 7 + [F32, F32, BF16, BF16]
    consts = [nrm, w, qg, kg, wdec, bdec]
    return pl.pallas_call(
        _in_proj_kernel,
        grid=(t // tm,),
        in_specs=[row(D_MODEL)] + [_const_spec(c, l) for c in consts],
        out_specs=[row(w) for w in out_w],
        out_shape=[jax.ShapeDtypeStruct((t, w), dt) for w, dt in zip(out_w, out_dt)],
        compiler_params=pltpu.CompilerParams(dimension_semantics=("parallel",), vmem_limit_bytes=VMEM_LIMIT),
        name="in_proj",
    )(x, *consts)


NA_BLOCK_ROWS = 8
NA_BLOCK_TOK = NA_BLOCK_ROWS * GRID_W
NA_WINDOW_TOK = NA_ROWS * GRID_W


def _na_kernel(q_ref, kp_ref, kc_ref, kn_ref, vp_ref, vc_ref, vn_ref, bias_ref, o_ref, kwin, vwin, s_scr, *, rows):
    i = pl.program_id(1)
    kwin[0:NA_BLOCK_TOK] = kp_ref[...]
    kwin[NA_BLOCK_TOK:2 * NA_BLOCK_TOK] = kc_ref[...]
    kwin[2 * NA_BLOCK_TOK:3 * NA_BLOCK_TOK] = kn_ref[...]
    vwin[0:NA_BLOCK_TOK] = vp_ref[...]
    vwin[NA_BLOCK_TOK:2 * NA_BLOCK_TOK] = vc_ref[...]
    vwin[2 * NA_BLOCK_TOK:3 * NA_BLOCK_TOK] = vn_ref[...]
    low_half = lax.broadcasted_iota(jnp.int32, (GRID_W, LANES), 1) < NA_HEAD_DIM
    pairs = range(NA_HEADS // 2)
    cols = lambda p: slice(p * LANES, (p + 1) * LANES)
    qrows = lambda j: slice(j * GRID_W, (j + 1) * GRID_W)

    def window(j):
        r = i * NA_BLOCK_ROWS + j
        rs = jnp.clip(r - NA_ROWS // 2, 0, rows - NA_ROWS)
        start = pl.multiple_of((rs - i * NA_BLOCK_ROWS + NA_BLOCK_ROWS) * GRID_W, GRID_W)
        return start, rs - r + NA_ROWS - 1

    def scores(j):
        start, bias_off = window(j)
        for p in pairs:
            qp = q_ref[qrows(j), cols(p)]
            zero = jnp.zeros_like(qp)
            q2 = jnp.concatenate([jnp.where(low_half, qp, zero), jnp.where(low_half, zero, qp)], axis=0)
            s = lax.dot_general(q2, kwin[pl.ds(start, NA_WINDOW_TOK), cols(p)], _NT, preferred_element_type=F32)
            s_scr[j % 2, p] = s + bias_ref[bias_off, p]

    def attend(j):
        start, _ = window(j)
        for p in pairs:
            s = s_scr[j % 2, p]
            e = jnp.exp2(s - jnp.max(s, axis=-1, keepdims=True))
            l = jnp.sum(e, axis=-1, keepdims=True)
            pv = _dot(e.astype(BF16), vwin[pl.ds(start, NA_WINDOW_TOK), cols(p)]) * (1.0 / l)
            o_ref[qrows(j), cols(p)] = jnp.where(low_half, pv[:GRID_W], pv[GRID_W:]).astype(BF16)

    scores(0)
    for j in range(NA_BLOCK_ROWS):
        if j + 1 < NA_BLOCK_ROWS:
            scores(j + 1)
        attend(j)


def _na(q, k, v, l, bias, batch, n):
    rows = n // GRID_W
    nblk = rows // NA_BLOCK_ROWS
    spec = lambda f: pl.BlockSpec((NA_BLOCK_TOK, NA_WIDTH), lambda b, i: (b * nblk + f(i), 0))
    cur = spec(lambda i: i)
    prev = spec(lambda i: jnp.maximum(i - 1, 0))
    nxt = spec(lambda i: jnp.minimum(i + 1, nblk - 1))
    return pl.pallas_call(
        functools.partial(_na_kernel, rows=rows),
        grid=(batch, nblk),
        in_specs=[cur, prev, cur, nxt, prev, cur, nxt, _const_spec(bias, l)],
        out_specs=cur,
        out_shape=jax.ShapeDtypeStruct((batch * n, NA_WIDTH), BF16),
        scratch_shapes=[pltpu.VMEM((3 * NA_BLOCK_TOK, NA_WIDTH), BF16),
                        pltpu.VMEM((3 * NA_BLOCK_TOK, NA_WIDTH), BF16),
                        pltpu.VMEM((2, NA_HEADS // 2, 2 * GRID_W, NA_WINDOW_TOK), F32)],
        compiler_params=pltpu.CompilerParams(dimension_semantics=("parallel", "parallel"),
                                             vmem_limit_bytes=VMEM_LIMIT),
        name="na",
    )(q, k, k, k, v, v, v, bias)


def _na_bias_table(rpb):
    cols = np.arange(GRID_W)
    col_start = np.clip(cols - NA_COLS // 2, 0, GRID_W - NA_COLS)
    col_mask = (cols[None, :] >= col_start[:, None]) & (cols[None, :] < col_start[:, None] + NA_COLS)
    col_idx = np.clip(cols[None, :] - cols[:, None] + NA_COLS - 1, 0, RPB_COLS - 1)
    onehot = (col_idx[None] == np.arange(RPB_COLS)[:, None, None]).astype(np.float32)
    bias = jnp.einsum("lhrk,kqc->lhqrc", rpb.astype(F32) * LOG2E, onehot, precision=lax.Precision.HIGHEST)
    masked = jnp.where(col_mask[:, None, :], bias, NEG)
    depth = rpb.shape[0]
    slabs = [masked[:, :, :, o:o + NA_ROWS].reshape(depth, NA_HEADS // 2, 2 * GRID_W, NA_WINDOW_TOK)
             for o in range(NA_ROWS)]
    return jnp.stack(slabs, axis=1)


def _gla_chunk_rows(s):
    return slice(s * GLA_CHUNK, (s + 1) * GLA_CHUNK)


def _gla_head_cols(h):
    return slice(h * GLA_DK, (h + 1) * GLA_DK)


def _gla_cumulate(d, g, nsub):
    tri = jnp.where(d["mask"], 1.0, 0.0).astype(BF16)
    ghi, glo = _split_bf16(g)
    return [_dot(tri, ghi[_gla_chunk_rows(s)]) + _dot(tri, glo[_gla_chunk_rows(s)]) for s in range(nsub)]


def _gla_rescale(d, b_chunks):
    mid, end = d["mid"], d["end"]
    scaled = []
    for s, b in enumerate(b_chunks):
        rows = _gla_chunk_rows(s)
        b_mid = b[mid:mid + 1, :]
        b_end = b[end:end + 1, :]
        qe = d["q"][rows, :] * jnp.exp(jnp.minimum(b - b_mid, EXP_CLAMP)).astype(BF16)
        ke = d["k"][rows, :] * jnp.exp(jnp.minimum(b_mid - b, EXP_CLAMP)).astype(BF16)
        d["qi"][rows, :] = qe * jnp.exp(b_mid).astype(BF16)
        d["dec"][s] = jnp.broadcast_to(jnp.exp(b_end), (8, GLA_WIDTH))
        k_out = ke * jnp.exp(b_end - b_mid).astype(BF16)
        scaled.append((qe, ke, k_out))
    return scaled


def _gla_products(d, scaled):
    pairs = [(s, h) for s in range(len(scaled)) for h in range(GLA_HEADS)]
    v = lambda s, h: d["v"][_gla_chunk_rows(s), _gla_head_cols(h)]
    a = [lax.dot_general(scaled[s][0][:, _gla_head_cols(h)], scaled[s][1][:, _gla_head_cols(h)], _NT,
                         preferred_element_type=F32) for s, h in pairs]
    for s, h in pairs:
        d["kv"][s, h] = lax.dot_general(v(s, h), scaled[s][2][:, _gla_head_cols(h)], _TN,
                                        preferred_element_type=F32)
    for (s, h), a_sh in zip(pairs, a):
        d["oi"][_gla_chunk_rows(s), _gla_head_cols(h)] = _dot(jnp.where(d["mask"], a_sh, 0.0).astype(BF16), v(s, h))


def _gla_scan_step(d, s):
    rows = _gla_chunk_rows(s)
    dec = d["dec"][s]
    states = [d["st"][h] for h in range(GLA_HEADS)]
    inter = [lax.dot_general(d["qi"][rows, _gla_head_cols(h)], states[h].astype(BF16), _NT,
                             preferred_element_type=F32) for h in range(GLA_HEADS)]
    for h in range(GLA_HEADS):
        cols = _gla_head_cols(h)
        d["st"][h] = states[h] * dec[0:1, cols] + d["kv"][s, h]
        d["o"][rows, cols] = (d["oi"][rows, cols] + inter[h]).astype(BF16)


def _gla_kernel(qf_ref, kf_ref, vf_ref, gf_ref, qr_ref, kr_ref, vr_ref, gr_ref,
                of_ref, ob_ref, stf_ref, stb_ref, oif_s, oib_s, qif_s, qib_s, kvf_s, kvb_s, decf_s, decb_s, *, nsub):
    @pl.when(pl.program_id(1) == 0)
    def _():
        stf_ref[...] = jnp.zeros_like(stf_ref)
        stb_ref[...] = jnp.zeros_like(stb_ref)

    c = GLA_CHUNK
    ri = lax.broadcasted_iota(jnp.int32, (c, c), 0)
    ci = lax.broadcasted_iota(jnp.int32, (c, c), 1)
    fwd = dict(q=qf_ref, k=kf_ref, v=vf_ref, g=gf_ref, o=of_ref, st=stf_ref, oi=oif_s,
               qi=qif_s, kv=kvf_s, dec=decf_s, mask=ci <= ri, mid=c // 2 - 1, end=c - 1)
    bwd = dict(q=qr_ref, k=kr_ref, v=vr_ref, g=gr_ref, o=ob_ref, st=stb_ref, oi=oib_s,
               qi=qib_s, kv=kvb_s, dec=decb_s, mask=ci >= ri, mid=c // 2, end=0)
    b = [_gla_cumulate(d, d["g"][...], nsub) for d in (fwd, bwd)]
    scaled = [_gla_rescale(d, b_d) for d, b_d in zip((fwd, bwd), b)]
    for d, scaled_d in zip((fwd, bwd), scaled):
        _gla_products(d, scaled_d)
    for s in range(nsub):
        _gla_scan_step(fwd, s)
        _gla_scan_step(bwd, nsub - 1 - s)


def _gla(q, k, v, gf, gr, batch, n, cb):
    nb = n // cb
    nsub = cb // GLA_CHUNK
    fwd = lambda w: pl.BlockSpec((cb, w), lambda b, c: (b * nb + c, 0))
    bwd = lambda w: pl.BlockSpec((cb, w), lambda b, c: (b * nb + nb - 1 - c, 0))
    out = jax.ShapeDtypeStruct((batch * n, GLA_WIDTH), BF16)
    state = pltpu.VMEM((GLA_HEADS, GLA_DV, GLA_DK), F32)
    return pl.pallas_call(
        functools.partial(_gla_kernel, nsub=nsub),
        grid=(batch, nb),
        in_specs=[fwd(GLA_WIDTH)] * 4 + [bwd(GLA_WIDTH)] * 4,
        out_specs=[fwd(GLA_WIDTH), bwd(GLA_WIDTH)],
        out_shape=[out, out],
        scratch_shapes=[state, state,
                        pltpu.VMEM((cb, GLA_WIDTH), F32), pltpu.VMEM((cb, GLA_WIDTH), F32),
                        pltpu.VMEM((cb, GLA_WIDTH), BF16), pltpu.VMEM((cb, GLA_WIDTH), BF16),
                        pltpu.VMEM((nsub, GLA_HEADS, GLA_DV, GLA_DK), F32),
                        pltpu.VMEM((nsub, GLA_HEADS, GLA_DV, GLA_DK), F32),
                        pltpu.VMEM((nsub, 8, GLA_WIDTH), F32), pltpu.VMEM((nsub, 8, GLA_WIDTH), F32)],
        compiler_params=pltpu.CompilerParams(dimension_semantics=("parallel", "arbitrary"),
                                             vmem_limit_bytes=VMEM_LIMIT),
        name="gla",
    )(q, k, v, gf, q, k, v, gr)


def _route(logits):
    lane = lax.broadcasted_iota(jnp.int32, logits.shape, 1)
    lane_f = lane.astype(F32)
    ninf = -jnp.inf

    def first_argmax(vals, vmax):
        return jnp.min(jnp.where(vals == vmax, lane_f, float(LANES)), axis=-1, keepdims=True)

    lg = jnp.where(lane < N_GROUPS, logits, ninf)
    g_max = jnp.max(lg, axis=-1, keepdims=True)
    p_sel = 1.0 / jnp.sum(jnp.exp(lg - g_max), axis=-1, keepdims=True)
    grp = first_argmax(lg, g_max)
    e_lo = ROUTER_EXPERT_LANE0 + EXPERTS_PER_GROUP * grp
    le = jnp.where((lane_f >= e_lo) & (lane_f < e_lo + EXPERTS_PER_GROUP), logits, ninf)
    l1 = jnp.max(le, axis=-1, keepdims=True)
    i1 = first_argmax(le, l1)
    le2 = jnp.where(lane_f == i1, ninf, le)
    l2 = jnp.max(le2, axis=-1, keepdims=True)
    i2 = first_argmax(le2, l2)
    t = jnp.exp(l2 - l1)
    w1 = p_sel / (1.0 + t)
    w2 = w1 * t
    return lane_f == i1, lane_f == i2, w1, w2


def _local_sort(sel1, sel2, w1, w2):
    n = sel1.shape[0]
    ri = lax.broadcasted_iota(jnp.int32, (n, n), 0)
    ci = lax.broadcasted_iota(jnp.int32, (n, n), 1)
    before = jnp.where(ci < ri, 1.0, 0.0).astype(BF16)
    li = lax.broadcasted_iota(jnp.int32, (LANES, LANES), 0)
    lj = lax.broadcasted_iota(jnp.int32, (LANES, LANES), 1)
    lower_lanes = jnp.where(li < lj, 1.0, 0.0).astype(BF16)
    oh1 = jnp.where(sel1, 1.0, 0.0)
    oh2 = jnp.where(sel2, 1.0, 0.0)
    cnt1 = jnp.sum(oh1, axis=0, keepdims=True)
    cnt = cnt1 + jnp.sum(oh2, axis=0, keepdims=True)
    gran = jnp.floor((cnt + (MOE_GRAN - 1)) * (1.0 / MOE_GRAN))
    offg = _dot(jnp.broadcast_to(gran, (8, LANES)).astype(BF16), lower_lanes)[0:1]
    pick = lambda sel, vals: jnp.sum(jnp.where(sel, vals, 0.0), axis=-1, keepdims=True)
    pos1 = pick(sel1, _dot(before, oh1.astype(BF16)) + MOE_GRAN * offg)
    pos2 = pick(sel2, _dot(before, oh2.astype(BF16)) + cnt1 + MOE_GRAN * offg)
    lane = lax.broadcasted_iota(jnp.int32, (n, LANES), 1)
    route = jnp.where(lane == 0, pos1, jnp.where(lane == 1, pos2, jnp.where(lane == 2, w1,
                                                                             jnp.where(lane == 3, w2, 0.0))))
    to_off = jnp.where((li == lj + ROUTER_EXPERT_LANE0) & (lj < N_EXPERTS), 1.0, 0.0).astype(BF16)
    to_len = jnp.where(li == lj + ROUTER_EXPERT_LANE0 - N_EXPERTS, 1.0, 0.0).astype(BF16)
    meta = (_dot(jnp.broadcast_to(offg, (8, LANES)).astype(BF16), to_off)
            + _dot(jnp.broadcast_to(gran, (8, LANES)).astype(BF16), to_len))
    return route, meta.astype(jnp.int32)


def _post_kernel(x_ref, ya_ref, of_ref, ob_ref, rb_ref, ga_ref, gb_ref, woa_ref, wob_ref, wout_ref,
                 gn_ref, nf_ref, wr_ref, br_ref, x1_ref, xn_ref, route_ref, meta_ref):
    nsub = x_ref.shape[0] // MOE_SUB
    subs = [slice(u * MOE_SUB, (u + 1) * MOE_SUB) for u in range(nsub)]
    val = [dict() for _ in range(nsub)]

    def branches(u):
        rows = subs[u]
        o = of_ref[rows, :].astype(F32) + ob_ref[rows, :].astype(F32)
        parts = []
        for h in range(GLA_HEADS):
            oh = o[:, h * GLA_DV:(h + 1) * GLA_DV]
            ms = jnp.mean(oh * oh, axis=-1, keepdims=True)
            parts.append(oh * lax.rsqrt(ms + EPS) * gn_ref[...])
        on = (jnp.concatenate(parts, axis=-1) * rb_ref[rows, :].astype(F32)).astype(BF16)
        val[u]["ya"] = _dot(ya_ref[rows, :], woa_ref[...])
        val[u]["yb"] = _dot(on, wob_ref[...])

    def merge(u):
        rows = subs[u]
        y = ga_ref[rows, :] * val[u]["ya"].astype(BF16) + gb_ref[rows, :] * val[u]["yb"].astype(BF16)
        val[u]["x1"] = x_ref[rows, :] + _dot(y, wout_ref[...])

    def norm_logits(u):
        rows = subs[u]
        x1 = val[u]["x1"]
        x1_ref[rows, :] = x1
        ms = jnp.mean(x1 * x1, axis=-1, keepdims=True)
        thi, tlo = _split_bf16(x1 * lax.rsqrt(ms + EPS) * nf_ref[...])
        xn_ref[rows, :] = thi
        hi = _dot(thi, wr_ref[...])
        val[u]["logits"] = hi[:, :LANES] + hi[:, LANES:] + _dot(tlo, wr_ref[:, :LANES]) + br_ref[...]

    def route(u):
        route_ref[subs[u], :], meta_ref[u] = _local_sort(*_route(val[u]["logits"]))

    stages = [branches, merge, norm_logits, route]
    for step in range(nsub + len(stages) - 1):
        for u in range(nsub):
            if 0 <= step - u < len(stages):
                stages[step - u](u)


def _post(x, ya, of, ob, rb, ga, gb, l, woa, wob, wout, gn, nf, wr, br, tm):
    t = x.shape[0]
    row = lambda w: pl.BlockSpec((tm, w), lambda i: (i, 0))
    consts = [woa, wob, wout, gn, nf, wr, br]
    return pl.pallas_call(
        _post_kernel,
        grid=(t // tm,),
        in_specs=[row(D_MODEL), row(NA_WIDTH), row(GLA_WIDTH), row(GLA_WIDTH), row(GLA_WIDTH),
                  row(D_MODEL), row(D_MODEL)] + [_const_spec(c, l) for c in consts],
        out_specs=[row(D_MODEL), row(D_MODEL), row(LANES),
                   pl.BlockSpec((tm // MOE_SUB, 8, LANES), lambda i: (i, 0, 0))],
        out_shape=[jax.ShapeDtypeStruct((t, D_MODEL), F32), jax.ShapeDtypeStruct((t, D_MODEL), BF16),
                   jax.ShapeDtypeStruct((t, LANES), F32),
                   jax.ShapeDtypeStruct((t // MOE_SUB, 8, LANES), jnp.int32)],
        compiler_params=pltpu.CompilerParams(dimension_semantics=("parallel",), vmem_limit_bytes=VMEM_LIMIT),
        name="post",
    )(x, ya, of, ob, rb, ga, gb, *consts)


def _moe_kernel(meta_ref, xn_ref, route_ref, x1_ref, wg_ref, wu_ref, wd_ref, o_ref, loc, stage, *, nsub):
    i = pl.program_id(0)
    step = pl.program_id(1)
    stage_spare = stage.shape[0] // MOE_GRAN - 1
    sub_rows = lambda u: slice(u * MOE_SUB, (u + 1) * MOE_SUB)
    gran_rows = lambda g: pl.ds(pl.multiple_of(g * MOE_GRAN, MOE_GRAN), MOE_GRAN)

    @pl.when(step == 0)
    def _dispatch():
        stage[...] = jnp.zeros_like(stage)
        sel = lax.broadcasted_iota(jnp.int32, (8, LANES), 0) == lax.broadcasted_iota(jnp.int32, (8, LANES), 1)
        sel = jnp.where(sel, 1.0, 0.0).astype(BF16)
        row_id = lax.broadcasted_iota(jnp.int32, (MOE_LOC_ROWS, MOE_SUB), 0).astype(F32)
        for u in range(nsub):
            rhi, rlo = _split_bf16(route_ref[sub_rows(u), :])
            pos_t = (lax.dot_general(sel, rhi, _NT, preferred_element_type=F32)
                     + lax.dot_general(sel, rlo, _NT, preferred_element_type=F32))
            onehot = jnp.where(row_id == pos_t[0:1], 1.0, 0.0) + jnp.where(row_id == pos_t[1:2], 1.0, 0.0)
            loc[u, :MOE_LOC_ROWS, :] = _dot(onehot.astype(BF16), xn_ref[sub_rows(u), :]).astype(BF16)
            loc[u, MOE_LOC_ROWS:, :] = jnp.zeros((MOE_LOC_SLACK, D_MODEL), BF16)

    def move_granules(e, to_stage):
        meta = lambda u, k: meta_ref[((i * nsub + u) * 2 + k) * N_EXPERTS + e]

        def move(u, off, dst, g, live=None):
            if to_stage:
                to = dst + g if live is None else jnp.where(live, dst + g, stage_spare)
                stage[gran_rows(to), :] = loc[u, gran_rows(off + g), :]
            else:
                to = off + g if live is None else jnp.where(live, off + g, MOE_LOC_ROWS // MOE_GRAN)
                loc[u, gran_rows(to), :] = stage[gran_rows(dst + g), :]

        runs = []
        dst = 0
        for u in range(nsub):
            off, n = meta(u, 0), meta(u, 1)
            for g in range(MOE_FAST_GRAN):
                move(u, off, dst, g, live=g < n)
            runs.append((u, off, n, dst))
            dst = dst + n

        @pl.when(functools.reduce(jnp.maximum, [n for _, _, n, _ in runs]) > MOE_FAST_GRAN)
        def _long_runs():
            for u, off, n, dst_u in runs:
                lax.fori_loop(MOE_FAST_GRAN, n, lambda g, c, u=u, off=off, dst_u=dst_u: (move(u, off, dst_u, g), c)[1], 0)

        return dst

    for k in range(MOE_EXPERTS_PER_STEP):
        e = step * MOE_EXPERTS_PER_STEP + k
        total = move_granules(e, True)

        def block(b, carry, k=k):
            rows = pl.ds(pl.multiple_of(b * MOE_ROW_BLOCK, MOE_ROW_BLOCK), MOE_ROW_BLOCK)
            xs = stage[rows, :]
            hg = _dot(xs, wg_ref[k])
            h = (hg * _sigmoid(hg) * _dot(xs, wu_ref[k])).astype(BF16)
            stage[rows, :] = _dot(h, wd_ref[k]).astype(BF16)
            return carry

        lax.fori_loop(0, (total * MOE_GRAN + MOE_ROW_BLOCK - 1) // MOE_ROW_BLOCK, block, 0)
        move_granules(e, False)

    @pl.when(step == N_EXPERTS // MOE_EXPERTS_PER_STEP - 1)
    def _combine():
        col_id = lax.broadcasted_iota(jnp.int32, (MOE_SUB, MOE_LOC_ROWS), 1).astype(F32)
        for u in range(nsub):
            r = route_ref[sub_rows(u), :]
            pick = jnp.where(col_id == r[:, 0:1], r[:, 2:3], jnp.where(col_id == r[:, 1:2], r[:, 3:4], 0.0))
            o_ref[sub_rows(u), :] = x1_ref[sub_rows(u), :] + _dot(pick.astype(BF16), loc[u, :MOE_LOC_ROWS, :])


def _moe(meta, xn, route, x1, l, wg, wu, wd, tb):
    t = xn.shape[0]
    nsub = tb // MOE_SUB
    row = lambda w, **kw: pl.BlockSpec((tb, w), lambda i, e, m: (i, 0), **kw)
    wspec = lambda a, b: pl.BlockSpec((None, MOE_EXPERTS_PER_STEP, a, b), lambda i, s, m: (l, s, 0, 0))
    return pl.pallas_call(
        functools.partial(_moe_kernel, nsub=nsub),
        grid_spec=pltpu.PrefetchScalarGridSpec(
            num_scalar_prefetch=1,
            grid=(t // tb, N_EXPERTS // MOE_EXPERTS_PER_STEP),
            in_specs=[row(D_MODEL, pipeline_mode=pl.Buffered(1)), row(LANES),
                      row(D_MODEL, pipeline_mode=pl.Buffered(1)),
                      wspec(D_MODEL, D_EXPERT), wspec(D_MODEL, D_EXPERT), wspec(D_EXPERT, D_MODEL)],
            out_specs=row(D_MODEL, pipeline_mode=pl.Buffered(1)),
            scratch_shapes=[pltpu.VMEM((nsub, MOE_LOC_ROWS + MOE_LOC_SLACK, D_MODEL), BF16),
                            pltpu.VMEM((pl.cdiv(tb + MOE_LOC_SLACK, MOE_ROW_BLOCK) * MOE_ROW_BLOCK, D_MODEL), BF16)]),
        out_shape=jax.ShapeDtypeStruct((t, D_MODEL), F32),
        compiler_params=pltpu.CompilerParams(dimension_semantics=("parallel", "arbitrary"),
                                             vmem_limit_bytes=VMEM_LIMIT),
        name="moe",
    )(meta, xn, route, x1, wg, wu, wd)


def _prep_params(norm_mix, w_in, q_norm_a, k_norm_a, rpb, w_decay, b_decay, gla_norm, w_o_a, w_o_b, w_out,
                 norm_ffn, w_router_g, b_router_g, w_router_e, b_router_e, w_gate, w_up, w_down):
    depth = w_in.shape[0]
    f32 = lambda a: a.astype(F32)
    pad_lanes = lambda a, before, total: jnp.pad(f32(a), [(0, 0)] * (a.ndim - 1) + [(before, total - before - a.shape[-1])])
    w_aligned = jnp.concatenate([w_in[:, :, :IN_Z_COL0 + 2 * GLA_RANK],
                                 jnp.zeros((depth, D_MODEL, IN_GATE_COL0 - IN_Z_COL0 - 2 * GLA_RANK), w_in.dtype),
                                 w_in[:, :, IN_Z_COL0 + 2 * GLA_RANK:]], axis=-1).astype(BF16)
    w_router = pad_lanes(jnp.concatenate([w_router_g, w_router_e], axis=-1), ROUTER_GROUP_LANE0, LANES)
    b_router = pad_lanes(jnp.concatenate([b_router_g, b_router_e], axis=-1), ROUTER_GROUP_LANE0, LANES)
    w_router = jnp.concatenate(_split_bf16(w_router), axis=-1)
    wdec = jnp.stack([jnp.pad(f32(w_decay[:, d]), ((0, 0), (d * GLA_RANK, LANES - (d + 1) * GLA_RANK), (0, 0)))
                      for d in range(2)], axis=1).astype(BF16)
    return dict(
        in_proj=(f32(norm_mix)[:, None, :], w_aligned,
                 jnp.tile(f32(q_norm_a), (1, NA_HEADS))[:, None, :] * (NA_HEAD_DIM ** -0.5 * LOG2E),
                 jnp.tile(f32(k_norm_a), (1, NA_HEADS))[:, None, :], wdec, f32(b_decay)[:, :, None, :]),
        na=(_na_bias_table(rpb),),
        post=(w_o_a.astype(BF16), w_o_b.astype(BF16), w_out.astype(BF16), f32(gla_norm)[:, None, :],
              f32(norm_ffn)[:, None, :], w_router, b_router[:, None, :]),
        moe=(w_gate.astype(BF16), w_up.astype(BF16), w_down.astype(BF16)),
    )


def _tile(t, want):
    while t % want:
        want //= 2
    return want


def _trunk(x, p, depth):
    batch, n, _ = x.shape
    t = batch * n
    x = x.reshape(t, D_MODEL)
    for l in range(depth):
        qa, ka, va, qb, kb, vb, rb, gf, gr, ga, gb = _in_proj(x, l, *p["in_proj"], _tile(t, 512))
        ya = _na(qa, ka, va, l, *p["na"], batch, n)
        of, ob = _gla(qb, kb, vb, gf, gr, batch, n, GLA_BLOCK_TOK)
        x1, xn, route, meta = _post(x, ya, of, ob, rb, ga, gb, l, *p["post"], _tile(t, 1024))
        meta = meta[:, 0, :2 * N_EXPERTS].reshape(-1)
        x = _moe(meta, xn, route, x1, l, *p["moe"], _tile(t, MOE_TILE))
    return x.reshape(batch, n, D_MODEL)


def kernel(x_prompt, x_sample, norm_mix, w_in, q_norm_a, k_norm_a, rpb, w_decay, b_decay, gla_norm, w_o_a, w_o_b,
           w_out, norm_ffn, w_router_g, b_router_g, w_router_e, b_router_e, w_gate, w_up, w_down):
    p = _prep_params(norm_mix, w_in, q_norm_a, k_norm_a, rpb, w_decay, b_decay, gla_norm, w_o_a, w_o_b, w_out,
                     norm_ffn, w_router_g, b_router_g, w_router_e, b_router_e, w_gate, w_up, w_down)
    depth = w_in.shape[0]
    return (_trunk(x_prompt, p, depth), _trunk(x_sample, p, depth))
```

```python
import functools

import jax
import jax.numpy as jnp
import numpy as np
from jax import lax
from jax.experimental import pallas as pl
from jax.experimental.pallas import tpu as pltpu

F32 = jnp.float32
BF16 = jnp.bfloat16

D_MODEL = 1024
GRID_W = 64
NA_HEADS = 8
NA_HEAD_DIM = 64
NA_WIDTH = NA_HEADS * NA_HEAD_DIM
NA_ROWS = 8
NA_COLS = 16
RPB_ROWS = 2 * NA_ROWS - 1
RPB_COLS = 2 * NA_COLS - 1
GLA_HEADS = 4
GLA_DK = 128
GLA_DV = 128
GLA_WIDTH = GLA_HEADS * GLA_DV
GLA_RANK = 16
GLA_GATE_NORM = 16.0
GLA_CHUNK = 64
GLA_BLOCK_TOK = 512
N_GROUPS = 4
EXPERTS_PER_GROUP = 4
N_EXPERTS = N_GROUPS * EXPERTS_PER_GROUP
D_EXPERT = 512
EPS = 1e-6
NEG = -1e30

LANES = 128
IN_Z_COL0 = 7 * 512
IN_GATE_COL0 = IN_Z_COL0 + LANES
MOE_SUB = 256
MOE_GRAN = 16
MOE_LOC_ROWS = 768
MOE_TILE = 2048
MOE_FAST_GRAN = 4
MOE_LOC_SLACK = (MOE_FAST_GRAN + 1) * MOE_GRAN
MOE_EXPERTS_PER_STEP = 1
MOE_EXPERT_STEPS = N_EXPERTS // MOE_EXPERTS_PER_STEP
MOE_COMBINE_STEPS = 2
MOE_ROW_BLOCK = 384
LOG2E = 1.4426950408889634
EXP_CLAMP = 80.0
ROUTER_GROUP_LANE0 = 0
ROUTER_EXPERT_LANE0 = N_GROUPS
VMEM_LIMIT = 56 * 1024 * 1024

_NT = (((1,), (1,)), ((), ()))
_TN = (((0,), (0,)), ((), ()))


def _dot(a, b):
    return jnp.dot(a, b, preferred_element_type=F32)


def _split_bf16(x):
    hi = x.astype(BF16)
    lo = (x - hi.astype(F32)).astype(BF16)
    return hi, lo


def _sigmoid(x):
    return 1.0 / (1.0 + jnp.exp(-x))


def _const_spec(stacked, l):
    tail = stacked.shape[1:]
    return pl.BlockSpec((None,) + tail, lambda *_: (l,) + (0,) * len(tail), pipeline_mode=pl.Buffered(1))


def _in_proj_kernel(x_ref, nrm_ref, w_ref, qg_ref, kg_ref, wdec_ref, bdec_ref,
                    qa_ref, ka_ref, va_ref, qb_ref, kb_ref, vb_ref, rb_ref, gf_ref, gr_ref, ga_ref, gb_ref):
    x = x_ref[...]
    ms = jnp.mean(x * x, axis=-1, keepdims=True)
    xn = (x * lax.rsqrt(ms + EPS) * nrm_ref[...]).astype(BF16)
    low_half = lax.broadcasted_iota(jnp.int32, (x.shape[0], LANES), 1) < NA_HEAD_DIM

    def proj(lo, width=512):
        return _dot(xn, w_ref[:, lo:lo + width])

    def head_norm(a, gain_ref):
        parts = []
        for p in range(NA_WIDTH // LANES):
            sq = a[:, p * LANES:(p + 1) * LANES]
            sq = sq * sq
            s_lo = jnp.sum(jnp.where(low_half, sq, 0.0), axis=-1, keepdims=True)
            s_hi = jnp.sum(jnp.where(low_half, 0.0, sq), axis=-1, keepdims=True)
            parts.append(jnp.where(low_half, s_lo, s_hi))
        ss = jnp.concatenate(parts, axis=-1)
        return a * lax.rsqrt(ss * (1.0 / NA_HEAD_DIM) + EPS) * gain_ref[...]

    z = proj(IN_Z_COL0, LANES).astype(BF16)
    ga_ref[...] = _sigmoid(proj(IN_GATE_COL0, D_MODEL)).astype(BF16)
    gb_ref[...] = _sigmoid(proj(IN_GATE_COL0 + D_MODEL, D_MODEL)).astype(BF16)
    for d, g_ref in enumerate((gf_ref, gr_ref)):
        lg = _dot(z, wdec_ref[d]) + bdec_ref[d]
        g_ref[...] = (jnp.minimum(lg, 0.0) - jnp.log(1.0 + jnp.exp(-jnp.abs(lg)))) * (1.0 / GLA_GATE_NORM)
    r = proj(3072)
    rb_ref[...] = (r * _sigmoid(r)).astype(BF16)
    qa = proj(0)
    ka = proj(512)
    qb_ref[...] = (proj(1536) * (GLA_DK ** -0.5)).astype(BF16)
    qa_ref[...] = head_norm(qa, qg_ref).astype(BF16)
    kb_ref[...] = proj(2048).astype(BF16)
    ka_ref[...] = head_norm(ka, kg_ref).astype(BF16)
    va_ref[...] = proj(1024).astype(BF16)
    vb_ref[...] = proj(2560).astype(BF16)


def _in_proj(x, l, nrm, w, qg, kg, wdec, bdec, tm):
    t = x.shape[0]
    row = lambda w: pl.BlockSpec((tm, w), lambda i: (i, 0))
    out_w = [512] * 7 + [GLA_WIDTH, GLA_WIDTH, D_MODEL, D_MODEL]
    out_dt = [BF16] * 7 + [F32, F32, BF16, BF16]
    consts = [nrm, w, qg, kg, wdec, bdec]
    return pl.pallas_call(
        _in_proj_kernel,
        grid=(t // tm,),
        in_specs=[row(D_MODEL)] + [_const_spec(c, l) for c in consts],
        out_specs=[row(w) for w in out_w],
        out_shape=[jax.ShapeDtypeStruct((t, w), dt) for w, dt in zip(out_w, out_dt)],
        compiler_params=pltpu.CompilerParams(dimension_semantics=("parallel",), vmem_limit_bytes=VMEM_LIMIT),
        name="in_proj",
    )(x, *consts)


NA_BLOCK_ROWS = 8
NA_BLOCK_TOK = NA_BLOCK_ROWS * GRID_W
NA_WINDOW_TOK = NA_ROWS * GRID_W


def _na_kernel(q_ref, kp_ref, kc_ref, kn_ref, vp_ref, vc_ref, vn_ref, bias_ref, o_ref, kwin, vwin, s_scr, *, rows):
    i = pl.program_id(1)
    kwin[0:NA_BLOCK_TOK] = kp_ref[...]
    kwin[NA_BLOCK_TOK:2 * NA_BLOCK_TOK] = kc_ref[...]
    kwin[2 * NA_BLOCK_TOK:3 * NA_BLOCK_TOK] = kn_ref[...]
    vwin[0:NA_BLOCK_TOK] = vp_ref[...]
    vwin[NA_BLOCK_TOK:2 * NA_BLOCK_TOK] = vc_ref[...]
    vwin[2 * NA_BLOCK_TOK:3 * NA_BLOCK_TOK] = vn_ref[...]
    low_half = lax.broadcasted_iota(jnp.int32, (GRID_W, LANES), 1) < NA_HEAD_DIM
    pairs = range(NA_HEADS // 2)
    cols = lambda p: slice(p * LANES, (p + 1) * LANES)
    qrows = lambda j: slice(j * GRID_W, (j + 1) * GRID_W)

    def window(j):
        r = i * NA_BLOCK_ROWS + j
        rs = jnp.clip(r - NA_ROWS // 2, 0, rows - NA_ROWS)
        start = pl.multiple_of((rs - i * NA_BLOCK_ROWS + NA_BLOCK_ROWS) * GRID_W, GRID_W)
        return start, rs - r + NA_ROWS - 1

    def scores(j):
        start, bias_off = window(j)
        for p in pairs:
            qp = q_ref[qrows(j), cols(p)]
            zero = jnp.zeros_like(qp)
            q2 = jnp.concatenate([jnp.where(low_half, qp, zero), jnp.where(low_half, zero, qp)], axis=0)
            s = lax.dot_general(q2, kwin[pl.ds(start, NA_WINDOW_TOK), cols(p)], _NT, preferred_element_type=F32)
            s_scr[j % 2, p] = s + bias_ref[bias_off, p]

    def attend(j):
        start, _ = window(j)
        for p in pairs:
            s = s_scr[j % 2, p]
            e = jnp.exp2(s - jnp.max(s, axis=-1, keepdims=True))
            l = jnp.sum(e, axis=-1, keepdims=True)
            pv = _dot(e.astype(BF16), vwin[pl.ds(start, NA_WINDOW_TOK), cols(p)]) * (1.0 / l)
            o_ref[qrows(j), cols(p)] = jnp.where(low_half, pv[:GRID_W], pv[GRID_W:]).astype(BF16)

    scores(0)
    for j in range(NA_BLOCK_ROWS):
        if j + 1 < NA_BLOCK_ROWS:
            scores(j + 1)
        attend(j)


def _na(q, k, v, l, bias, batch, n):
    rows = n // GRID_W
    nblk = rows // NA_BLOCK_ROWS
    spec = lambda f: pl.BlockSpec((NA_BLOCK_TOK, NA_WIDTH), lambda b, i: (b * nblk + f(i), 0))
    cur = spec(lambda i: i)
    prev = spec(lambda i: jnp.maximum(i - 1, 0))
    nxt = spec(lambda i: jnp.minimum(i + 1, nblk - 1))
    return pl.pallas_call(
        functools.partial(_na_kernel, rows=rows),
        grid=(batch, nblk),
        in_specs=[cur, prev, cur, nxt, prev, cur, nxt, _const_spec(bias, l)],
        out_specs=cur,
        out_shape=jax.ShapeDtypeStruct((batch * n, NA_WIDTH), BF16),
        scratch_shapes=[pltpu.VMEM((3 * NA_BLOCK_TOK, NA_WIDTH), BF16),
                        pltpu.VMEM((3 * NA_BLOCK_TOK, NA_WIDTH), BF16),
                        pltpu.VMEM((2, NA_HEADS // 2, 2 * GRID_W, NA_WINDOW_TOK), F32)],
        compiler_params=pltpu.CompilerParams(dimension_semantics=("parallel", "parallel"),
                                             vmem_limit_bytes=VMEM_LIMIT),
        name="na",
    )(q, k, k, k, v, v, v, bias)


def _na_bias_table(rpb):
    cols = np.arange(GRID_W)
    col_start = np.clip(cols - NA_COLS // 2, 0, GRID_W - NA_COLS)
    col_mask = (cols[None, :] >= col_start[:, None]) & (cols[None, :] < col_start[:, None] + NA_COLS)
    col_idx = np.clip(cols[None, :] - cols[:, None] + NA_COLS - 1, 0, RPB_COLS - 1)
    onehot = (col_idx[None] == np.arange(RPB_COLS)[:, None, None]).astype(np.float32)
    bias = jnp.einsum("lhrk,kqc->lhqrc", rpb.astype(F32) * LOG2E, onehot, precision=lax.Precision.HIGHEST)
    masked = jnp.where(col_mask[:, None, :], bias, NEG)
    depth = rpb.shape[0]
    slabs = [masked[:, :, :, o:o + NA_ROWS].reshape(depth, NA_HEADS // 2, 2 * GRID_W, NA_WINDOW_TOK)
             for o in range(NA_ROWS)]
    return jnp.stack(slabs, axis=1)


def _gla_chunk_rows(s):
    return slice(s * GLA_CHUNK, (s + 1) * GLA_CHUNK)


def _gla_head_cols(h):
    return slice(h * GLA_DK, (h + 1) * GLA_DK)


def _gla_cumulate(d, g, nsub):
    tri = jnp.where(d["mask"], 1.0, 0.0).astype(BF16)
    ghi, glo = _split_bf16(g)
    return [_dot(tri, ghi[_gla_chunk_rows(s)]) + _dot(tri, glo[_gla_chunk_rows(s)]) for s in range(nsub)]


def _gla_rescale(d, b_chunks):
    mid, end = d["mid"], d["end"]
    scaled = []
    for s, b in enumerate(b_chunks):
        rows = _gla_chunk_rows(s)
        b_mid = b[mid:mid + 1, :]
        b_end = b[end:end + 1, :]
        qe = d["q"][rows, :] * jnp.exp(jnp.minimum(b - b_mid, EXP_CLAMP)).astype(BF16)
        ke = d["k"][rows, :] * jnp.exp(jnp.minimum(b_mid - b, EXP_CLAMP)).astype(BF16)
        d["qi"][rows, :] = qe * jnp.exp(b_mid).astype(BF16)
        d["dec"][s] = jnp.broadcast_to(jnp.exp(b_end), (8, GLA_WIDTH))
        k_out = ke * jnp.exp(b_end - b_mid).astype(BF16)
        scaled.append((qe, ke, k_out))
    return scaled


def _gla_products(d, scaled):
    pairs = [(s, h) for s in range(len(scaled)) for h in range(GLA_HEADS)]
    v = lambda s, h: d["v"][_gla_chunk_rows(s), _gla_head_cols(h)]
    a = [lax.dot_general(scaled[s][0][:, _gla_head_cols(h)], scaled[s][1][:, _gla_head_cols(h)], _NT,
                         preferred_element_type=F32) for s, h in pairs]
    for s, h in pairs:
        d["kv"][s, h] = lax.dot_general(v(s, h), scaled[s][2][:, _gla_head_cols(h)], _TN,
                                        preferred_element_type=F32)
    for (s, h), a_sh in zip(pairs, a):
        d["oi"][_gla_chunk_rows(s), _gla_head_cols(h)] = _dot(jnp.where(d["mask"], a_sh, 0.0).astype(BF16), v(s, h))


def _gla_scan_step(d, s):
    rows = _gla_chunk_rows(s)
    dec = d["dec"][s]
    states = [d["st"][h] for h in range(GLA_HEADS)]
    inter = [lax.dot_general(d["qi"][rows, _gla_head_cols(h)], states[h].astype(BF16), _NT,
                             preferred_element_type=F32) for h in range(GLA_HEADS)]
    for h in range(GLA_HEADS):
        cols = _gla_head_cols(h)
        d["st"][h] = states[h] * dec[0:1, cols] + d["kv"][s, h]
        d["o"][rows, cols] = (d["oi"][rows, cols] + inter[h]).astype(BF16)


def _gla_kernel(qf_ref, kf_ref, vf_ref, gf_ref, qr_ref, kr_ref, vr_ref, gr_ref,
                of_ref, ob_ref, stf_ref, stb_ref, oif_s, oib_s, qif_s, qib_s, kvf_s, kvb_s, decf_s, decb_s, *, nsub):
    @pl.when(pl.program_id(1) == 0)
    def _():
        stf_ref[...] = jnp.zeros_like(stf_ref)
        stb_ref[...] = jnp.zeros_like(stb_ref)

    c = GLA_CHUNK
    ri = lax.broadcasted_iota(jnp.int32, (c, c), 0)
    ci = lax.broadcasted_iota(jnp.int32, (c, c), 1)
    fwd = dict(q=qf_ref, k=kf_ref, v=vf_ref, g=gf_ref, o=of_ref, st=stf_ref, oi=oif_s,
               qi=qif_s, kv=kvf_s, dec=decf_s, mask=ci <= ri, mid=c // 2 - 1, end=c - 1)
    bwd = dict(q=qr_ref, k=kr_ref, v=vr_ref, g=gr_ref, o=ob_ref, st=stb_ref, oi=oib_s,
               qi=qib_s, kv=kvb_s, dec=decb_s, mask=ci >= ri, mid=c // 2, end=0)
    b = [_gla_cumulate(d, d["g"][...], nsub) for d in (fwd, bwd)]
    scaled = [_gla_rescale(d, b_d) for d, b_d in zip((fwd, bwd), b)]
    for d, scaled_d in zip((fwd, bwd), scaled):
        _gla_products(d, scaled_d)
    for s in range(nsub):
        _gla_scan_step(fwd, s)
        _gla_scan_step(bwd, nsub - 1 - s)


def _gla(q, k, v, gf, gr, batch, n, cb):
    nb = n // cb
    nsub = cb // GLA_CHUNK
    fwd = lambda w: pl.BlockSpec((cb, w), lambda b, c: (b * nb + c, 0))
    bwd = lambda w: pl.BlockSpec((cb, w), lambda b, c: (b * nb + nb - 1 - c, 0))
    out = jax.ShapeDtypeStruct((batch * n, GLA_WIDTH), BF16)
    state = pltpu.VMEM((GLA_HEADS, GLA_DV, GLA_DK), F32)
    return pl.pallas_call(
        functools.partial(_gla_kernel, nsub=nsub),
        grid=(batch, nb),
        in_specs=[fwd(GLA_WIDTH)] * 4 + [bwd(GLA_WIDTH)] * 4,
        out_specs=[fwd(GLA_WIDTH), bwd(GLA_WIDTH)],
        out_shape=[out, out],
        scratch_shapes=[state, state,
                        pltpu.VMEM((cb, GLA_WIDTH), F32), pltpu.VMEM((cb, GLA_WIDTH), F32),
                        pltpu.VMEM((cb, GLA_WIDTH), BF16), pltpu.VMEM((cb, GLA_WIDTH), BF16),
                        pltpu.VMEM((nsub, GLA_HEADS, GLA_DV, GLA_DK), F32),
                        pltpu.VMEM((nsub, GLA_HEADS, GLA_DV, GLA_DK), F32),
                        pltpu.VMEM((nsub, 8, GLA_WIDTH), F32), pltpu.VMEM((nsub, 8, GLA_WIDTH), F32)],
        compiler_params=pltpu.CompilerParams(dimension_semantics=("parallel", "arbitrary"),
                                             vmem_limit_bytes=VMEM_LIMIT),
        name="gla",
    )(q, k, v, gf, q, k, v, gr)


def _route(logits):
    lane = lax.broadcasted_iota(jnp.int32, logits.shape, 1)
    lane_f = lane.astype(F32)
    ninf = -jnp.inf

    def first_argmax(vals, vmax):
        return jnp.min(jnp.where(vals == vmax, lane_f, float(LANES)), axis=-1, keepdims=True)

    lg = jnp.where(lane < N_GROUPS, logits, ninf)
    g_max = jnp.max(lg, axis=-1, keepdims=True)
    p_sel = 1.0 / jnp.sum(jnp.exp(lg - g_max), axis=-1, keepdims=True)
    grp = first_argmax(lg, g_max)
    e_lo = ROUTER_EXPERT_LANE0 + EXPERTS_PER_GROUP * grp
    le = jnp.where((lane_f >= e_lo) & (lane_f < e_lo + EXPERTS_PER_GROUP), logits, ninf)
    l1 = jnp.max(le, axis=-1, keepdims=True)
    i1 = first_argmax(le, l1)
    le2 = jnp.where(lane_f == i1, ninf, le)
    l2 = jnp.max(le2, axis=-1, keepdims=True)
    i2 = first_argmax(le2, l2)
    t = jnp.exp(l2 - l1)
    w1 = p_sel / (1.0 + t)
    w2 = w1 * t
    return lane_f == i1, lane_f == i2, w1, w2


def _local_sort(sel1, sel2, w1, w2):
    n = sel1.shape[0]
    ri = lax.broadcasted_iota(jnp.int32, (n, n), 0)
    ci = lax.broadcasted_iota(jnp.int32, (n, n), 1)
    before = jnp.where(ci < ri, 1.0, 0.0).astype(BF16)
    li = lax.broadcasted_iota(jnp.int32, (LANES, LANES), 0)
    lj = lax.broadcasted_iota(jnp.int32, (LANES, LANES), 1)
    lower_lanes = jnp.where(li < lj, 1.0, 0.0).astype(BF16)
    oh1 = jnp.where(sel1, 1.0, 0.0)
    oh2 = jnp.where(sel2, 1.0, 0.0)
    cnt1 = jnp.sum(oh1, axis=0, keepdims=True)
    cnt = cnt1 + jnp.sum(oh2, axis=0, keepdims=True)
    gran = jnp.floor((cnt + (MOE_GRAN - 1)) * (1.0 / MOE_GRAN))
    offg = _dot(jnp.broadcast_to(gran, (8, LANES)).astype(BF16), lower_lanes)[0:1]
    pick = lambda sel, vals: jnp.sum(jnp.where(sel, vals, 0.0), axis=-1, keepdims=True)
    pos1 = pick(sel1, _dot(before, oh1.astype(BF16)) + MOE_GRAN * offg)
    pos2 = pick(sel2, _dot(before, oh2.astype(BF16)) + cnt1 + MOE_GRAN * offg)
    lane = lax.broadcasted_iota(jnp.int32, (n, LANES), 1)
    route = jnp.where(lane == 0, pos1, jnp.where(lane == 1, pos2, jnp.where(lane == 2, w1,
                                                                             jnp.where(lane == 3, w2, 0.0))))
    to_off = jnp.where((li == lj + ROUTER_EXPERT_LANE0) & (lj < N_EXPERTS), 1.0, 0.0).astype(BF16)
    to_len = jnp.where(li == lj + ROUTER_EXPERT_LANE0 - N_EXPERTS, 1.0, 0.0).astype(BF16)
    meta = (_dot(jnp.broadcast_to(offg, (8, LANES)).astype(BF16), to_off)
            + _dot(jnp.broadcast_to(gran, (8, LANES)).astype(BF16), to_len))
    return route, meta.astype(jnp.int32)


def _post_kernel(x_ref, ya_ref, of_ref, ob_ref, rb_ref, ga_ref, gb_ref, woa_ref, wob_ref, wout_ref,
                 gn_ref, nf_ref, wr_ref, br_ref, x1_ref, xn_ref, route_ref, meta_ref):
    nsub = x_ref.shape[0] // MOE_SUB
    subs = [slice(u * MOE_SUB, (u + 1) * MOE_SUB) for u in range(nsub)]
    val = [dict() for _ in range(nsub)]

    def branches(u):
        rows = subs[u]
        o = of_ref[rows, :].astype(F32) + ob_ref[rows, :].astype(F32)
        parts = []
        for h in range(GLA_HEADS):
            oh = o[:, h * GLA_DV:(h + 1) * GLA_DV]
            ms = jnp.mean(oh * oh, axis=-1, keepdims=True)
            parts.append(oh * lax.rsqrt(ms + EPS) * gn_ref[...])
        on = (jnp.concatenate(parts, axis=-1) * rb_ref[rows, :].astype(F32)).astype(BF16)
        val[u]["ya"] = _dot(ya_ref[rows, :], woa_ref[...])
        val[u]["yb"] = _dot(on, wob_ref[...])

    def merge(u):
        rows = subs[u]
        y = ga_ref[rows, :] * val[u]["ya"].astype(BF16) + gb_ref[rows, :] * val[u]["yb"].astype(BF16)
        val[u]["x1"] = x_ref[rows, :] + _dot(y, wout_ref[...])

    def norm_logits(u):
        rows = subs[u]
        x1 = val[u]["x1"]
        x1_ref[rows, :] = x1
        ms = jnp.mean(x1 * x1, axis=-1, keepdims=True)
        thi, tlo = _split_bf16(x1 * lax.rsqrt(ms + EPS) * nf_ref[...])
        xn_ref[rows, :] = thi
        hi = _dot(thi, wr_ref[...])
        val[u]["logits"] = hi[:, :LANES] + hi[:, LANES:] + _dot(tlo, wr_ref[:, :LANES]) + br_ref[...]

    def route(u):
        route_ref[subs[u], :], meta_ref[u] = _local_sort(*_route(val[u]["logits"]))

    stages = [branches, merge, norm_logits, route]
    for step in range(nsub + len(stages) - 1):
        for u in range(nsub):
            if 0 <= step - u < len(stages):
                stages[step - u](u)


def _post(x, ya, of, ob, rb, ga, gb, l, woa, wob, wout, gn, nf, wr, br, tm):
    t = x.shape[0]
    row = lambda w: pl.BlockSpec((tm, w), lambda i: (i, 0))
    consts = [woa, wob, wout, gn, nf, wr, br]
    return pl.pallas_call(
        _post_kernel,
        grid=(t // tm,),
        in_specs=[row(D_MODEL), row(NA_WIDTH), row(GLA_WIDTH), row(GLA_WIDTH), row(GLA_WIDTH),
                  row(D_MODEL), row(D_MODEL)] + [_const_spec(c, l) for c in consts],
        out_specs=[row(D_MODEL), row(D_MODEL), row(LANES),
                   pl.BlockSpec((tm // MOE_SUB, 8, LANES), lambda i: (i, 0, 0))],
        out_shape=[jax.ShapeDtypeStruct((t, D_MODEL), F32), jax.ShapeDtypeStruct((t, D_MODEL), BF16),
                   jax.ShapeDtypeStruct((t, LANES), F32),
                   jax.ShapeDtypeStruct((t // MOE_SUB, 8, LANES), jnp.int32)],
        compiler_params=pltpu.CompilerParams(dimension_semantics=("parallel",), vmem_limit_bytes=VMEM_LIMIT),
        name="post",
    )(x, ya, of, ob, rb, ga, gb, *consts)


def _moe_kernel(meta_ref, xn_ref, route_ref, x1_ref, wg_ref, wu_ref, wd_ref, o_ref, loc, stage, *, nsub):
    i = pl.program_id(0)
    step = pl.program_id(1)
    stage_spare = stage.shape[0] // MOE_GRAN - 1
    sub_rows = lambda u: slice(u * MOE_SUB, (u + 1) * MOE_SUB)
    gran_rows = lambda g: pl.ds(pl.multiple_of(g * MOE_GRAN, MOE_GRAN), MOE_GRAN)

    @pl.when(step == 0)
    def _dispatch():
        stage[...] = jnp.zeros_like(stage)
        sel = lax.broadcasted_iota(jnp.int32, (8, LANES), 0) == lax.broadcasted_iota(jnp.int32, (8, LANES), 1)
        sel = jnp.where(sel, 1.0, 0.0).astype(BF16)
        row_id = lax.broadcasted_iota(jnp.int32, (MOE_LOC_ROWS, MOE_SUB), 0).astype(F32)
        for u in range(nsub):
            rhi, rlo = _split_bf16(route_ref[sub_rows(u), :])
            pos_t = (lax.dot_general(sel, rhi, _NT, preferred_element_type=F32)
                     + lax.dot_general(sel, rlo, _NT, preferred_element_type=F32))
            onehot = jnp.where(row_id == pos_t[0:1], 1.0, 0.0) + jnp.where(row_id == pos_t[1:2], 1.0, 0.0)
            loc[u, :MOE_LOC_ROWS, :] = _dot(onehot.astype(BF16), xn_ref[sub_rows(u), :]).astype(BF16)
            loc[u, MOE_LOC_ROWS:, :] = jnp.zeros((MOE_LOC_SLACK, D_MODEL), BF16)

    def move_granules(e, to_stage):
        meta = lambda u, k: meta_ref[((i * nsub + u) * 2 + k) * N_EXPERTS + e]

        def move(u, off, dst, g, live=None):
            if to_stage:
                to = dst + g if live is None else jnp.where(live, dst + g, stage_spare)
                stage[gran_rows(to), :] = loc[u, gran_rows(off + g), :]
            else:
                to = off + g if live is None else jnp.where(live, off + g, MOE_LOC_ROWS // MOE_GRAN)
                loc[u, gran_rows(to), :] = stage[gran_rows(dst + g), :]

        runs = []
        dst = 0
        for u in range(nsub):
            off, n = meta(u, 0), meta(u, 1)
            for g in range(MOE_FAST_GRAN):
                move(u, off, dst, g, live=g < n)
            runs.append((u, off, n, dst))
            dst = dst + n

        @pl.when(functools.reduce(jnp.maximum, [n for _, _, n, _ in runs]) > MOE_FAST_GRAN)
        def _long_runs():
            for u, off, n, dst_u in runs:
                lax.fori_loop(MOE_FAST_GRAN, n, lambda g, c, u=u, off=off, dst_u=dst_u: (move(u, off, dst_u, g), c)[1], 0)

        return dst

    @pl.when(step < MOE_EXPERT_STEPS)
    def _experts():
        for k in range(MOE_EXPERTS_PER_STEP):
            e = step * MOE_EXPERTS_PER_STEP + k
            total = move_granules(e, True)

            def block(b, carry, k=k):
                rows = pl.ds(pl.multiple_of(b * MOE_ROW_BLOCK, MOE_ROW_BLOCK), MOE_ROW_BLOCK)
                xs = stage[rows, :]
                hg = _dot(xs, wg_ref[k])
                h = (hg * _sigmoid(hg) * _dot(xs, wu_ref[k])).astype(BF16)
                stage[rows, :] = _dot(h, wd_ref[k]).astype(BF16)
                return carry

            lax.fori_loop(0, (total * MOE_GRAN + MOE_ROW_BLOCK - 1) // MOE_ROW_BLOCK, block, 0)
            move_granules(e, False)

    @pl.when(step >= MOE_EXPERT_STEPS)
    def _combine():
        col_id = lax.broadcasted_iota(jnp.int32, (MOE_SUB, MOE_LOC_ROWS), 1).astype(F32)
        per_slice = nsub // MOE_COMBINE_STEPS
        for j in range(per_slice):
            u = (step - MOE_EXPERT_STEPS) * per_slice + j
            r = route_ref[pl.ds(pl.multiple_of(u * MOE_SUB, MOE_SUB), MOE_SUB), :]
            pick = jnp.where(col_id == r[:, 0:1], r[:, 2:3], jnp.where(col_id == r[:, 1:2], r[:, 3:4], 0.0))
            o_ref[sub_rows(j), :] = x1_ref[sub_rows(j), :] + _dot(pick.astype(BF16), loc[u, :MOE_LOC_ROWS, :])


def _moe(meta, xn, route, x1, l, wg, wu, wd, tb):
    t = xn.shape[0]
    nsub = tb // MOE_SUB
    assert nsub % MOE_COMBINE_STEPS == 0
    row = lambda w: pl.BlockSpec((tb, w), lambda i, s, m: (i, 0))
    sliced = pl.BlockSpec((tb // MOE_COMBINE_STEPS, D_MODEL),
                          lambda i, s, m: (i * MOE_COMBINE_STEPS + jnp.maximum(s - MOE_EXPERT_STEPS, 0), 0))
    wspec = lambda a, b: pl.BlockSpec((None, MOE_EXPERTS_PER_STEP, a, b),
                                      lambda i, s, m: (l, jnp.minimum(s, MOE_EXPERT_STEPS - 1), 0, 0))
    return pl.pallas_call(
        functools.partial(_moe_kernel, nsub=nsub),
        grid_spec=pltpu.PrefetchScalarGridSpec(
            num_scalar_prefetch=1,
            grid=(t // tb, MOE_EXPERT_STEPS + MOE_COMBINE_STEPS),
            in_specs=[row(D_MODEL), row(LANES), sliced,
                      wspec(D_MODEL, D_EXPERT), wspec(D_MODEL, D_EXPERT), wspec(D_EXPERT, D_MODEL)],
            out_specs=sliced,
            scratch_shapes=[pltpu.VMEM((nsub, MOE_LOC_ROWS + MOE_LOC_SLACK, D_MODEL), BF16),
                            pltpu.VMEM((pl.cdiv(tb + MOE_LOC_SLACK, MOE_ROW_BLOCK) * MOE_ROW_BLOCK, D_MODEL), BF16)]),
        out_shape=jax.ShapeDtypeStruct((t, D_MODEL), F32),
        compiler_params=pltpu.CompilerParams(dimension_semantics=("parallel", "arbitrary"),
                                             vmem_limit_bytes=VMEM_LIMIT),
        name="moe",
    )(meta, xn, route, x1, wg, wu, wd)


def _prep_params(norm_mix, w_in, q_norm_a, k_norm_a, rpb, w_decay, b_decay, gla_norm, w_o_a, w_o_b, w_out,
                 norm_ffn, w_router_g, b_router_g, w_router_e, b_router_e, w_gate, w_up, w_down):
    depth = w_in.shape[0]
    f32 = lambda a: a.astype(F32)
    pad_lanes = lambda a, before, total: jnp.pad(f32(a), [(0, 0)] * (a.ndim - 1) + [(before, total - before - a.shape[-1])])
    w_aligned = jnp.concatenate([w_in[:, :, :IN_Z_COL0 + 2 * GLA_RANK],
                                 jnp.zeros((depth, D_MODEL, IN_GATE_COL0 - IN_Z_COL0 - 2 * GLA_RANK), w_in.dtype),
                                 w_in[:, :, IN_Z_COL0 + 2 * GLA_RANK:]], axis=-1).astype(BF16)
    w_router = pad_lanes(jnp.concatenate([w_router_g, w_router_e], axis=-1), ROUTER_GROUP_LANE0, LANES)
    b_router = pad_lanes(jnp.concatenate([b_router_g, b_router_e], axis=-1), ROUTER_GROUP_LANE0, LANES)
    w_router = jnp.concatenate(_split_bf16(w_router), axis=-1)
    wdec = jnp.stack([jnp.pad(f32(w_decay[:, d]), ((0, 0), (d * GLA_RANK, LANES - (d + 1) * GLA_RANK), (0, 0)))
                      for d in range(2)], axis=1).astype(BF16)
    return dict(
        in_proj=(f32(norm_mix)[:, None, :], w_aligned,
                 jnp.tile(f32(q_norm_a), (1, NA_HEADS))[:, None, :] * (NA_HEAD_DIM ** -0.5 * LOG2E),
                 jnp.tile(f32(k_norm_a), (1, NA_HEADS))[:, None, :], wdec, f32(b_decay)[:, :, None, :]),
        na=(_na_bias_table(rpb),),
        post=(w_o_a.astype(BF16), w_o_b.astype(BF16), w_out.astype(BF16), f32(gla_norm)[:, None, :],
              f32(norm_ffn)[:, None, :], w_router, b_router[:, None, :]),
        moe=(w_gate.astype(BF16), w_up.astype(BF16), w_down.astype(BF16)),
    )


def _tile(t, want):
    while t % want:
        want //= 2
    return want


def _trunk(x, p, depth):
    batch, n, _ = x.shape
    t = batch * n
    x = x.reshape(t, D_MODEL)
    for l in range(depth):
        qa, ka, va, qb, kb, vb, rb, gf, gr, ga, gb = _in_proj(x, l, *p["in_proj"], _tile(t, 512))
        ya = _na(qa, ka, va, l, *p["na"], batch, n)
        of, ob = _gla(qb, kb, vb, gf, gr, batch, n, GLA_BLOCK_TOK)
        x1, xn, route, meta = _post(x, ya, of, ob, rb, ga, gb, l, *p["post"], _tile(t, 1024))
        meta = meta[:, 0, :2 * N_EXPERTS].reshape(-1)
        x = _moe(meta, xn, route, x1, l, *p["moe"], _tile(t, MOE_TILE))
    return x.reshape(batch, n, D_MODEL)


def kernel(x_prompt, x_sample, norm_mix, w_in, q_norm_a, k_norm_a, rpb, w_decay, b_decay, gla_norm, w_o_a, w_o_b,
           w_out, norm_ffn, w_router_g, b_router_g, w_router_e, b_router_e, w_gate, w_up, w_down):
    p = _prep_params(norm_mix, w_in, q_norm_a, k_norm_a, rpb, w_decay, b_decay, gla_norm, w_o_a, w_o_b, w_out,
                     norm_ffn, w_router_g, b_router_g, w_router_e, b_router_e, w_gate, w_up, w_down)
    depth = w_in.shape[0]
    return (_trunk(x_prompt, p, depth), _trunk(x_sample, p, depth))
```

```python
import functools

import jax
import jax.numpy as jnp
import numpy as np
from jax import lax
from jax.experimental import pallas as pl
from jax.experimental.pallas import tpu as pltpu

F32 = jnp.float32
BF16 = jnp.bfloat16

D_MODEL = 1024
GRID_W = 64
NA_HEADS = 8
NA_HEAD_DIM = 64
NA_WIDTH = NA_HEADS * NA_HEAD_DIM
NA_ROWS = 8
NA_COLS = 16
RPB_ROWS = 2 * NA_ROWS - 1
RPB_COLS = 2 * NA_COLS - 1
GLA_HEADS = 4
GLA_DK = 128
GLA_DV = 128
GLA_WIDTH = GLA_HEADS * GLA_DV
GLA_RANK = 16
GLA_GATE_NORM = 16.0
GLA_CHUNK = 64
GLA_BLOCK_TOK = 1024
N_GROUPS = 4
EXPERTS_PER_GROUP = 4
N_EXPERTS = N_GROUPS * EXPERTS_PER_GROUP
D_EXPERT = 512
EPS = 1e-6
NEG = -1e30

LANES = 128
IN_Z_COL0 = 7 * 512
IN_GATE_COL0 = IN_Z_COL0 + LANES
MOE_SUB = 256
MOE_GRAN = 16
MOE_LOC_ROWS = 768
MOE_TILE = 2048
MOE_FAST_GRAN = 4
MOE_LOC_SLACK = (MOE_FAST_GRAN + 1) * MOE_GRAN
MOE_EXPERTS_PER_STEP = 2
MOE_EXPERT_STEPS = N_EXPERTS // MOE_EXPERTS_PER_STEP
MOE_COMBINE_STEPS = 2
MOE_ROW_BLOCK = 384
LOG2E = 1.4426950408889634
EXP_CLAMP = 80.0
ROUTER_GROUP_LANE0 = 0
ROUTER_EXPERT_LANE0 = N_GROUPS
VMEM_LIMIT = 56 * 1024 * 1024

_NT = (((1,), (1,)), ((), ()))
_TN = (((0,), (0,)), ((), ()))


def _dot(a, b):
    return jnp.dot(a, b, preferred_element_type=F32)


def _split_bf16(x):
    hi = x.astype(BF16)
    lo = (x - hi.astype(F32)).astype(BF16)
    return hi, lo


def _sigmoid(x):
    return 1.0 / (1.0 + jnp.exp(-x))


def _const_spec(stacked, l):
    tail = stacked.shape[1:]
    return pl.BlockSpec((None,) + tail, lambda *_: (l,) + (0,) * len(tail), pipeline_mode=pl.Buffered(1))


def _in_proj_kernel(x_ref, nrm_ref, w_ref, qg_ref, kg_ref, wdec_ref, bdec_ref,
                    qa_ref, ka_ref, va_ref, qb_ref, kb_ref, vb_ref, rb_ref, gf_ref, gr_ref, ga_ref, gb_ref):
    x = x_ref[...]
    ms = jnp.mean(x * x, axis=-1, keepdims=True)
    xn = (x * lax.rsqrt(ms + EPS) * nrm_ref[...]).astype(BF16)
    low_half = lax.broadcasted_iota(jnp.int32, (x.shape[0], LANES), 1) < NA_HEAD_DIM

    def proj(lo, width=512):
        return _dot(xn, w_ref[:, lo:lo + width])

    def head_norm(a, gain_ref):
        parts = []
        for p in range(NA_WIDTH // LANES):
            sq = a[:, p * LANES:(p + 1) * LANES]
            sq = sq * sq
            s_lo = jnp.sum(jnp.where(low_half, sq, 0.0), axis=-1, keepdims=True)
            s_hi = jnp.sum(jnp.where(low_half, 0.0, sq), axis=-1, keepdims=True)
            parts.append(jnp.where(low_half, s_lo, s_hi))
        ss = jnp.concatenate(parts, axis=-1)
        return a * lax.rsqrt(ss * (1.0 / NA_HEAD_DIM) + EPS) * gain_ref[...]

    z = proj(IN_Z_COL0, LANES).astype(BF16)
    ga_ref[...] = _sigmoid(proj(IN_GATE_COL0, D_MODEL)).astype(BF16)
    gb_ref[...] = _sigmoid(proj(IN_GATE_COL0 + D_MODEL, D_MODEL)).astype(BF16)
    for d, g_ref in enumerate((gf_ref, gr_ref)):
        lg = _dot(z, wdec_ref[d]) + bdec_ref[d]
        g_ref[...] = (jnp.minimum(lg, 0.0) - jnp.log(1.0 + jnp.exp(-jnp.abs(lg)))) * (1.0 / GLA_GATE_NORM)
    r = proj(3072)
    rb_ref[...] = (r * _sigmoid(r)).astype(BF16)
    qa = proj(0)
    ka = proj(512)
    qb_ref[...] = (proj(1536) * (GLA_DK ** -0.5)).astype(BF16)
    qa_ref[...] = head_norm(qa, qg_ref).astype(BF16)
    kb_ref[...] = proj(2048).astype(BF16)
    ka_ref[...] = head_norm(ka, kg_ref).astype(BF16)
    va_ref[...] = proj(1024).astype(BF16)
    vb_ref[...] = proj(2560).astype(BF16)


def _in_proj(x, l, nrm, w, qg, kg, wdec, bdec, tm):
    t = x.shape[0]
    row = lambda w: pl.BlockSpec((tm, w), lambda i: (i, 0))
    out_w = [512] * 7 + [GLA_WIDTH, GLA_WIDTH, D_MODEL, D_MODEL]
    out_dt = [BF16] * 7 + [F32, F32, BF16, BF16]
    consts = [nrm, w, qg, kg, wdec, bdec]
    return pl.pallas_call(
        _in_proj_kernel,
        grid=(t // tm,),
        in_specs=[row(D_MODEL)] + [_const_spec(c, l) for c in consts],
        out_specs=[row(w) for w in out_w],
        out_shape=[jax.ShapeDtypeStruct((t, w), dt) for w, dt in zip(out_w, out_dt)],
        compiler_params=pltpu.CompilerParams(dimension_semantics=("parallel",), vmem_limit_bytes=VMEM_LIMIT),
        name="in_proj",
    )(x, *consts)


NA_BLOCK_ROWS = 8
NA_BLOCK_TOK = NA_BLOCK_ROWS * GRID_W
NA_WINDOW_TOK = NA_ROWS * GRID_W


def _na_kernel(q_ref, kp_ref, kc_ref, kn_ref, vp_ref, vc_ref, vn_ref, bias_ref, o_ref, kwin, vwin, s_scr, *, rows):
    i = pl.program_id(1)
    kwin[0:NA_BLOCK_TOK] = kp_ref[...]
    kwin[NA_BLOCK_TOK:2 * NA_BLOCK_TOK] = kc_ref[...]
    kwin[2 * NA_BLOCK_TOK:3 * NA_BLOCK_TOK] = kn_ref[...]
    vwin[0:NA_BLOCK_TOK] = vp_ref[...]
    vwin[NA_BLOCK_TOK:2 * NA_BLOCK_TOK] = vc_ref[...]
    vwin[2 * NA_BLOCK_TOK:3 * NA_BLOCK_TOK] = vn_ref[...]
    low_half = lax.broadcasted_iota(jnp.int32, (GRID_W, LANES), 1) < NA_HEAD_DIM
    pairs = range(NA_HEADS // 2)
    cols = lambda p: slice(p * LANES, (p + 1) * LANES)
    qrows = lambda j: slice(j * GRID_W, (j + 1) * GRID_W)

    def window(j):
        r = i * NA_BLOCK_ROWS + j
        rs = jnp.clip(r - NA_ROWS // 2, 0, rows - NA_ROWS)
        start = pl.multiple_of((rs - i * NA_BLOCK_ROWS + NA_BLOCK_ROWS) * GRID_W, GRID_W)
        return start, rs - r + NA_ROWS - 1

    def scores(j):
        start, bias_off = window(j)
        for p in pairs:
            qp = q_ref[qrows(j), cols(p)]
            zero = jnp.zeros_like(qp)
            q2 = jnp.concatenate([jnp.where(low_half, qp, zero), jnp.where(low_half, zero, qp)], axis=0)
            s = lax.dot_general(q2, kwin[pl.ds(start, NA_WINDOW_TOK), cols(p)], _NT, preferred_element_type=F32)
            s_scr[j % 2, p] = s + bias_ref[bias_off, p]

    def attend(j):
        start, _ = window(j)
        for p in pairs:
            s = s_scr[j % 2, p]
            e = jnp.exp2(s - jnp.max(s, axis=-1, keepdims=True))
            l = jnp.sum(e, axis=-1, keepdims=True)
            pv = _dot(e.astype(BF16), vwin[pl.ds(start, NA_WINDOW_TOK), cols(p)]) * (1.0 / l)
            o_ref[qrows(j), cols(p)] = jnp.where(low_half, pv[:GRID_W], pv[GRID_W:]).astype(BF16)

    scores(0)
    for j in range(NA_BLOCK_ROWS):
        if j + 1 < NA_BLOCK_ROWS:
            scores(j + 1)
        attend(j)


def _na(q, k, v, l, bias, batch, n):
    rows = n // GRID_W
    nblk = rows // NA_BLOCK_ROWS
    spec = lambda f: pl.BlockSpec((NA_BLOCK_TOK, NA_WIDTH), lambda b, i: (b * nblk + f(i), 0))
    cur = spec(lambda i: i)
    prev = spec(lambda i: jnp.maximum(i - 1, 0))
    nxt = spec(lambda i: jnp.minimum(i + 1, nblk - 1))
    return pl.pallas_call(
        functools.partial(_na_kernel, rows=rows),
        grid=(batch, nblk),
        in_specs=[cur, prev, cur, nxt, prev, cur, nxt, _const_spec(bias, l)],
        out_specs=cur,
        out_shape=jax.ShapeDtypeStruct((batch * n, NA_WIDTH), BF16),
        scratch_shapes=[pltpu.VMEM((3 * NA_BLOCK_TOK, NA_WIDTH), BF16),
                        pltpu.VMEM((3 * NA_BLOCK_TOK, NA_WIDTH), BF16),
                        pltpu.VMEM((2, NA_HEADS // 2, 2 * GRID_W, NA_WINDOW_TOK), F32)],
        compiler_params=pltpu.CompilerParams(dimension_semantics=("parallel", "parallel"),
                                             vmem_limit_bytes=VMEM_LIMIT),
        name="na",
    )(q, k, k, k, v, v, v, bias)


def _na_bias_table(rpb):
    cols = np.arange(GRID_W)
    col_start = np.clip(cols - NA_COLS // 2, 0, GRID_W - NA_COLS)
    col_mask = (cols[None, :] >= col_start[:, None]) & (cols[None, :] < col_start[:, None] + NA_COLS)
    col_idx = np.clip(cols[None, :] - cols[:, None] + NA_COLS - 1, 0, RPB_COLS - 1)
    onehot = (col_idx[None] == np.arange(RPB_COLS)[:, None, None]).astype(np.float32)
    bias = jnp.einsum("lhrk,kqc->lhqrc", rpb.astype(F32) * LOG2E, onehot, precision=lax.Precision.HIGHEST)
    masked = jnp.where(col_mask[:, None, :], bias, NEG)
    depth = rpb.shape[0]
    slabs = [masked[:, :, :, o:o + NA_ROWS].reshape(depth, NA_HEADS // 2, 2 * GRID_W, NA_WINDOW_TOK)
             for o in range(NA_ROWS)]
    return jnp.stack(slabs, axis=1)


def _gla_chunk_rows(s):
    return slice(s * GLA_CHUNK, (s + 1) * GLA_CHUNK)


def _gla_head_cols(h):
    return slice(h * GLA_DK, (h + 1) * GLA_DK)


def _gla_cumulate(d, g, nsub):
    tri = jnp.where(d["mask"], 1.0, 0.0).astype(BF16)
    ghi, glo = _split_bf16(g)
    return [_dot(tri, ghi[_gla_chunk_rows(s)]) + _dot(tri, glo[_gla_chunk_rows(s)]) for s in range(nsub)]


def _gla_rescale(d, b_chunks):
    mid, end = d["mid"], d["end"]
    scaled = []
    for s, b in enumerate(b_chunks):
        rows = _gla_chunk_rows(s)
        b_mid = b[mid:mid + 1, :]
        b_end = b[end:end + 1, :]
        qe = d["q"][rows, :] * jnp.exp(jnp.minimum(b - b_mid, EXP_CLAMP)).astype(BF16)
        ke = d["k"][rows, :] * jnp.exp(jnp.minimum(b_mid - b, EXP_CLAMP)).astype(BF16)
        d["qi"][rows, :] = qe * jnp.exp(b_mid).astype(BF16)
        d["dec"][s] = jnp.broadcast_to(jnp.exp(b_end), (8, GLA_WIDTH))
        k_out = ke * jnp.exp(b_end - b_mid).astype(BF16)
        scaled.append((qe, ke, k_out))
    return scaled


def _gla_products(d, scaled):
    pairs = [(s, h) for s in range(len(scaled)) for h in range(GLA_HEADS)]
    v = lambda s, h: d["v"][_gla_chunk_rows(s), _gla_head_cols(h)]
    a = [lax.dot_general(scaled[s][0][:, _gla_head_cols(h)], scaled[s][1][:, _gla_head_cols(h)], _NT,
                         preferred_element_type=F32) for s, h in pairs]
    for s, h in pairs:
        d["kv"][s, h] = lax.dot_general(v(s, h), scaled[s][2][:, _gla_head_cols(h)], _TN,
                                        preferred_element_type=F32)
    for (s, h), a_sh in zip(pairs, a):
        d["oi"][_gla_chunk_rows(s), _gla_head_cols(h)] = _dot(jnp.where(d["mask"], a_sh, 0.0).astype(BF16), v(s, h))


def _gla_scan_step(d, s):
    rows = _gla_chunk_rows(s)
    dec = d["dec"][s]
    states = [d["st"][h] for h in range(GLA_HEADS)]
    inter = [lax.dot_general(d["qi"][rows, _gla_head_cols(h)], states[h].astype(BF16), _NT,
                             preferred_element_type=F32) for h in range(GLA_HEADS)]
    for h in range(GLA_HEADS):
        cols = _gla_head_cols(h)
        d["st"][h] = states[h] * dec[0:1, cols] + d["kv"][s, h]
        d["o"][rows, cols] = (d["oi"][rows, cols] + inter[h]).astype(BF16)


def _gla_kernel(qf_ref, kf_ref, vf_ref, gf_ref, qr_ref, kr_ref, vr_ref, gr_ref,
                of_ref, ob_ref, stf_ref, stb_ref, oif_s, oib_s, qif_s, qib_s, kvf_s, kvb_s, decf_s, decb_s, *, nsub):
    @pl.when(pl.program_id(1) == 0)
    def _():
        stf_ref[...] = jnp.zeros_like(stf_ref)
        stb_ref[...] = jnp.zeros_like(stb_ref)

    c = GLA_CHUNK
    ri = lax.broadcasted_iota(jnp.int32, (c, c), 0)
    ci = lax.broadcasted_iota(jnp.int32, (c, c), 1)
    fwd = dict(q=qf_ref, k=kf_ref, v=vf_ref, g=gf_ref, o=of_ref, st=stf_ref, oi=oif_s,
               qi=qif_s, kv=kvf_s, dec=decf_s, mask=ci <= ri, mid=c // 2 - 1, end=c - 1)
    bwd = dict(q=qr_ref, k=kr_ref, v=vr_ref, g=gr_ref, o=ob_ref, st=stb_ref, oi=oib_s,
               qi=qib_s, kv=kvb_s, dec=decb_s, mask=ci >= ri, mid=c // 2, end=0)
    b = [_gla_cumulate(d, d["g"][...], nsub) for d in (fwd, bwd)]
    scaled = [_gla_rescale(d, b_d) for d, b_d in zip((fwd, bwd), b)]
    for d, scaled_d in zip((fwd, bwd), scaled):
        _gla_products(d, scaled_d)
    for s in range(nsub):
        _gla_scan_step(fwd, s)
        _gla_scan_step(bwd, nsub - 1 - s)


def _gla(q, k, v, gf, gr, batch, n, cb):
    nb = n // cb
    nsub = cb // GLA_CHUNK
    fwd = lambda w: pl.BlockSpec((cb, w), lambda b, c: (b * nb + c, 0))
    bwd = lambda w: pl.BlockSpec((cb, w), lambda b, c: (b * nb + nb - 1 - c, 0))
    out = jax.ShapeDtypeStruct((batch * n, GLA_WIDTH), BF16)
    state = pltpu.VMEM((GLA_HEADS, GLA_DV, GLA_DK), F32)
    return pl.pallas_call(
        functools.partial(_gla_kernel, nsub=nsub),
        grid=(batch, nb),
        in_specs=[fwd(GLA_WIDTH)] * 4 + [bwd(GLA_WIDTH)] * 4,
        out_specs=[fwd(GLA_WIDTH), bwd(GLA_WIDTH)],
        out_shape=[out, out],
        scratch_shapes=[state, state,
                        pltpu.VMEM((cb, GLA_WIDTH), F32), pltpu.VMEM((cb, GLA_WIDTH), F32),
                        pltpu.VMEM((cb, GLA_WIDTH), BF16), pltpu.VMEM((cb, GLA_WIDTH), BF16),
                        pltpu.VMEM((nsub, GLA_HEADS, GLA_DV, GLA_DK), F32),
                        pltpu.VMEM((nsub, GLA_HEADS, GLA_DV, GLA_DK), F32),
                        pltpu.VMEM((nsub, 8, GLA_WIDTH), F32), pltpu.VMEM((nsub, 8, GLA_WIDTH), F32)],
        compiler_params=pltpu.CompilerParams(dimension_semantics=("parallel", "arbitrary"),
                                             vmem_limit_bytes=VMEM_LIMIT),
        name="gla",
    )(q, k, v, gf, q, k, v, gr)


def _route(logits):
    lane = lax.broadcasted_iota(jnp.int32, logits.shape, 1)
    lane_f = lane.astype(F32)
    ninf = -jnp.inf

    def first_argmax(vals, vmax):
        return jnp.min(jnp.where(vals == vmax, lane_f, float(LANES)), axis=-1, keepdims=True)

    lg = jnp.where(lane < N_GROUPS, logits, ninf)
    g_max = jnp.max(lg, axis=-1, keepdims=True)
    p_sel = 1.0 / jnp.sum(jnp.exp(lg - g_max), axis=-1, keepdims=True)
    grp = first_argmax(lg, g_max)
    e_lo = ROUTER_EXPERT_LANE0 + EXPERTS_PER_GROUP * grp
    le = jnp.where((lane_f >= e_lo) & (lane_f < e_lo + EXPERTS_PER_GROUP), logits, ninf)
    l1 = jnp.max(le, axis=-1, keepdims=True)
    i1 = first_argmax(le, l1)
    le2 = jnp.where(lane_f == i1, ninf, le)
    l2 = jnp.max(le2, axis=-1, keepdims=True)
    i2 = first_argmax(le2, l2)
    t = jnp.exp(l2 - l1)
    w1 = p_sel / (1.0 + t)
    w2 = w1 * t
    return lane_f == i1, lane_f == i2, w1, w2


def _local_sort(sel1, sel2, w1, w2):
    n = sel1.shape[0]
    ri = lax.broadcasted_iota(jnp.int32, (n, n), 0)
    ci = lax.broadcasted_iota(jnp.int32, (n, n), 1)
    before = jnp.where(ci < ri, 1.0, 0.0).astype(BF16)
    li = lax.broadcasted_iota(jnp.int32, (LANES, LANES), 0)
    lj = lax.broadcasted_iota(jnp.int32, (LANES, LANES), 1)
    lower_lanes = jnp.where(li < lj, 1.0, 0.0).astype(BF16)
    oh1 = jnp.where(sel1, 1.0, 0.0)
    oh2 = jnp.where(sel2, 1.0, 0.0)
    cnt1 = jnp.sum(oh1, axis=0, keepdims=True)
    cnt = cnt1 + jnp.sum(oh2, axis=0, keepdims=True)
    gran = jnp.floor((cnt + (MOE_GRAN - 1)) * (1.0 / MOE_GRAN))
    offg = _dot(jnp.broadcast_to(gran, (8, LANES)).astype(BF16), lower_lanes)[0:1]
    pick = lambda sel, vals: jnp.sum(jnp.where(sel, vals, 0.0), axis=-1, keepdims=True)
    pos1 = pick(sel1, _dot(before, oh1.astype(BF16)) + MOE_GRAN * offg)
    pos2 = pick(sel2, _dot(before, oh2.astype(BF16)) + cnt1 + MOE_GRAN * offg)
    lane = lax.broadcasted_iota(jnp.int32, (n, LANES), 1)
    route = jnp.where(lane == 0, pos1, jnp.where(lane == 1, pos2, jnp.where(lane == 2, w1,
                                                                             jnp.where(lane == 3, w2, 0.0))))
    to_off = jnp.where((li == lj + ROUTER_EXPERT_LANE0) & (lj < N_EXPERTS), 1.0, 0.0).astype(BF16)
    to_len = jnp.where(li == lj + ROUTER_EXPERT_LANE0 - N_EXPERTS, 1.0, 0.0).astype(BF16)
    meta = (_dot(jnp.broadcast_to(offg, (8, LANES)).astype(BF16), to_off)
            + _dot(jnp.broadcast_to(gran, (8, LANES)).astype(BF16), to_len))
    return route, meta.astype(jnp.int32)


def _post_kernel(x_ref, ya_ref, of_ref, ob_ref, rb_ref, ga_ref, gb_ref, woa_ref, wob_ref, wout_ref,
                 gn_ref, nf_ref, wr_ref, br_ref, x1_ref, xn_ref, route_ref, meta_ref):
    nsub = x_ref.shape[0] // MOE_SUB
    subs = [slice(u * MOE_SUB, (u + 1) * MOE_SUB) for u in range(nsub)]
    val = [dict() for _ in range(nsub)]

    def branches(u):
        rows = subs[u]
        o = of_ref[rows, :].astype(F32) + ob_ref[rows, :].astype(F32)
        parts = []
        for h in range(GLA_HEADS):
            oh = o[:, h * GLA_DV:(h + 1) * GLA_DV]
            ms = jnp.mean(oh * oh, axis=-1, keepdims=True)
            parts.append(oh * lax.rsqrt(ms + EPS) * gn_ref[...])
        on = (jnp.concatenate(parts, axis=-1) * rb_ref[rows, :].astype(F32)).astype(BF16)
        val[u]["ya"] = _dot(ya_ref[rows, :], woa_ref[...])
        val[u]["yb"] = _dot(on, wob_ref[...])

    def merge(u):
        rows = subs[u]
        y = ga_ref[rows, :] * val[u]["ya"].astype(BF16) + gb_ref[rows, :] * val[u]["yb"].astype(BF16)
        val[u]["x1"] = x_ref[rows, :] + _dot(y, wout_ref[...])

    def norm_logits(u):
        rows = subs[u]
        x1 = val[u]["x1"]
        x1_ref[rows, :] = x1
        ms = jnp.mean(x1 * x1, axis=-1, keepdims=True)
        thi, tlo = _split_bf16(x1 * lax.rsqrt(ms + EPS) * nf_ref[...])
        xn_ref[rows, :] = thi
        hi = _dot(thi, wr_ref[...])
        val[u]["logits"] = hi[:, :LANES] + hi[:, LANES:] + _dot(tlo, wr_ref[:, :LANES]) + br_ref[...]

    def route(u):
        route_ref[subs[u], :], meta_ref[u] = _local_sort(*_route(val[u]["logits"]))

    stages = [branches, merge, norm_logits, route]
    for step in range(nsub + len(stages) - 1):
        for u in range(nsub):
            if 0 <= step - u < len(stages):
                stages[step - u](u)


def _post(x, ya, of, ob, rb, ga, gb, l, woa, wob, wout, gn, nf, wr, br, tm):
    t = x.shape[0]
    row = lambda w: pl.BlockSpec((tm, w), lambda i: (i, 0))
    consts = [woa, wob, wout, gn, nf, wr, br]
    return pl.pallas_call(
        _post_kernel,
        grid=(t // tm,),
        in_specs=[row(D_MODEL), row(NA_WIDTH), row(GLA_WIDTH), row(GLA_WIDTH), row(GLA_WIDTH),
                  row(D_MODEL), row(D_MODEL)] + [_const_spec(c, l) for c in consts],
        out_specs=[row(D_MODEL), row(D_MODEL), row(LANES),
                   pl.BlockSpec((tm // MOE_SUB, 8, LANES), lambda i: (i, 0, 0))],
        out_shape=[jax.ShapeDtypeStruct((t, D_MODEL), F32), jax.ShapeDtypeStruct((t, D_MODEL), BF16),
                   jax.ShapeDtypeStruct((t, LANES), F32),
                   jax.ShapeDtypeStruct((t // MOE_SUB, 8, LANES), jnp.int32)],
        compiler_params=pltpu.CompilerParams(dimension_semantics=("parallel",), vmem_limit_bytes=VMEM_LIMIT),
        name="post",
    )(x, ya, of, ob, rb, ga, gb, *consts)


def _moe_kernel(meta_ref, xn_ref, route_ref, x1_ref, wg_ref, wu_ref, wd_ref, o_ref, loc, stage, *, nsub):
    i = pl.program_id(0)
    step = pl.program_id(1)
    stage_spare = stage.shape[0] // MOE_GRAN - 1
    sub_rows = lambda u: slice(u * MOE_SUB, (u + 1) * MOE_SUB)
    gran_rows = lambda g: pl.ds(pl.multiple_of(g * MOE_GRAN, MOE_GRAN), MOE_GRAN)

    @pl.when(step == 0)
    def _dispatch():
        stage[...] = jnp.zeros_like(stage)
        sel = lax.broadcasted_iota(jnp.int32, (8, LANES), 0) == lax.broadcasted_iota(jnp.int32, (8, LANES), 1)
        sel = jnp.where(sel, 1.0, 0.0).astype(BF16)
        row_id = lax.broadcasted_iota(jnp.int32, (MOE_LOC_ROWS, MOE_SUB), 0).astype(F32)
        for u in range(nsub):
            rhi, rlo = _split_bf16(route_ref[sub_rows(u), :])
            pos_t = (lax.dot_general(sel, rhi, _NT, preferred_element_type=F32)
                     + lax.dot_general(sel, rlo, _NT, preferred_element_type=F32))
            onehot = jnp.where(row_id == pos_t[0:1], 1.0, 0.0) + jnp.where(row_id == pos_t[1:2], 1.0, 0.0)
            loc[u, :MOE_LOC_ROWS, :] = _dot(onehot.astype(BF16), xn_ref[sub_rows(u), :]).astype(BF16)
            loc[u, MOE_LOC_ROWS:, :] = jnp.zeros((MOE_LOC_SLACK, D_MODEL), BF16)

    def move_granules(e, to_stage):
        meta = lambda u, k: meta_ref[((i * nsub + u) * 2 + k) * N_EXPERTS + e]

        def move(u, off, dst, g, live=None):
            if to_stage:
                to = dst + g if live is None else jnp.where(live, dst + g, stage_spare)
                stage[gran_rows(to), :] = loc[u, gran_rows(off + g), :]
            else:
                to = off + g if live is None else jnp.where(live, off + g, MOE_LOC_ROWS // MOE_GRAN)
                loc[u, gran_rows(to), :] = stage[gran_rows(dst + g), :]

        runs = []
        dst = 0
        for u in range(nsub):
            off, n = meta(u, 0), meta(u, 1)
            for g in range(MOE_FAST_GRAN):
                move(u, off, dst, g, live=g < n)
            runs.append((u, off, n, dst))
            dst = dst + n

        @pl.when(functools.reduce(jnp.maximum, [n for _, _, n, _ in runs]) > MOE_FAST_GRAN)
        def _long_runs():
            for u, off, n, dst_u in runs:
                lax.fori_loop(MOE_FAST_GRAN, n, lambda g, c, u=u, off=off, dst_u=dst_u: (move(u, off, dst_u, g), c)[1], 0)

        return dst

    @pl.when(step < MOE_EXPERT_STEPS)
    def _experts():
        for k in range(MOE_EXPERTS_PER_STEP):
            e = step * MOE_EXPERTS_PER_STEP + k
            total = move_granules(e, True)

            def block(b, carry, k=k):
                rows = pl.ds(pl.multiple_of(b * MOE_ROW_BLOCK, MOE_ROW_BLOCK), MOE_ROW_BLOCK)
                xs = stage[rows, :]
                hg = _dot(xs, wg_ref[k])
                h = (hg * _sigmoid(hg) * _dot(xs, wu_ref[k])).astype(BF16)
                stage[rows, :] = _dot(h, wd_ref[k]).astype(BF16)
                return carry

            lax.fori_loop(0, (total * MOE_GRAN + MOE_ROW_BLOCK - 1) // MOE_ROW_BLOCK, block, 0)
            move_granules(e, False)

    @pl.when(step >= MOE_EXPERT_STEPS)
    def _combine():
        col_id = lax.broadcasted_iota(jnp.int32, (MOE_SUB, MOE_LOC_ROWS), 1).astype(F32)
        per_slice = nsub // MOE_COMBINE_STEPS
        for j in range(per_slice):
            u = (step - MOE_EXPERT_STEPS) * per_slice + j
            r = route_ref[pl.ds(pl.multiple_of(u * MOE_SUB, MOE_SUB), MOE_SUB), :]
            pick = jnp.where(col_id == r[:, 0:1], r[:, 2:3], jnp.where(col_id == r[:, 1:2], r[:, 3:4], 0.0))
            o_ref[sub_rows(j), :] = x1_ref[sub_rows(j), :] + _dot(pick.astype(BF16), loc[u, :MOE_LOC_ROWS, :])


def _moe(meta, xn, route, x1, l, wg, wu, wd, tb):
    t = xn.shape[0]
    nsub = tb // MOE_SUB
    assert nsub % MOE_COMBINE_STEPS == 0
    row = lambda w, **kw: pl.BlockSpec((tb, w), lambda i, s, m: (i, 0), **kw)
    sliced = pl.BlockSpec((tb // MOE_COMBINE_STEPS, D_MODEL),
                          lambda i, s, m: (i * MOE_COMBINE_STEPS + jnp.maximum(s - MOE_EXPERT_STEPS, 0), 0))
    wspec = lambda a, b: pl.BlockSpec((None, MOE_EXPERTS_PER_STEP, a, b),
                                      lambda i, s, m: (l, jnp.minimum(s, MOE_EXPERT_STEPS - 1), 0, 0))
    return pl.pallas_call(
        functools.partial(_moe_kernel, nsub=nsub),
        grid_spec=pltpu.PrefetchScalarGridSpec(
            num_scalar_prefetch=1,
            grid=(t // tb, MOE_EXPERT_STEPS + MOE_COMBINE_STEPS),
            in_specs=[row(D_MODEL, pipeline_mode=pl.Buffered(1)), row(LANES), sliced,
                      wspec(D_MODEL, D_EXPERT), wspec(D_MODEL, D_EXPERT), wspec(D_EXPERT, D_MODEL)],
            out_specs=sliced,
            scratch_shapes=[pltpu.VMEM((nsub, MOE_LOC_ROWS + MOE_LOC_SLACK, D_MODEL), BF16),
                            pltpu.VMEM((pl.cdiv(tb + MOE_LOC_SLACK, MOE_ROW_BLOCK) * MOE_ROW_BLOCK, D_MODEL), BF16)]),
        out_shape=jax.ShapeDtypeStruct((t, D_MODEL), F32),
        compiler_params=pltpu.CompilerParams(dimension_semantics=("parallel", "arbitrary"),
                                             vmem_limit_bytes=VMEM_LIMIT),
        name="moe",
    )(meta, xn, route, x1, wg, wu, wd)


def _prep_params(norm_mix, w_in, q_norm_a, k_norm_a, rpb, w_decay, b_decay, gla_norm, w_o_a, w_o_b, w_out,
                 norm_ffn, w_router_g, b_router_g, w_router_e, b_router_e, w_gate, w_up, w_down):
    depth = w_in.shape[0]
    f32 = lambda a: a.astype(F32)
    pad_lanes = lambda a, before, total: jnp.pad(f32(a), [(0, 0)] * (a.ndim - 1) + [(before, total - before - a.shape[-1])])
    w_aligned = jnp.concatenate([w_in[:, :, :IN_Z_COL0 + 2 * GLA_RANK],
                                 jnp.zeros((depth, D_MODEL, IN_GATE_COL0 - IN_Z_COL0 - 2 * GLA_RANK), w_in.dtype),
                                 w_in[:, :, IN_Z_COL0 + 2 * GLA_RANK:]], axis=-1).astype(BF16)
    w_router = pad_lanes(jnp.concatenate([w_router_g, w_router_e], axis=-1), ROUTER_GROUP_LANE0, LANES)
    b_router = pad_lanes(jnp.concatenate([b_router_g, b_router_e], axis=-1), ROUTER_GROUP_LANE0, LANES)
    w_router = jnp.concatenate(_split_bf16(w_router), axis=-1)
    wdec = jnp.stack([jnp.pad(f32(w_decay[:, d]), ((0, 0), (d * GLA_RANK, LANES - (d + 1) * GLA_RANK), (0, 0)))
                      for d in range(2)], axis=1).astype(BF16)
    return dict(
        in_proj=(f32(norm_mix)[:, None, :], w_aligned,
                 jnp.tile(f32(q_norm_a), (1, NA_HEADS))[:, None, :] * (NA_HEAD_DIM ** -0.5 * LOG2E),
                 jnp.tile(f32(k_norm_a), (1, NA_HEADS))[:, None, :], wdec, f32(b_decay)[:, :, None, :]),
        na=(_na_bias_table(rpb),),
        post=(w_o_a.astype(BF16), w_o_b.astype(BF16), w_out.astype(BF16), f32(gla_norm)[:, None, :],
              f32(norm_ffn)[:, None, :], w_router, b_router[:, None, :]),
        moe=(w_gate.astype(BF16), w_up.astype(BF16), w_down.astype(BF16)),
    )


def _tile(t, want):
    while t % want:
        want //= 2
    return want


def _trunk(x, p, depth):
    batch, n, _ = x.shape
    t = batch * n
    x = x.reshape(t, D_MODEL)
    for l in range(depth):
        qa, ka, va, qb, kb, vb, rb, gf, gr, ga, gb = _in_proj(x, l, *p["in_proj"], _tile(t, 512))
        ya = _na(qa, ka, va, l, *p["na"], batch, n)
        of, ob = _gla(qb, kb, vb, gf, gr, batch, n, GLA_BLOCK_TOK)
        x1, xn, route, meta = _post(x, ya, of, ob, rb, ga, gb, l, *p["post"], _tile(t, 1024))
        meta = meta[:, 0, :2 * N_EXPERTS].reshape(-1)
        x = _moe(meta, xn, route, x1, l, *p["moe"], _tile(t, MOE_TILE))
    return x.reshape(batch, n, D_MODEL)


def kernel(x_prompt, x_sample, norm_mix, w_in, q_norm_a, k_norm_a, rpb, w_decay, b_decay, gla_norm, w_o_a, w_o_b,
           w_out, norm_ffn, w_router_g, b_router_g, w_router_e, b_router_e, w_gate, w_up, w_down):
    p = _prep_params(norm_mix, w_in, q_norm_a, k_norm_a, rpb, w_decay, b_decay, gla_norm, w_o_a, w_o_b, w_out,
                     norm_ffn, w_router_g, b_router_g, w_router_e, b_router_e, w_gate, w_up, w_down)
    depth = w_in.shape[0]
    return (_trunk(x_prompt, p, depth), _trunk(x_sample, p, depth))
```

```python
import functools

import jax
import jax.numpy as jnp
import numpy as np
from jax import lax
from jax.experimental import pallas as pl
from jax.experimental.pallas import tpu as pltpu

F32 = jnp.float32
BF16 = jnp.bfloat16

D_MODEL = 1024
GRID_W = 64
NA_HEADS = 8
NA_HEAD_DIM = 64
NA_WIDTH = NA_HEADS * NA_HEAD_DIM
NA_ROWS = 8
NA_COLS = 16
RPB_ROWS = 2 * NA_ROWS - 1
RPB_COLS = 2 * NA_COLS - 1
GLA_HEADS = 4
GLA_DK = 128
GLA_DV = 128
GLA_WIDTH = GLA_HEADS * GLA_DV
GLA_RANK = 16
GLA_GATE_NORM = 16.0
GLA_CHUNK = 64
GLA_BLOCK_TOK = 1024
N_GROUPS = 4
EXPERTS_PER_GROUP = 4
N_EXPERTS = N_GROUPS * EXPERTS_PER_GROUP
D_EXPERT = 512
EPS = 1e-6
NEG = -1e30

LANES = 128
IN_Z_COL0 = 7 * 512
IN_GATE_COL0 = IN_Z_COL0 + LANES
MOE_SUB = 256
MOE_GRAN = 16
MOE_LOC_ROWS = 768
MOE_TILE = 2048
MOE_FAST_GRAN = 4
MOE_LOC_SLACK = (MOE_FAST_GRAN + 1) * MOE_GRAN
MOE_EXPERTS_PER_STEP = 2
MOE_EXPERT_STEPS = N_EXPERTS // MOE_EXPERTS_PER_STEP
MOE_COMBINE_STEPS = 2
MOE_ROW_BLOCK = 384
LOG2E = 1.4426950408889634
EXP_CLAMP = 80.0
ROUTER_GROUP_LANE0 = 0
ROUTER_EXPERT_LANE0 = N_GROUPS
VMEM_LIMIT = 56 * 1024 * 1024

_NT = (((1,), (1,)), ((), ()))
_TN = (((0,), (0,)), ((), ()))


def _dot(a, b):
    return jnp.dot(a, b, preferred_element_type=F32)


def _split_bf16(x):
    hi = x.astype(BF16)
    lo = (x - hi.astype(F32)).astype(BF16)
    return hi, lo


def _sigmoid(x):
    return 1.0 / (1.0 + jnp.exp(-x))


def _const_spec(stacked, l):
    tail = stacked.shape[1:]
    return pl.BlockSpec((None,) + tail, lambda *_: (l,) + (0,) * len(tail), pipeline_mode=pl.Buffered(1))


def _in_proj_kernel(x_ref, nrm_ref, w_ref, qg_ref, kg_ref, wdec_ref, bdec_ref,
                    qa_ref, ka_ref, va_ref, qb_ref, kb_ref, vb_ref, rb_ref, gf_ref, gr_ref, ga_ref, gb_ref):
    x = x_ref[...]
    ms = jnp.mean(x * x, axis=-1, keepdims=True)
    xn = (x * lax.rsqrt(ms + EPS) * nrm_ref[...]).astype(BF16)
    low_half = lax.broadcasted_iota(jnp.int32, (x.shape[0], LANES), 1) < NA_HEAD_DIM

    def proj(lo, width=512):
        return _dot(xn, w_ref[:, lo:lo + width])

    def head_norm(a, gain_ref):
        parts = []
        for p in range(NA_WIDTH // LANES):
            sq = a[:, p * LANES:(p + 1) * LANES]
            sq = sq * sq
            s_lo = jnp.sum(jnp.where(low_half, sq, 0.0), axis=-1, keepdims=True)
            s_hi = jnp.sum(jnp.where(low_half, 0.0, sq), axis=-1, keepdims=True)
            parts.append(jnp.where(low_half, s_lo, s_hi))
        ss = jnp.concatenate(parts, axis=-1)
        return a * lax.rsqrt(ss * (1.0 / NA_HEAD_DIM) + EPS) * gain_ref[...]

    z = proj(IN_Z_COL0, LANES).astype(BF16)
    ga_ref[...] = _sigmoid(proj(IN_GATE_COL0, D_MODEL)).astype(BF16)
    gb_ref[...] = _sigmoid(proj(IN_GATE_COL0 + D_MODEL, D_MODEL)).astype(BF16)
    for d, g_ref in enumerate((gf_ref, gr_ref)):
        lg = _dot(z, wdec_ref[d]) + bdec_ref[d]
        g_ref[...] = (jnp.minimum(lg, 0.0) - jnp.log(1.0 + jnp.exp(-jnp.abs(lg)))) * (1.0 / GLA_GATE_NORM)
    r = proj(3072)
    rb_ref[...] = (r * _sigmoid(r)).astype(BF16)
    qa = proj(0)
    ka = proj(512)
    qb_ref[...] = (proj(1536) * (GLA_DK ** -0.5)).astype(BF16)
    qa_ref[...] = head_norm(qa, qg_ref).astype(BF16)
    kb_ref[...] = proj(2048).astype(BF16)
    ka_ref[...] = head_norm(ka, kg_ref).astype(BF16)
    va_ref[...] = proj(1024).astype(BF16)
    vb_ref[...] = proj(2560).astype(BF16)


def _in_proj(x, l, nrm, w, qg, kg, wdec, bdec, tm):
    t = x.shape[0]
    row = lambda w: pl.BlockSpec((tm, w), lambda i: (i, 0))
    out_w = [512] * 7 + [GLA_WIDTH, GLA_WIDTH, D_MODEL, D_MODEL]
    out_dt = [BF16] * 7 + [F32, F32, BF16, BF16]
    consts = [nrm, w, qg, kg, wdec, bdec]
    return pl.pallas_call(
        _in_proj_kernel,
        grid=(t // tm,),
        in_specs=[row(D_MODEL)] + [_const_spec(c, l) for c in consts],
        out_specs=[row(w) for w in out_w],
        out_shape=[jax.ShapeDtypeStruct((t, w), dt) for w, dt in zip(out_w, out_dt)],
        compiler_params=pltpu.CompilerParams(dimension_semantics=("parallel",), vmem_limit_bytes=VMEM_LIMIT),
        name="in_proj",
    )(x, *consts)


NA_BLOCK_ROWS = 8
NA_BLOCK_TOK = NA_BLOCK_ROWS * GRID_W
NA_WINDOW_TOK = NA_ROWS * GRID_W


def _na_halo_start(i, n):
    return jnp.clip((i - 1) * NA_BLOCK_TOK, 0, n - min(3 * NA_BLOCK_TOK, n))


def _na_kernel(q_ref, kwin, vwin, bias_ref, o_ref, s_scr, *, rows):
    i = pl.program_id(1)
    halo0 = _na_halo_start(i, rows * GRID_W)
    low_half = lax.broadcasted_iota(jnp.int32, (GRID_W, LANES), 1) < NA_HEAD_DIM
    pairs = range(NA_HEADS // 2)
    cols = lambda p: slice(p * LANES, (p + 1) * LANES)
    qrows = lambda j: slice(j * GRID_W, (j + 1) * GRID_W)

    def window(j):
        r = i * NA_BLOCK_ROWS + j
        rs = jnp.clip(r - NA_ROWS // 2, 0, rows - NA_ROWS)
        start = pl.multiple_of(rs * GRID_W - halo0, GRID_W)
        return start, rs - r + NA_ROWS - 1

    def scores(j):
        start, bias_off = window(j)
        for p in pairs:
            qp = q_ref[qrows(j), cols(p)]
            zero = jnp.zeros_like(qp)
            q2 = jnp.concatenate([jnp.where(low_half, qp, zero), jnp.where(low_half, zero, qp)], axis=0)
            s = lax.dot_general(q2, kwin[pl.ds(start, NA_WINDOW_TOK), cols(p)], _NT, preferred_element_type=F32)
            s_scr[j % 2, p] = s + bias_ref[bias_off, p]

    def attend(j):
        start, _ = window(j)
        for p in pairs:
            s = s_scr[j % 2, p]
            e = jnp.exp2(s - jnp.max(s, axis=-1, keepdims=True))
            l = jnp.sum(e, axis=-1, keepdims=True)
            pv = _dot(e.astype(BF16), vwin[pl.ds(start, NA_WINDOW_TOK), cols(p)]) * (1.0 / l)
            o_ref[qrows(j), cols(p)] = jnp.where(low_half, pv[:GRID_W], pv[GRID_W:]).astype(BF16)

    scores(0)
    for j in range(NA_BLOCK_ROWS):
        if j + 1 < NA_BLOCK_ROWS:
            scores(j + 1)
        attend(j)


def _na(q, k, v, l, bias, batch, n):
    assert n % NA_BLOCK_TOK == 0, "sequence length must be a whole number of 8-row query blocks"
    rows = n // GRID_W
    nblk = rows // NA_BLOCK_ROWS
    cur = pl.BlockSpec((NA_BLOCK_TOK, NA_WIDTH), lambda b, i: (b * nblk + i, 0))
    halo = pl.BlockSpec((pl.Element(min(3 * NA_BLOCK_TOK, n)), pl.Element(NA_WIDTH)),
                        lambda b, i: (pl.multiple_of(b * n + _na_halo_start(i, n), NA_BLOCK_TOK), 0))
    return pl.pallas_call(
        functools.partial(_na_kernel, rows=rows),
        grid=(batch, nblk),
        in_specs=[cur, halo, halo, _const_spec(bias, l)],
        out_specs=cur,
        out_shape=jax.ShapeDtypeStruct((batch * n, NA_WIDTH), BF16),
        scratch_shapes=[pltpu.VMEM((2, NA_HEADS // 2, 2 * GRID_W, NA_WINDOW_TOK), F32)],
        compiler_params=pltpu.CompilerParams(dimension_semantics=("parallel", "parallel"),
                                             vmem_limit_bytes=VMEM_LIMIT),
        name="na",
    )(q, k, v, bias)


def _na_bias_table(rpb):
    cols = np.arange(GRID_W)
    col_start = np.clip(cols - NA_COLS // 2, 0, GRID_W - NA_COLS)
    col_mask = (cols[None, :] >= col_start[:, None]) & (cols[None, :] < col_start[:, None] + NA_COLS)
    col_idx = np.clip(cols[None, :] - cols[:, None] + NA_COLS - 1, 0, RPB_COLS - 1)
    pick_col = (col_idx[None] == np.arange(RPB_COLS)[:, None, None]).astype(np.float32)
    row = np.arange(RPB_ROWS)[:, None, None]
    pick_row = (row == np.arange(NA_ROWS)[:, None] + np.arange(NA_ROWS)[None, :]).astype(np.float32)
    bias = jnp.einsum("lhrk,kqc->lhrqc", rpb.astype(F32) * LOG2E, pick_col, precision=lax.Precision.HIGHEST)
    bias = jnp.einsum("lhrqc,roj->lohqjc", bias, pick_row, precision=lax.Precision.HIGHEST)
    masked = jnp.where(col_mask[:, None, :], bias, NEG)
    return masked.reshape(rpb.shape[0], NA_ROWS, NA_HEADS // 2, 2 * GRID_W, NA_WINDOW_TOK)


def _gla_chunk_rows(s):
    return slice(s * GLA_CHUNK, (s + 1) * GLA_CHUNK)


def _gla_head_cols(h):
    return slice(h * GLA_DK, (h + 1) * GLA_DK)


def _gla_cumulate(d, g, nsub):
    tri = jnp.where(d["mask"], 1.0, 0.0).astype(BF16)
    ghi, glo = _split_bf16(g)
    return [_dot(tri, ghi[_gla_chunk_rows(s)]) + _dot(tri, glo[_gla_chunk_rows(s)]) for s in range(nsub)]


def _gla_rescale(d, b_chunks):
    mid, end = d["mid"], d["end"]
    scaled = []
    for s, b in enumerate(b_chunks):
        rows = _gla_chunk_rows(s)
        b_mid = b[mid:mid + 1, :]
        b_end = b[end:end + 1, :]
        qe = d["q"][rows, :] * jnp.exp(jnp.minimum(b - b_mid, EXP_CLAMP)).astype(BF16)
        ke = d["k"][rows, :] * jnp.exp(jnp.minimum(b_mid - b, EXP_CLAMP)).astype(BF16)
        d["qi"][rows, :] = qe * jnp.exp(b_mid).astype(BF16)
        d["dec"][s] = jnp.broadcast_to(jnp.exp(b_end), (8, GLA_WIDTH))
        k_out = ke * jnp.exp(b_end - b_mid).astype(BF16)
        scaled.append((qe, ke, k_out))
    return scaled


def _gla_products(d, scaled):
    pairs = [(s, h) for s in range(len(scaled)) for h in range(GLA_HEADS)]
    v = lambda s, h: d["v"][_gla_chunk_rows(s), _gla_head_cols(h)]
    a = [lax.dot_general(scaled[s][0][:, _gla_head_cols(h)], scaled[s][1][:, _gla_head_cols(h)], _NT,
                         preferred_element_type=F32) for s, h in pairs]
    for s, h in pairs:
        d["kv"][s, h] = lax.dot_general(v(s, h), scaled[s][2][:, _gla_head_cols(h)], _TN,
                                        preferred_element_type=F32)
    for (s, h), a_sh in zip(pairs, a):
        d["oi"][_gla_chunk_rows(s), _gla_head_cols(h)] = _dot(jnp.where(d["mask"], a_sh, 0.0).astype(BF16), v(s, h))


def _gla_scan_step(d, s):
    rows = _gla_chunk_rows(s)
    dec = d["dec"][s]
    states = [d["st"][h] for h in range(GLA_HEADS)]
    inter = [lax.dot_general(d["qi"][rows, _gla_head_cols(h)], states[h].astype(BF16), _NT,
                             preferred_element_type=F32) for h in range(GLA_HEADS)]
    for h in range(GLA_HEADS):
        cols = _gla_head_cols(h)
        d["st"][h] = states[h] * dec[0:1, cols] + d["kv"][s, h]
        d["o"][rows, cols] = (d["oi"][rows, cols] + inter[h]).astype(BF16)


def _gla_kernel(qf_ref, kf_ref, vf_ref, gf_ref, qr_ref, kr_ref, vr_ref, gr_ref,
                of_ref, ob_ref, stf_ref, stb_ref, oif_s, oib_s, qif_s, qib_s, kvf_s, kvb_s, decf_s, decb_s, *, nsub):
    @pl.when(pl.program_id(1) == 0)
    def _():
        stf_ref[...] = jnp.zeros_like(stf_ref)
        stb_ref[...] = jnp.zeros_like(stb_ref)

    c = GLA_CHUNK
    ri = lax.broadcasted_iota(jnp.int32, (c, c), 0)
    ci = lax.broadcasted_iota(jnp.int32, (c, c), 1)
    fwd = dict(q=qf_ref, k=kf_ref, v=vf_ref, g=gf_ref, o=of_ref, st=stf_ref, oi=oif_s,
               qi=qif_s, kv=kvf_s, dec=decf_s, mask=ci <= ri, mid=c // 2 - 1, end=c - 1)
    bwd = dict(q=qr_ref, k=kr_ref, v=vr_ref, g=gr_ref, o=ob_ref, st=stb_ref, oi=oib_s,
               qi=qib_s, kv=kvb_s, dec=decb_s, mask=ci >= ri, mid=c // 2, end=0)
    b = [_gla_cumulate(d, d["g"][...], nsub) for d in (fwd, bwd)]
    scaled = [_gla_rescale(d, b_d) for d, b_d in zip((fwd, bwd), b)]
    for d, scaled_d in zip((fwd, bwd), scaled):
        _gla_products(d, scaled_d)
    for s in range(nsub):
        _gla_scan_step(fwd, s)
        _gla_scan_step(bwd, nsub - 1 - s)


def _gla(q, k, v, gf, gr, batch, n, cb):
    assert n % cb == 0, "sequence length must be a whole number of GLA blocks"
    nb = n // cb
    nsub = cb // GLA_CHUNK
    fwd = lambda w: pl.BlockSpec((cb, w), lambda b, c: (b * nb + c, 0))
    bwd = lambda w: pl.BlockSpec((cb, w), lambda b, c: (b * nb + nb - 1 - c, 0))
    out = jax.ShapeDtypeStruct((batch * n, GLA_WIDTH), BF16)
    state = pltpu.VMEM((GLA_HEADS, GLA_DV, GLA_DK), F32)
    return pl.pallas_call(
        functools.partial(_gla_kernel, nsub=nsub),
        grid=(batch, nb),
        in_specs=[fwd(GLA_WIDTH)] * 4 + [bwd(GLA_WIDTH)] * 4,
        out_specs=[fwd(GLA_WIDTH), bwd(GLA_WIDTH)],
        out_shape=[out, out],
        scratch_shapes=[state, state,
                        pltpu.VMEM((cb, GLA_WIDTH), F32), pltpu.VMEM((cb, GLA_WIDTH), F32),
                        pltpu.VMEM((cb, GLA_WIDTH), BF16), pltpu.VMEM((cb, GLA_WIDTH), BF16),
                        pltpu.VMEM((nsub, GLA_HEADS, GLA_DV, GLA_DK), F32),
                        pltpu.VMEM((nsub, GLA_HEADS, GLA_DV, GLA_DK), F32),
                        pltpu.VMEM((nsub, 8, GLA_WIDTH), F32), pltpu.VMEM((nsub, 8, GLA_WIDTH), F32)],
        compiler_params=pltpu.CompilerParams(dimension_semantics=("parallel", "arbitrary"),
                                             vmem_limit_bytes=VMEM_LIMIT),
        name="gla",
    )(q, k, v, gf, q, k, v, gr)


def _route(logits):
    lane = lax.broadcasted_iota(jnp.int32, logits.shape, 1)
    lane_f = lane.astype(F32)
    ninf = -jnp.inf

    def first_argmax(vals, vmax):
        return jnp.min(jnp.where(vals == vmax, lane_f, float(LANES)), axis=-1, keepdims=True)

    lg = jnp.where(lane < N_GROUPS, logits, ninf)
    g_max = jnp.max(lg, axis=-1, keepdims=True)
    p_sel = 1.0 / jnp.sum(jnp.exp(lg - g_max), axis=-1, keepdims=True)
    grp = first_argmax(lg, g_max)
    e_lo = ROUTER_EXPERT_LANE0 + EXPERTS_PER_GROUP * grp
    le = jnp.where((lane_f >= e_lo) & (lane_f < e_lo + EXPERTS_PER_GROUP), logits, ninf)
    l1 = jnp.max(le, axis=-1, keepdims=True)
    i1 = first_argmax(le, l1)
    le2 = jnp.where(lane_f == i1, ninf, le)
    l2 = jnp.max(le2, axis=-1, keepdims=True)
    i2 = first_argmax(le2, l2)
    t = jnp.exp(l2 - l1)
    w1 = p_sel / (1.0 + t)
    w2 = w1 * t
    return lane_f == i1, lane_f == i2, w1, w2


def _local_sort(sel1, sel2, w1, w2):
    n = sel1.shape[0]
    ri = lax.broadcasted_iota(jnp.int32, (n, n), 0)
    ci = lax.broadcasted_iota(jnp.int32, (n, n), 1)
    before = jnp.where(ci < ri, 1.0, 0.0).astype(BF16)
    li = lax.broadcasted_iota(jnp.int32, (LANES, LANES), 0)
    lj = lax.broadcasted_iota(jnp.int32, (LANES, LANES), 1)
    lower_lanes = jnp.where(li < lj, 1.0, 0.0).astype(BF16)
    oh1 = jnp.where(sel1, 1.0, 0.0)
    oh2 = jnp.where(sel2, 1.0, 0.0)
    cnt1 = jnp.sum(oh1, axis=0, keepdims=True)
    cnt = cnt1 + jnp.sum(oh2, axis=0, keepdims=True)
    gran = jnp.floor((cnt + (MOE_GRAN - 1)) * (1.0 / MOE_GRAN))
    offg = _dot(jnp.broadcast_to(gran, (8, LANES)).astype(BF16), lower_lanes)[0:1]
    pick = lambda sel, vals: jnp.sum(jnp.where(sel, vals, 0.0), axis=-1, keepdims=True)
    pos1 = pick(sel1, _dot(before, oh1.astype(BF16)) + MOE_GRAN * offg)
    pos2 = pick(sel2, _dot(before, oh2.astype(BF16)) + cnt1 + MOE_GRAN * offg)
    lane = lax.broadcasted_iota(jnp.int32, (n, LANES), 1)
    route = jnp.where(lane == 0, pos1, jnp.where(lane == 1, pos2, jnp.where(lane == 2, w1,
                                                                             jnp.where(lane == 3, w2, 0.0))))
    to_off = jnp.where((li == lj + ROUTER_EXPERT_LANE0) & (lj < N_EXPERTS), 1.0, 0.0).astype(BF16)
    to_len = jnp.where(li == lj + ROUTER_EXPERT_LANE0 - N_EXPERTS, 1.0, 0.0).astype(BF16)
    meta = (_dot(jnp.broadcast_to(offg, (8, LANES)).astype(BF16), to_off)
            + _dot(jnp.broadcast_to(gran, (8, LANES)).astype(BF16), to_len))
    return route, meta.astype(jnp.int32)


def _post_kernel(x_ref, ya_ref, of_ref, ob_ref, rb_ref, ga_ref, gb_ref, woa_ref, wob_ref, wout_ref,
                 gn_ref, nf_ref, wr_ref, br_ref, x1_ref, xn_ref, route_ref, meta_ref):
    nsub = x_ref.shape[0] // MOE_SUB
    subs = [slice(u * MOE_SUB, (u + 1) * MOE_SUB) for u in range(nsub)]
    val = [dict() for _ in range(nsub)]

    def branches(u):
        rows = subs[u]
        o = of_ref[rows, :].astype(F32) + ob_ref[rows, :].astype(F32)
        parts = []
        for h in range(GLA_HEADS):
            oh = o[:, h * GLA_DV:(h + 1) * GLA_DV]
            ms = jnp.mean(oh * oh, axis=-1, keepdims=True)
            parts.append(oh * lax.rsqrt(ms + EPS) * gn_ref[...])
        on = (jnp.concatenate(parts, axis=-1) * rb_ref[rows, :].astype(F32)).astype(BF16)
        val[u]["ya"] = _dot(ya_ref[rows, :], woa_ref[...])
        val[u]["yb"] = _dot(on, wob_ref[...])

    def merge(u):
        rows = subs[u]
        y = ga_ref[rows, :] * val[u]["ya"].astype(BF16) + gb_ref[rows, :] * val[u]["yb"].astype(BF16)
        val[u]["x1"] = x_ref[rows, :] + _dot(y, wout_ref[...])

    def norm_logits(u):
        rows = subs[u]
        x1 = val[u]["x1"]
        x1_ref[rows, :] = x1
        ms = jnp.mean(x1 * x1, axis=-1, keepdims=True)
        thi, tlo = _split_bf16(x1 * lax.rsqrt(ms + EPS) * nf_ref[...])
        xn_ref[rows, :] = thi
        hi = _dot(thi, wr_ref[...])
        val[u]["logits"] = hi[:, :LANES] + hi[:, LANES:] + _dot(tlo, wr_ref[:, :LANES]) + br_ref[...]

    def route(u):
        route_ref[subs[u], :], meta_ref[u] = _local_sort(*_route(val[u]["logits"]))

    stages = [branches, merge, norm_logits, route]
    for step in range(nsub + len(stages) - 1):
        for u in range(nsub):
            if 0 <= step - u < len(stages):
                stages[step - u](u)


def _post(x, ya, of, ob, rb, ga, gb, l, woa, wob, wout, gn, nf, wr, br, tm):
    t = x.shape[0]
    row = lambda w: pl.BlockSpec((tm, w), lambda i: (i, 0))
    consts = [woa, wob, wout, gn, nf, wr, br]
    return pl.pallas_call(
        _post_kernel,
        grid=(t // tm,),
        in_specs=[row(D_MODEL), row(NA_WIDTH), row(GLA_WIDTH), row(GLA_WIDTH), row(GLA_WIDTH),
                  row(D_MODEL), row(D_MODEL)] + [_const_spec(c, l) for c in consts],
        out_specs=[row(D_MODEL), row(D_MODEL), row(LANES),
                   pl.BlockSpec((tm // MOE_SUB, 8, LANES), lambda i: (i, 0, 0))],
        out_shape=[jax.ShapeDtypeStruct((t, D_MODEL), F32), jax.ShapeDtypeStruct((t, D_MODEL), BF16),
                   jax.ShapeDtypeStruct((t, LANES), F32),
                   jax.ShapeDtypeStruct((t // MOE_SUB, 8, LANES), jnp.int32)],
        compiler_params=pltpu.CompilerParams(dimension_semantics=("parallel",), vmem_limit_bytes=VMEM_LIMIT),
        name="post",
    )(x, ya, of, ob, rb, ga, gb, *consts)


def _moe_kernel(meta_ref, xn_ref, route_ref, x1_ref, wg_ref, wu_ref, wd_ref, o_ref, loc, stage, *, nsub):
    i = pl.program_id(0)
    step = pl.program_id(1)
    stage_spare = stage.shape[0] // MOE_GRAN - 1
    sub_rows = lambda u: slice(u * MOE_SUB, (u + 1) * MOE_SUB)
    gran_rows = lambda g: pl.ds(pl.multiple_of(g * MOE_GRAN, MOE_GRAN), MOE_GRAN)

    @pl.when(step == 0)
    def _dispatch():
        stage[...] = jnp.zeros_like(stage)
        sel = lax.broadcasted_iota(jnp.int32, (8, LANES), 0) == lax.broadcasted_iota(jnp.int32, (8, LANES), 1)
        sel = jnp.where(sel, 1.0, 0.0).astype(BF16)
        row_id = lax.broadcasted_iota(jnp.int32, (MOE_LOC_ROWS, MOE_SUB), 0).astype(F32)
        for u in range(nsub):
            rhi, rlo = _split_bf16(route_ref[sub_rows(u), :])
            pos_t = (lax.dot_general(sel, rhi, _NT, preferred_element_type=F32)
                     + lax.dot_general(sel, rlo, _NT, preferred_element_type=F32))
            onehot = jnp.where(row_id == pos_t[0:1], 1.0, 0.0) + jnp.where(row_id == pos_t[1:2], 1.0, 0.0)
            loc[u, :MOE_LOC_ROWS, :] = _dot(onehot.astype(BF16), xn_ref[sub_rows(u), :]).astype(BF16)
            loc[u, MOE_LOC_ROWS:, :] = jnp.zeros((MOE_LOC_SLACK, D_MODEL), BF16)

    def move_granules(e, to_stage):
        meta = lambda u, k: meta_ref[((i * nsub + u) * 2 + k) * N_EXPERTS + e]

        def move(u, off, dst, g, live=None):
            if to_stage:
                to = dst + g if live is None else jnp.where(live, dst + g, stage_spare)
                stage[gran_rows(to), :] = loc[u, gran_rows(off + g), :]
            else:
                to = off + g if live is None else jnp.where(live, off + g, MOE_LOC_ROWS // MOE_GRAN)
                loc[u, gran_rows(to), :] = stage[gran_rows(dst + g), :]

        runs = []
        dst = 0
        for u in range(nsub):
            off, n = meta(u, 0), meta(u, 1)
            for g in range(MOE_FAST_GRAN):
                move(u, off, dst, g, live=g < n)
            runs.append((u, off, n, dst))
            dst = dst + n

        @pl.when(functools.reduce(jnp.maximum, [n for _, _, n, _ in runs]) > MOE_FAST_GRAN)
        def _long_runs():
            for u, off, n, dst_u in runs:
                lax.fori_loop(MOE_FAST_GRAN, n, lambda g, c, u=u, off=off, dst_u=dst_u: (move(u, off, dst_u, g), c)[1], 0)

        return dst

    @pl.when(step < MOE_EXPERT_STEPS)
    def _experts():
        for k in range(MOE_EXPERTS_PER_STEP):
            e = step * MOE_EXPERTS_PER_STEP + k
            total = move_granules(e, True)

            def block(b, carry, k=k):
                rows = pl.ds(pl.multiple_of(b * MOE_ROW_BLOCK, MOE_ROW_BLOCK), MOE_ROW_BLOCK)
                xs = stage[rows, :]
                hg = _dot(xs, wg_ref[k])
                h = (hg * _sigmoid(hg) * _dot(xs, wu_ref[k])).astype(BF16)
                stage[rows, :] = _dot(h, wd_ref[k]).astype(BF16)
                return carry

            lax.fori_loop(0, (total * MOE_GRAN + MOE_ROW_BLOCK - 1) // MOE_ROW_BLOCK, block, 0)
            move_granules(e, False)

    @pl.when(step >= MOE_EXPERT_STEPS)
    def _combine():
        col_id = lax.broadcasted_iota(jnp.int32, (MOE_SUB, MOE_LOC_ROWS), 1).astype(F32)
        per_slice = nsub // MOE_COMBINE_STEPS
        for j in range(per_slice):
            u = (step - MOE_EXPERT_STEPS) * per_slice + j
            r = route_ref[pl.ds(pl.multiple_of(u * MOE_SUB, MOE_SUB), MOE_SUB), :]
            pick = jnp.where(col_id == r[:, 0:1], r[:, 2:3], jnp.where(col_id == r[:, 1:2], r[:, 3:4], 0.0))
            o_ref[sub_rows(j), :] = x1_ref[sub_rows(j), :] + _dot(pick.astype(BF16), loc[u, :MOE_LOC_ROWS, :])


def _moe(meta, xn, route, x1, l, wg, wu, wd, tb):
    t = xn.shape[0]
    nsub = tb // MOE_SUB
    assert nsub % MOE_COMBINE_STEPS == 0
    row = lambda w, **kw: pl.BlockSpec((tb, w), lambda i, s, m: (i, 0), **kw)
    sliced = pl.BlockSpec((tb // MOE_COMBINE_STEPS, D_MODEL),
                          lambda i, s, m: (i * MOE_COMBINE_STEPS + jnp.maximum(s - MOE_EXPERT_STEPS, 0), 0))
    wspec = lambda a, b: pl.BlockSpec((None, MOE_EXPERTS_PER_STEP, a, b),
                                      lambda i, s, m: (l, jnp.minimum(s, MOE_EXPERT_STEPS - 1), 0, 0))
    return pl.pallas_call(
        functools.partial(_moe_kernel, nsub=nsub),
        grid_spec=pltpu.PrefetchScalarGridSpec(
            num_scalar_prefetch=1,
            grid=(t // tb, MOE_EXPERT_STEPS + MOE_COMBINE_STEPS),
            in_specs=[row(D_MODEL, pipeline_mode=pl.Buffered(1)), row(LANES), sliced,
                      wspec(D_MODEL, D_EXPERT), wspec(D_MODEL, D_EXPERT), wspec(D_EXPERT, D_MODEL)],
            out_specs=sliced,
            scratch_shapes=[pltpu.VMEM((nsub, MOE_LOC_ROWS + MOE_LOC_SLACK, D_MODEL), BF16),
                            pltpu.VMEM((pl.cdiv(tb + MOE_LOC_SLACK, MOE_ROW_BLOCK) * MOE_ROW_BLOCK, D_MODEL), BF16)]),
        out_shape=jax.ShapeDtypeStruct((t, D_MODEL), F32),
        compiler_params=pltpu.CompilerParams(dimension_semantics=("parallel", "arbitrary"),
                                             vmem_limit_bytes=VMEM_LIMIT),
        name="moe",
    )(meta, xn, route, x1, wg, wu, wd)


def _prep_params(norm_mix, w_in, q_norm_a, k_norm_a, rpb, w_decay, b_decay, gla_norm, w_o_a, w_o_b, w_out,
                 norm_ffn, w_router_g, b_router_g, w_router_e, b_router_e, w_gate, w_up, w_down):
    depth = w_in.shape[0]
    f32 = lambda a: a.astype(F32)
    pad_lanes = lambda a, before, total: jnp.pad(f32(a), [(0, 0)] * (a.ndim - 1) + [(before, total - before - a.shape[-1])])
    w_aligned = jnp.concatenate([w_in[:, :, :IN_Z_COL0 + 2 * GLA_RANK],
                                 jnp.zeros((depth, D_MODEL, IN_GATE_COL0 - IN_Z_COL0 - 2 * GLA_RANK), w_in.dtype),
                                 w_in[:, :, IN_Z_COL0 + 2 * GLA_RANK:]], axis=-1).astype(BF16)
    w_router = pad_lanes(jnp.concatenate([w_router_g, w_router_e], axis=-1), ROUTER_GROUP_LANE0, LANES)
    b_router = pad_lanes(jnp.concatenate([b_router_g, b_router_e], axis=-1), ROUTER_GROUP_LANE0, LANES)
    w_router = jnp.concatenate(_split_bf16(w_router), axis=-1)
    wdec = jnp.stack([jnp.pad(f32(w_decay[:, d]), ((0, 0), (d * GLA_RANK, LANES - (d + 1) * GLA_RANK), (0, 0)))
                      for d in range(2)], axis=1).astype(BF16)
    return dict(
        in_proj=(f32(norm_mix)[:, None, :], w_aligned,
                 jnp.tile(f32(q_norm_a), (1, NA_HEADS))[:, None, :] * (NA_HEAD_DIM ** -0.5 * LOG2E),
                 jnp.tile(f32(k_norm_a), (1, NA_HEADS))[:, None, :], wdec, f32(b_decay)[:, :, None, :]),
        na=(_na_bias_table(rpb),),
        post=(w_o_a.astype(BF16), w_o_b.astype(BF16), w_out.astype(BF16), f32(gla_norm)[:, None, :],
              f32(norm_ffn)[:, None, :], w_router, b_router[:, None, :]),
        moe=(w_gate.astype(BF16), w_up.astype(BF16), w_down.astype(BF16)),
    )


def _tile(t, want):
    while t % want:
        want //= 2
    return want


def _trunk(x, p, depth):
    batch, n, _ = x.shape
    t = batch * n
    x = x.reshape(t, D_MODEL)
    for l in range(depth):
        qa, ka, va, qb, kb, vb, rb, gf, gr, ga, gb = _in_proj(x, l, *p["in_proj"], _tile(t, 512))
        ya = _na(qa, ka, va, l, *p["na"], batch, n)
        of, ob = _gla(qb, kb, vb, gf, gr, batch, n, GLA_BLOCK_TOK)
        x1, xn, route, meta = _post(x, ya, of, ob, rb, ga, gb, l, *p["post"], _tile(t, 1024))
        meta = meta[:, 0, :2 * N_EXPERTS].reshape(-1)
        x = _moe(meta, xn, route, x1, l, *p["moe"], _tile(t, MOE_TILE))
    return x.reshape(batch, n, D_MODEL)


def kernel(x_prompt, x_sample, norm_mix, w_in, q_norm_a, k_norm_a, rpb, w_decay, b_decay, gla_norm, w_o_a, w_o_b,
           w_out, norm_ffn, w_router_g, b_router_g, w_router_e, b_router_e, w_gate, w_up, w_down):
    p = _prep_params(norm_mix, w_in, q_norm_a, k_norm_a, rpb, w_decay, b_decay, gla_norm, w_o_a, w_o_b, w_out,
                     norm_ffn, w_router_g, b_router_g, w_router_e, b_router_e, w_gate, w_up, w_down)
    depth = w_in.shape[0]
    return (_trunk(x_prompt, p, depth), _trunk(x_sample, p, depth))
```

```python
import functools

import jax
import jax.numpy as jnp
import numpy as np
from jax import lax
from jax.experimental import pallas as pl
from jax.experimental.pallas import tpu as pltpu

F32 = jnp.float32
BF16 = jnp.bfloat16

D_MODEL = 1024
GRID_W = 64
NA_HEADS = 8
NA_HEAD_DIM = 64
NA_WIDTH = NA_HEADS * NA_HEAD_DIM
NA_ROWS = 8
NA_COLS = 16
RPB_ROWS = 2 * NA_ROWS - 1
RPB_COLS = 2 * NA_COLS - 1
GLA_HEADS = 4
GLA_DK = 128
GLA_DV = 128
GLA_WIDTH = GLA_HEADS * GLA_DV
GLA_RANK = 16
GLA_GATE_NORM = 16.0
GLA_CHUNK = 64
GLA_BLOCK_TOK = 1024
N_GROUPS = 4
EXPERTS_PER_GROUP = 4
N_EXPERTS = N_GROUPS * EXPERTS_PER_GROUP
D_EXPERT = 512
EPS = 1e-6
NEG = -1e30

LANES = 128
IN_Z_COL0 = 7 * 512
IN_GATE_COL0 = IN_Z_COL0 + LANES
MOE_SUB = 256
MOE_GRAN = 16
MOE_LOC_ROWS = 768
MOE_TILE = 2048
MOE_FAST_GRAN = 4
MOE_LOC_SLACK = (MOE_FAST_GRAN + 1) * MOE_GRAN
MOE_EXPERTS_PER_STEP = 2
MOE_EXPERT_STEPS = N_EXPERTS // MOE_EXPERTS_PER_STEP
MOE_COMBINE_STEPS = 2
MOE_ROW_BLOCK = 384
LOG2E = 1.4426950408889634
EXP_CLAMP = 80.0
ROUTER_GROUP_LANE0 = 0
ROUTER_EXPERT_LANE0 = N_GROUPS
VMEM_LIMIT = 56 * 1024 * 1024

_NT = (((1,), (1,)), ((), ()))
_TN = (((0,), (0,)), ((), ()))


def _dot(a, b):
    return jnp.dot(a, b, preferred_element_type=F32)


def _split_bf16(x):
    hi = x.astype(BF16)
    lo = (x - hi.astype(F32)).astype(BF16)
    return hi, lo


def _sigmoid(x):
    return 1.0 / (1.0 + jnp.exp(-x))


def _const_spec(stacked, l):
    tail = stacked.shape[1:]
    return pl.BlockSpec((None,) + tail, lambda *_: (l,) + (0,) * len(tail), pipeline_mode=pl.Buffered(1))


def _in_proj_kernel(x_ref, nrm_ref, w_ref, qg_ref, kg_ref, wdec_ref, bdec_ref,
                    qa_ref, ka_ref, va_ref, qb_ref, kb_ref, vb_ref, rb_ref, gf_ref, gr_ref, ga_ref, gb_ref):
    x = x_ref[...]
    ms = jnp.mean(x * x, axis=-1, keepdims=True)
    xn = (x * lax.rsqrt(ms + EPS) * nrm_ref[...]).astype(BF16)
    low_half = lax.broadcasted_iota(jnp.int32, (x.shape[0], LANES), 1) < NA_HEAD_DIM

    def proj(lo, width=512):
        return _dot(xn, w_ref[:, lo:lo + width])

    def head_norm(a, gain_ref):
        parts = []
        for p in range(NA_WIDTH // LANES):
            sq = a[:, p * LANES:(p + 1) * LANES]
            sq = sq * sq
            s_lo = jnp.sum(jnp.where(low_half, sq, 0.0), axis=-1, keepdims=True)
            s_hi = jnp.sum(jnp.where(low_half, 0.0, sq), axis=-1, keepdims=True)
            parts.append(jnp.where(low_half, s_lo, s_hi))
        ss = jnp.concatenate(parts, axis=-1)
        return a * lax.rsqrt(ss * (1.0 / NA_HEAD_DIM) + EPS) * gain_ref[...]

    z = proj(IN_Z_COL0, LANES).astype(BF16)
    ga_ref[...] = _sigmoid(proj(IN_GATE_COL0, D_MODEL)).astype(BF16)
    gb_ref[...] = _sigmoid(proj(IN_GATE_COL0 + D_MODEL, D_MODEL)).astype(BF16)
    for d, g_ref in enumerate((gf_ref, gr_ref)):
        lg = _dot(z, wdec_ref[d]) + bdec_ref[d]
        g_ref[...] = (jnp.minimum(lg, 0.0) - jnp.log(1.0 + jnp.exp(-jnp.abs(lg)))) * (1.0 / GLA_GATE_NORM)
    r = proj(3072)
    rb_ref[...] = (r * _sigmoid(r)).astype(BF16)
    qa = proj(0)
    ka = proj(512)
    qb_ref[...] = (proj(1536) * (GLA_DK ** -0.5)).astype(BF16)
    qa_ref[...] = head_norm(qa, qg_ref).astype(BF16)
    kb_ref[...] = proj(2048).astype(BF16)
    ka_ref[...] = head_norm(ka, kg_ref).astype(BF16)
    va_ref[...] = proj(1024).astype(BF16)
    vb_ref[...] = proj(2560).astype(BF16)


def _in_proj(x, l, nrm, w, qg, kg, wdec, bdec, tm):
    t = x.shape[0]
    row = lambda w: pl.BlockSpec((tm, w), lambda i: (i, 0))
    out_w = [512] * 7 + [GLA_WIDTH, GLA_WIDTH, D_MODEL, D_MODEL]
    out_dt = [BF16] * 7 + [F32, F32, BF16, BF16]
    consts = [nrm, w, qg, kg, wdec, bdec]
    return pl.pallas_call(
        _in_proj_kernel,
        grid=(t // tm,),
        in_specs=[row(D_MODEL)] + [_const_spec(c, l) for c in consts],
        out_specs=[row(w) for w in out_w],
        out_shape=[jax.ShapeDtypeStruct((t, w), dt) for w, dt in zip(out_w, out_dt)],
        compiler_params=pltpu.CompilerParams(dimension_semantics=("parallel",), vmem_limit_bytes=VMEM_LIMIT),
        name="in_proj",
    )(x, *consts)


NA_BLOCK_ROWS = 16
NA_BLOCK_TOK = NA_BLOCK_ROWS * GRID_W
NA_WINDOW_TOK = NA_ROWS * GRID_W
NA_HALO_TOK = NA_BLOCK_TOK + 2 * NA_WINDOW_TOK


def _na_halo_start(i, n):
    return jnp.clip(i * NA_BLOCK_TOK - NA_WINDOW_TOK, 0, n - min(NA_HALO_TOK, n))


def _na_kernel(q_ref, kwin, vwin, bias_ref, o_ref, s_scr, *, rows):
    i = pl.program_id(1)
    halo0 = _na_halo_start(i, rows * GRID_W)
    low_half = lax.broadcasted_iota(jnp.int32, (GRID_W, LANES), 1) < NA_HEAD_DIM
    pairs = range(NA_HEADS // 2)
    cols = lambda p: slice(p * LANES, (p + 1) * LANES)
    qrows = lambda j: slice(j * GRID_W, (j + 1) * GRID_W)

    def window(j):
        r = i * NA_BLOCK_ROWS + j
        rs = jnp.clip(r - NA_ROWS // 2, 0, rows - NA_ROWS)
        start = pl.multiple_of(rs * GRID_W - halo0, GRID_W)
        return start, rs - r + NA_ROWS - 1

    def scores(j):
        start, bias_off = window(j)
        for p in pairs:
            qp = q_ref[qrows(j), cols(p)]
            zero = jnp.zeros_like(qp)
            q2 = jnp.concatenate([jnp.where(low_half, qp, zero), jnp.where(low_half, zero, qp)], axis=0)
            s = lax.dot_general(q2, kwin[pl.ds(start, NA_WINDOW_TOK), cols(p)], _NT, preferred_element_type=F32)
            s_scr[j % 2, p] = s + bias_ref[bias_off, p]

    def attend(j):
        start, _ = window(j)
        for p in pairs:
            s = s_scr[j % 2, p]
            e = jnp.exp2(s - jnp.max(s, axis=-1, keepdims=True))
            l = jnp.sum(e, axis=-1, keepdims=True)
            pv = _dot(e.astype(BF16), vwin[pl.ds(start, NA_WINDOW_TOK), cols(p)]) * (1.0 / l)
            o_ref[qrows(j), cols(p)] = jnp.where(low_half, pv[:GRID_W], pv[GRID_W:]).astype(BF16)

    scores(0)
    for j in range(NA_BLOCK_ROWS):
        if j + 1 < NA_BLOCK_ROWS:
            scores(j + 1)
        attend(j)


def _na(q, k, v, l, bias, batch, n):
    assert n % NA_BLOCK_TOK == 0, "sequence length must be a whole number of 8-row query blocks"
    rows = n // GRID_W
    nblk = rows // NA_BLOCK_ROWS
    cur = pl.BlockSpec((NA_BLOCK_TOK, NA_WIDTH), lambda b, i: (b * nblk + i, 0))
    halo = pl.BlockSpec((pl.Element(min(NA_HALO_TOK, n)), pl.Element(NA_WIDTH)),
                        lambda b, i: (pl.multiple_of(b * n + _na_halo_start(i, n), NA_WINDOW_TOK), 0))
    return pl.pallas_call(
        functools.partial(_na_kernel, rows=rows),
        grid=(batch, nblk),
        in_specs=[cur, halo, halo, _const_spec(bias, l)],
        out_specs=cur,
        out_shape=jax.ShapeDtypeStruct((batch * n, NA_WIDTH), BF16),
        scratch_shapes=[pltpu.VMEM((2, NA_HEADS // 2, 2 * GRID_W, NA_WINDOW_TOK), F32)],
        compiler_params=pltpu.CompilerParams(dimension_semantics=("parallel", "parallel"),
                                             vmem_limit_bytes=VMEM_LIMIT),
        name="na",
    )(q, k, v, bias)


def _na_bias_table(rpb):
    cols = np.arange(GRID_W)
    col_start = np.clip(cols - NA_COLS // 2, 0, GRID_W - NA_COLS)
    col_mask = (cols[None, :] >= col_start[:, None]) & (cols[None, :] < col_start[:, None] + NA_COLS)
    col_idx = np.clip(cols[None, :] - cols[:, None] + NA_COLS - 1, 0, RPB_COLS - 1)
    pick_col = (col_idx[None] == np.arange(RPB_COLS)[:, None, None]).astype(np.float32)
    row = np.arange(RPB_ROWS)[:, None, None]
    pick_row = (row == np.arange(NA_ROWS)[:, None] + np.arange(NA_ROWS)[None, :]).astype(np.float32)
    bias = jnp.einsum("lhrk,kqc->lhrqc", rpb.astype(F32) * LOG2E, pick_col, precision=lax.Precision.HIGHEST)
    bias = jnp.einsum("lhrqc,roj->lohqjc", bias, pick_row, precision=lax.Precision.HIGHEST)
    masked = jnp.where(col_mask[:, None, :], bias, NEG)
    return masked.reshape(rpb.shape[0], NA_ROWS, NA_HEADS // 2, 2 * GRID_W, NA_WINDOW_TOK)


def _gla_chunk_rows(s):
    return slice(s * GLA_CHUNK, (s + 1) * GLA_CHUNK)


def _gla_head_cols(h):
    return slice(h * GLA_DK, (h + 1) * GLA_DK)


def _gla_cumulate(d, g, nsub):
    tri = jnp.where(d["mask"], 1.0, 0.0).astype(BF16)
    ghi, glo = _split_bf16(g)
    return [_dot(tri, ghi[_gla_chunk_rows(s)]) + _dot(tri, glo[_gla_chunk_rows(s)]) for s in range(nsub)]


def _gla_rescale(d, b_chunks):
    mid, end = d["mid"], d["end"]
    scaled = []
    for s, b in enumerate(b_chunks):
        rows = _gla_chunk_rows(s)
        b_mid = b[mid:mid + 1, :]
        b_end = b[end:end + 1, :]
        qe = d["q"][rows, :] * jnp.exp(jnp.minimum(b - b_mid, EXP_CLAMP)).astype(BF16)
        ke = d["k"][rows, :] * jnp.exp(jnp.minimum(b_mid - b, EXP_CLAMP)).astype(BF16)
        d["qi"][rows, :] = qe * jnp.exp(b_mid).astype(BF16)
        d["dec"][s] = jnp.broadcast_to(jnp.exp(b_end), (8, GLA_WIDTH))
        k_out = ke * jnp.exp(b_end - b_mid).astype(BF16)
        scaled.append((qe, ke, k_out))
    return scaled


def _gla_products(d, scaled):
    pairs = [(s, h) for s in range(len(scaled)) for h in range(GLA_HEADS)]
    v = lambda s, h: d["v"][_gla_chunk_rows(s), _gla_head_cols(h)]
    a = [lax.dot_general(scaled[s][0][:, _gla_head_cols(h)], scaled[s][1][:, _gla_head_cols(h)], _NT,
                         preferred_element_type=F32) for s, h in pairs]
    for s, h in pairs:
        d["kv"][s, h] = lax.dot_general(v(s, h), scaled[s][2][:, _gla_head_cols(h)], _TN,
                                        preferred_element_type=F32)
    for (s, h), a_sh in zip(pairs, a):
        d["oi"][_gla_chunk_rows(s), _gla_head_cols(h)] = _dot(jnp.where(d["mask"], a_sh, 0.0).astype(BF16), v(s, h))


def _gla_scan_step(d, s):
    rows = _gla_chunk_rows(s)
    dec = d["dec"][s]
    states = [d["st"][h] for h in range(GLA_HEADS)]
    inter = [lax.dot_general(d["qi"][rows, _gla_head_cols(h)], states[h].astype(BF16), _NT,
                             preferred_element_type=F32) for h in range(GLA_HEADS)]
    for h in range(GLA_HEADS):
        cols = _gla_head_cols(h)
        d["st"][h] = states[h] * dec[0:1, cols] + d["kv"][s, h]
        d["o"][rows, cols] = (d["oi"][rows, cols] + inter[h]).astype(BF16)


def _gla_kernel(qf_ref, kf_ref, vf_ref, gf_ref, qr_ref, kr_ref, vr_ref, gr_ref,
                of_ref, ob_ref, stf_ref, stb_ref, oif_s, oib_s, qif_s, qib_s, kvf_s, kvb_s, decf_s, decb_s, *, nsub):
    @pl.when(pl.program_id(1) == 0)
    def _():
        stf_ref[...] = jnp.zeros_like(stf_ref)
        stb_ref[...] = jnp.zeros_like(stb_ref)

    c = GLA_CHUNK
    ri = lax.broadcasted_iota(jnp.int32, (c, c), 0)
    ci = lax.broadcasted_iota(jnp.int32, (c, c), 1)
    fwd = dict(q=qf_ref, k=kf_ref, v=vf_ref, g=gf_ref, o=of_ref, st=stf_ref, oi=oif_s,
               qi=qif_s, kv=kvf_s, dec=decf_s, mask=ci <= ri, mid=c // 2 - 1, end=c - 1)
    bwd = dict(q=qr_ref, k=kr_ref, v=vr_ref, g=gr_ref, o=ob_ref, st=stb_ref, oi=oib_s,
               qi=qib_s, kv=kvb_s, dec=decb_s, mask=ci >= ri, mid=c // 2, end=0)
    b = [_gla_cumulate(d, d["g"][...], nsub) for d in (fwd, bwd)]
    scaled = [_gla_rescale(d, b_d) for d, b_d in zip((fwd, bwd), b)]
    for d, scaled_d in zip((fwd, bwd), scaled):
        _gla_products(d, scaled_d)
    for s in range(nsub):
        _gla_scan_step(fwd, s)
        _gla_scan_step(bwd, nsub - 1 - s)


def _gla(q, k, v, gf, gr, batch, n, cb):
    assert n % cb == 0, "sequence length must be a whole number of GLA blocks"
    nb = n // cb
    nsub = cb // GLA_CHUNK
    fwd = lambda w: pl.BlockSpec((cb, w), lambda b, c: (b * nb + c, 0))
    bwd = lambda w: pl.BlockSpec((cb, w), lambda b, c: (b * nb + nb - 1 - c, 0))
    out = jax.ShapeDtypeStruct((batch * n, GLA_WIDTH), BF16)
    state = pltpu.VMEM((GLA_HEADS, GLA_DV, GLA_DK), F32)
    return pl.pallas_call(
        functools.partial(_gla_kernel, nsub=nsub),
        grid=(batch, nb),
        in_specs=[fwd(GLA_WIDTH)] * 4 + [bwd(GLA_WIDTH)] * 4,
        out_specs=[fwd(GLA_WIDTH), bwd(GLA_WIDTH)],
        out_shape=[out, out],
        scratch_shapes=[state, state,
                        pltpu.VMEM((cb, GLA_WIDTH), F32), pltpu.VMEM((cb, GLA_WIDTH), F32),
                        pltpu.VMEM((cb, GLA_WIDTH), BF16), pltpu.VMEM((cb, GLA_WIDTH), BF16),
                        pltpu.VMEM((nsub, GLA_HEADS, GLA_DV, GLA_DK), F32),
                        pltpu.VMEM((nsub, GLA_HEADS, GLA_DV, GLA_DK), F32),
                        pltpu.VMEM((nsub, 8, GLA_WIDTH), F32), pltpu.VMEM((nsub, 8, GLA_WIDTH), F32)],
        compiler_params=pltpu.CompilerParams(dimension_semantics=("parallel", "arbitrary"),
                                             vmem_limit_bytes=VMEM_LIMIT),
        name="gla",
    )(q, k, v, gf, q, k, v, gr)


def _route(logits):
    lane = lax.broadcasted_iota(jnp.int32, logits.shape, 1)
    lane_f = lane.astype(F32)
    ninf = -jnp.inf

    def first_argmax(vals, vmax):
        return jnp.min(jnp.where(vals == vmax, lane_f, float(LANES)), axis=-1, keepdims=True)

    lg = jnp.where(lane < N_GROUPS, logits, ninf)
    g_max = jnp.max(lg, axis=-1, keepdims=True)
    p_sel = 1.0 / jnp.sum(jnp.exp(lg - g_max), axis=-1, keepdims=True)
    grp = first_argmax(lg, g_max)
    e_lo = ROUTER_EXPERT_LANE0 + EXPERTS_PER_GROUP * grp
    le = jnp.where((lane_f >= e_lo) & (lane_f < e_lo + EXPERTS_PER_GROUP), logits, ninf)
    l1 = jnp.max(le, axis=-1, keepdims=True)
    i1 = first_argmax(le, l1)
    le2 = jnp.where(lane_f == i1, ninf, le)
    l2 = jnp.max(le2, axis=-1, keepdims=True)
    i2 = first_argmax(le2, l2)
    t = jnp.exp(l2 - l1)
    w1 = p_sel / (1.0 + t)
    w2 = w1 * t
    return lane_f == i1, lane_f == i2, w1, w2


def _local_sort(sel1, sel2, w1, w2):
    n = sel1.shape[0]
    ri = lax.broadcasted_iota(jnp.int32, (n, n), 0)
    ci = lax.broadcasted_iota(jnp.int32, (n, n), 1)
    before = jnp.where(ci < ri, 1.0, 0.0).astype(BF16)
    li = lax.broadcasted_iota(jnp.int32, (LANES, LANES), 0)
    lj = lax.broadcasted_iota(jnp.int32, (LANES, LANES), 1)
    lower_lanes = jnp.where(li < lj, 1.0, 0.0).astype(BF16)
    oh1 = jnp.where(sel1, 1.0, 0.0)
    oh2 = jnp.where(sel2, 1.0, 0.0)
    cnt1 = jnp.sum(oh1, axis=0, keepdims=True)
    cnt = cnt1 + jnp.sum(oh2, axis=0, keepdims=True)
    gran = jnp.floor((cnt + (MOE_GRAN - 1)) * (1.0 / MOE_GRAN))
    offg = _dot(jnp.broadcast_to(gran, (8, LANES)).astype(BF16), lower_lanes)[0:1]
    pick = lambda sel, vals: jnp.sum(jnp.where(sel, vals, 0.0), axis=-1, keepdims=True)
    pos1 = pick(sel1, _dot(before, oh1.astype(BF16)) + MOE_GRAN * offg)
    pos2 = pick(sel2, _dot(before, oh2.astype(BF16)) + cnt1 + MOE_GRAN * offg)
    lane = lax.broadcasted_iota(jnp.int32, (n, LANES), 1)
    route = jnp.where(lane == 0, pos1, jnp.where(lane == 1, pos2, jnp.where(lane == 2, w1,
                                                                             jnp.where(lane == 3, w2, 0.0))))
    to_off = jnp.where((li == lj + ROUTER_EXPERT_LANE0) & (lj < N_EXPERTS), 1.0, 0.0).astype(BF16)
    to_len = jnp.where(li == lj + ROUTER_EXPERT_LANE0 - N_EXPERTS, 1.0, 0.0).astype(BF16)
    meta = (_dot(jnp.broadcast_to(offg, (8, LANES)).astype(BF16), to_off)
            + _dot(jnp.broadcast_to(gran, (8, LANES)).astype(BF16), to_len))
    return route, meta.astype(jnp.int32)


def _post_kernel(x_ref, ya_ref, of_ref, ob_ref, rb_ref, ga_ref, gb_ref, woa_ref, wob_ref, wout_ref,
                 gn_ref, nf_ref, wr_ref, br_ref, x1_ref, xn_ref, route_ref, meta_ref):
    nsub = x_ref.shape[0] // MOE_SUB
    subs = [slice(u * MOE_SUB, (u + 1) * MOE_SUB) for u in range(nsub)]
    val = [dict() for _ in range(nsub)]

    def branches(u):
        rows = subs[u]
        o = of_ref[rows, :].astype(F32) + ob_ref[rows, :].astype(F32)
        parts = []
        for h in range(GLA_HEADS):
            oh = o[:, h * GLA_DV:(h + 1) * GLA_DV]
            ms = jnp.mean(oh * oh, axis=-1, keepdims=True)
            parts.append(oh * lax.rsqrt(ms + EPS) * gn_ref[...])
        on = (jnp.concatenate(parts, axis=-1) * rb_ref[rows, :].astype(F32)).astype(BF16)
        val[u]["ya"] = _dot(ya_ref[rows, :], woa_ref[...])
        val[u]["yb"] = _dot(on, wob_ref[...])

    def merge(u):
        rows = subs[u]
        y = ga_ref[rows, :] * val[u]["ya"].astype(BF16) + gb_ref[rows, :] * val[u]["yb"].astype(BF16)
        val[u]["x1"] = x_ref[rows, :] + _dot(y, wout_ref[...])

    def norm_logits(u):
        rows = subs[u]
        x1 = val[u]["x1"]
        x1_ref[rows, :] = x1
        ms = jnp.mean(x1 * x1, axis=-1, keepdims=True)
        thi, tlo = _split_bf16(x1 * lax.rsqrt(ms + EPS) * nf_ref[...])
        xn_ref[rows, :] = thi
        hi = _dot(thi, wr_ref[...])
        val[u]["logits"] = hi[:, :LANES] + hi[:, LANES:] + _dot(tlo, wr_ref[:, :LANES]) + br_ref[...]

    def route(u):
        route_ref[subs[u], :], meta_ref[u] = _local_sort(*_route(val[u]["logits"]))

    stages = [branches, merge, norm_logits, route]
    for step in range(nsub + len(stages) - 1):
        for u in range(nsub):
            if 0 <= step - u < len(stages):
                stages[step - u](u)


def _post(x, ya, of, ob, rb, ga, gb, l, woa, wob, wout, gn, nf, wr, br, tm):
    t = x.shape[0]
    row = lambda w: pl.BlockSpec((tm, w), lambda i: (i, 0))
    consts = [woa, wob, wout, gn, nf, wr, br]
    return pl.pallas_call(
        _post_kernel,
        grid=(t // tm,),
        in_specs=[row(D_MODEL), row(NA_WIDTH), row(GLA_WIDTH), row(GLA_WIDTH), row(GLA_WIDTH),
                  row(D_MODEL), row(D_MODEL)] + [_const_spec(c, l) for c in consts],
        out_specs=[row(D_MODEL), row(D_MODEL), row(LANES),
                   pl.BlockSpec((tm // MOE_SUB, 8, LANES), lambda i: (i, 0, 0))],
        out_shape=[jax.ShapeDtypeStruct((t, D_MODEL), F32), jax.ShapeDtypeStruct((t, D_MODEL), BF16),
                   jax.ShapeDtypeStruct((t, LANES), F32),
                   jax.ShapeDtypeStruct((t // MOE_SUB, 8, LANES), jnp.int32)],
        compiler_params=pltpu.CompilerParams(dimension_semantics=("parallel",), vmem_limit_bytes=VMEM_LIMIT),
        name="post",
    )(x, ya, of, ob, rb, ga, gb, *consts)


def _moe_kernel(meta_ref, xn_ref, route_ref, x1_ref, wg_ref, wu_ref, wd_ref, o_ref, loc, stage, *, nsub):
    i = pl.program_id(0)
    step = pl.program_id(1)
    stage_spare = stage.shape[0] // MOE_GRAN - 1
    sub_rows = lambda u: slice(u * MOE_SUB, (u + 1) * MOE_SUB)
    gran_rows = lambda g: pl.ds(pl.multiple_of(g * MOE_GRAN, MOE_GRAN), MOE_GRAN)

    @pl.when(step == 0)
    def _dispatch():
        stage[...] = jnp.zeros_like(stage)
        sel = lax.broadcasted_iota(jnp.int32, (8, LANES), 0) == lax.broadcasted_iota(jnp.int32, (8, LANES), 1)
        sel = jnp.where(sel, 1.0, 0.0).astype(BF16)
        row_id = lax.broadcasted_iota(jnp.int32, (MOE_LOC_ROWS, MOE_SUB), 0).astype(F32)
        for u in range(nsub):
            rhi, rlo = _split_bf16(route_ref[sub_rows(u), :])
            pos_t = (lax.dot_general(sel, rhi, _NT, preferred_element_type=F32)
                     + lax.dot_general(sel, rlo, _NT, preferred_element_type=F32))
            onehot = jnp.where(row_id == pos_t[0:1], 1.0, 0.0) + jnp.where(row_id == pos_t[1:2], 1.0, 0.0)
            loc[u, :MOE_LOC_ROWS, :] = _dot(onehot.astype(BF16), xn_ref[sub_rows(u), :]).astype(BF16)
            loc[u, MOE_LOC_ROWS:, :] = jnp.zeros((MOE_LOC_SLACK, D_MODEL), BF16)

    def move_granules(e, to_stage):
        meta = lambda u, k: meta_ref[((i * nsub + u) * 2 + k) * N_EXPERTS + e]

        def move(u, off, dst, g, live=None):
            if to_stage:
                to = dst + g if live is None else jnp.where(live, dst + g, stage_spare)
                stage[gran_rows(to), :] = loc[u, gran_rows(off + g), :]
            else:
                to = off + g if live is None else jnp.where(live, off + g, MOE_LOC_ROWS // MOE_GRAN)
                loc[u, gran_rows(to), :] = stage[gran_rows(dst + g), :]

        runs = []
        dst = 0
        for u in range(nsub):
            off, n = meta(u, 0), meta(u, 1)
            for g in range(MOE_FAST_GRAN):
                move(u, off, dst, g, live=g < n)
            runs.append((u, off, n, dst))
            dst = dst + n

        @pl.when(functools.reduce(jnp.maximum, [n for _, _, n, _ in runs]) > MOE_FAST_GRAN)
        def _long_runs():
            for u, off, n, dst_u in runs:
                lax.fori_loop(MOE_FAST_GRAN, n, lambda g, c, u=u, off=off, dst_u=dst_u: (move(u, off, dst_u, g), c)[1], 0)

        return dst

    @pl.when(step < MOE_EXPERT_STEPS)
    def _experts():
        for k in range(MOE_EXPERTS_PER_STEP):
            e = step * MOE_EXPERTS_PER_STEP + k
            total = move_granules(e, True)

            def block(b, carry, k=k):
                rows = pl.ds(pl.multiple_of(b * MOE_ROW_BLOCK, MOE_ROW_BLOCK), MOE_ROW_BLOCK)
                xs = stage[rows, :]
                hg = _dot(xs, wg_ref[k])
                h = (hg * _sigmoid(hg) * _dot(xs, wu_ref[k])).astype(BF16)
                stage[rows, :] = _dot(h, wd_ref[k]).astype(BF16)
                return carry

            lax.fori_loop(0, (total * MOE_GRAN + MOE_ROW_BLOCK - 1) // MOE_ROW_BLOCK, block, 0)
            move_granules(e, False)

    @pl.when(step >= MOE_EXPERT_STEPS)
    def _combine():
        col_id = lax.broadcasted_iota(jnp.int32, (MOE_SUB, MOE_LOC_ROWS), 1).astype(F32)
        per_slice = nsub // MOE_COMBINE_STEPS
        for j in range(per_slice):
            u = (step - MOE_EXPERT_STEPS) * per_slice + j
            r = route_ref[pl.ds(pl.multiple_of(u * MOE_SUB, MOE_SUB), MOE_SUB), :]
            pick = jnp.where(col_id == r[:, 0:1], r[:, 2:3], jnp.where(col_id == r[:, 1:2], r[:, 3:4], 0.0))
            o_ref[sub_rows(j), :] = x1_ref[sub_rows(j), :] + _dot(pick.astype(BF16), loc[u, :MOE_LOC_ROWS, :])


def _moe(meta, xn, route, x1, l, wg, wu, wd, tb):
    t = xn.shape[0]
    nsub = tb // MOE_SUB
    assert nsub % MOE_COMBINE_STEPS == 0
    row = lambda w, **kw: pl.BlockSpec((tb, w), lambda i, s, m: (i, 0), **kw)
    sliced = pl.BlockSpec((tb // MOE_COMBINE_STEPS, D_MODEL),
                          lambda i, s, m: (i * MOE_COMBINE_STEPS + jnp.maximum(s - MOE_EXPERT_STEPS, 0), 0))
    wspec = lambda a, b: pl.BlockSpec((None, MOE_EXPERTS_PER_STEP, a, b),
                                      lambda i, s, m: (l, jnp.minimum(s, MOE_EXPERT_STEPS - 1), 0, 0))
    return pl.pallas_call(
        functools.partial(_moe_kernel, nsub=nsub),
        grid_spec=pltpu.PrefetchScalarGridSpec(
            num_scalar_prefetch=1,
            grid=(t // tb, MOE_EXPERT_STEPS + MOE_COMBINE_STEPS),
            in_specs=[row(D_MODEL, pipeline_mode=pl.Buffered(1)), row(LANES), sliced,
                      wspec(D_MODEL, D_EXPERT), wspec(D_MODEL, D_EXPERT), wspec(D_EXPERT, D_MODEL)],
            out_specs=sliced,
            scratch_shapes=[pltpu.VMEM((nsub, MOE_LOC_ROWS + MOE_LOC_SLACK, D_MODEL), BF16),
                            pltpu.VMEM((pl.cdiv(tb + MOE_LOC_SLACK, MOE_ROW_BLOCK) * MOE_ROW_BLOCK, D_MODEL), BF16)]),
        out_shape=jax.ShapeDtypeStruct((t, D_MODEL), F32),
        compiler_params=pltpu.CompilerParams(dimension_semantics=("parallel", "arbitrary"),
                                             vmem_limit_bytes=VMEM_LIMIT),
        name="moe",
    )(meta, xn, route, x1, wg, wu, wd)


def _prep_params(norm_mix, w_in, q_norm_a, k_norm_a, rpb, w_decay, b_decay, gla_norm, w_o_a, w_o_b, w_out,
                 norm_ffn, w_router_g, b_router_g, w_router_e, b_router_e, w_gate, w_up, w_down):
    depth = w_in.shape[0]
    f32 = lambda a: a.astype(F32)
    pad_lanes = lambda a, before, total: jnp.pad(f32(a), [(0, 0)] * (a.ndim - 1) + [(before, total - before - a.shape[-1])])
    w_aligned = jnp.concatenate([w_in[:, :, :IN_Z_COL0 + 2 * GLA_RANK],
                                 jnp.zeros((depth, D_MODEL, IN_GATE_COL0 - IN_Z_COL0 - 2 * GLA_RANK), w_in.dtype),
                                 w_in[:, :, IN_Z_COL0 + 2 * GLA_RANK:]], axis=-1).astype(BF16)
    w_router = pad_lanes(jnp.concatenate([w_router_g, w_router_e], axis=-1), ROUTER_GROUP_LANE0, LANES)
    b_router = pad_lanes(jnp.concatenate([b_router_g, b_router_e], axis=-1), ROUTER_GROUP_LANE0, LANES)
    w_router = jnp.concatenate(_split_bf16(w_router), axis=-1)
    wdec = jnp.stack([jnp.pad(f32(w_decay[:, d]), ((0, 0), (d * GLA_RANK, LANES - (d + 1) * GLA_RANK), (0, 0)))
                      for d in range(2)], axis=1).astype(BF16)
    return dict(
        in_proj=(f32(norm_mix)[:, None, :], w_aligned,
                 jnp.tile(f32(q_norm_a), (1, NA_HEADS))[:, None, :] * (NA_HEAD_DIM ** -0.5 * LOG2E),
                 jnp.tile(f32(k_norm_a), (1, NA_HEADS))[:, None, :], wdec, f32(b_decay)[:, :, None, :]),
        na=(_na_bias_table(rpb),),
        post=(w_o_a.astype(BF16), w_o_b.astype(BF16), w_out.astype(BF16), f32(gla_norm)[:, None, :],
              f32(norm_ffn)[:, None, :], w_router, b_router[:, None, :]),
        moe=(w_gate.astype(BF16), w_up.astype(BF16), w_down.astype(BF16)),
    )


def _tile(t, want):
    while t % want:
        want //= 2
    return want


def _trunk(x, p, depth):
    batch, n, _ = x.shape
    t = batch * n
    x = x.reshape(t, D_MODEL)
    for l in range(depth):
        qa, ka, va, qb, kb, vb, rb, gf, gr, ga, gb = _in_proj(x, l, *p["in_proj"], _tile(t, 512))
        ya = _na(qa, ka, va, l, *p["na"], batch, n)
        of, ob = _gla(qb, kb, vb, gf, gr, batch, n, GLA_BLOCK_TOK)
        x1, xn, route, meta = _post(x, ya, of, ob, rb, ga, gb, l, *p["post"], _tile(t, 1024))
        meta = meta[:, 0, :2 * N_EXPERTS].reshape(-1)
        x = _moe(meta, xn, route, x1, l, *p["moe"], _tile(t, MOE_TILE))
    return x.reshape(batch, n, D_MODEL)


def kernel(x_prompt, x_sample, norm_mix, w_in, q_norm_a, k_norm_a, rpb, w_decay, b_decay, gla_norm, w_o_a, w_o_b,
           w_out, norm_ffn, w_router_g, b_router_g, w_router_e, b_router_e, w_gate, w_up, w_down):
    p = _prep_params(norm_mix, w_in, q_norm_a, k_norm_a, rpb, w_decay, b_decay, gla_norm, w_o_a, w_o_b, w_out,
                     norm_ffn, w_router_g, b_router_g, w_router_e, b_router_e, w_gate, w_up, w_down)
    depth = w_in.shape[0]
    return (_trunk(x_prompt, p, depth), _trunk(x_sample, p, depth))
```

```python
import functools

import jax
import jax.numpy as jnp
import numpy as np
from jax import lax
from jax.experimental import pallas as pl
from jax.experimental.pallas import tpu as pltpu

F32 = jnp.float32
BF16 = jnp.bfloat16

D_MODEL = 1024
GRID_W = 64
NA_HEADS = 8
NA_HEAD_DIM = 64
NA_WIDTH = NA_HEADS * NA_HEAD_DIM
NA_ROWS = 8
NA_COLS = 16
RPB_ROWS = 2 * NA_ROWS - 1
RPB_COLS = 2 * NA_COLS - 1
GLA_HEADS = 4
GLA_DK = 128
GLA_DV = 128
GLA_WIDTH = GLA_HEADS * GLA_DV
GLA_RANK = 16
GLA_GATE_NORM = 16.0
GLA_CHUNK = 64
GLA_BLOCK_TOK = 1024
N_GROUPS = 4
EXPERTS_PER_GROUP = 4
N_EXPERTS = N_GROUPS * EXPERTS_PER_GROUP
D_EXPERT = 512
EPS = 1e-6
NEG = -1e30

LANES = 128
IN_Z_COL0 = 7 * 512
IN_GATE_COL0 = IN_Z_COL0 + LANES
MOE_SUB = 256
MOE_GRAN = 16
MOE_LOC_ROWS = 768
MOE_TILE = 2048
MOE_FAST_GRAN = 4
MOE_LOC_SLACK = (MOE_FAST_GRAN + 1) * MOE_GRAN
MOE_EXPERTS_PER_STEP = 2
MOE_EXPERT_STEPS = N_EXPERTS // MOE_EXPERTS_PER_STEP
MOE_COMBINE_STEPS = 2
MOE_ROW_BLOCK = 352
LOG2E = 1.4426950408889634
EXP_CLAMP = 80.0
ROUTER_GROUP_LANE0 = 0
ROUTER_EXPERT_LANE0 = N_GROUPS
VMEM_LIMIT = 56 * 1024 * 1024

_NT = (((1,), (1,)), ((), ()))
_TN = (((0,), (0,)), ((), ()))


def _dot(a, b):
    return jnp.dot(a, b, preferred_element_type=F32)


def _split_bf16(x):
    hi = x.astype(BF16)
    lo = (x - hi.astype(F32)).astype(BF16)
    return hi, lo


def _sigmoid(x):
    return 1.0 / (1.0 + jnp.exp(-x))


def _const_spec(stacked, l):
    tail = stacked.shape[1:]
    return pl.BlockSpec((None,) + tail, lambda *_: (l,) + (0,) * len(tail), pipeline_mode=pl.Buffered(1))


def _in_proj_kernel(x_ref, nrm_ref, w_ref, qg_ref, kg_ref, wdec_ref, bdec_ref,
                    qa_ref, ka_ref, va_ref, qb_ref, kb_ref, vb_ref, rb_ref, gf_ref, gr_ref, ga_ref, gb_ref):
    x = x_ref[...]
    ms = jnp.mean(x * x, axis=-1, keepdims=True)
    xn = (x * lax.rsqrt(ms + EPS) * nrm_ref[...]).astype(BF16)
    low_half = lax.broadcasted_iota(jnp.int32, (x.shape[0], LANES), 1) < NA_HEAD_DIM

    def proj(lo, width=512):
        return _dot(xn, w_ref[:, lo:lo + width])

    def head_norm(a, gain_ref):
        parts = []
        for p in range(NA_WIDTH // LANES):
            sq = a[:, p * LANES:(p + 1) * LANES]
            sq = sq * sq
            s_lo = jnp.sum(jnp.where(low_half, sq, 0.0), axis=-1, keepdims=True)
            s_hi = jnp.sum(jnp.where(low_half, 0.0, sq), axis=-1, keepdims=True)
            parts.append(jnp.where(low_half, s_lo, s_hi))
        ss = jnp.concatenate(parts, axis=-1)
        return a * lax.rsqrt(ss * (1.0 / NA_HEAD_DIM) + EPS) * gain_ref[...]

    z = proj(IN_Z_COL0, LANES).astype(BF16)
    ga_ref[...] = _sigmoid(proj(IN_GATE_COL0, D_MODEL)).astype(BF16)
    gb_ref[...] = _sigmoid(proj(IN_GATE_COL0 + D_MODEL, D_MODEL)).astype(BF16)
    for d, g_ref in enumerate((gf_ref, gr_ref)):
        lg = _dot(z, wdec_ref[d]) + bdec_ref[d]
        g_ref[...] = (jnp.minimum(lg, 0.0) - jnp.log(1.0 + jnp.exp(-jnp.abs(lg)))) * (1.0 / GLA_GATE_NORM)
    r = proj(3072)
    rb_ref[...] = (r * _sigmoid(r)).astype(BF16)
    qa = proj(0)
    ka = proj(512)
    qb_ref[...] = (proj(1536) * (GLA_DK ** -0.5)).astype(BF16)
    qa_ref[...] = head_norm(qa, qg_ref).astype(BF16)
    kb_ref[...] = proj(2048).astype(BF16)
    ka_ref[...] = head_norm(ka, kg_ref).astype(BF16)
    va_ref[...] = proj(1024).astype(BF16)
    vb_ref[...] = proj(2560).astype(BF16)


def _in_proj(x, l, nrm, w, qg, kg, wdec, bdec, tm):
    t = x.shape[0]
    row = lambda w: pl.BlockSpec((tm, w), lambda i: (i, 0))
    out_w = [512] * 7 + [GLA_WIDTH, GLA_WIDTH, D_MODEL, D_MODEL]
    out_dt = [BF16] * 7 + [F32, F32, BF16, BF16]
    consts = [nrm, w, qg, kg, wdec, bdec]
    return pl.pallas_call(
        _in_proj_kernel,
        grid=(t // tm,),
        in_specs=[row(D_MODEL)] + [_const_spec(c, l) for c in consts],
        out_specs=[row(w) for w in out_w],
        out_shape=[jax.ShapeDtypeStruct((t, w), dt) for w, dt in zip(out_w, out_dt)],
        compiler_params=pltpu.CompilerParams(dimension_semantics=("parallel",), vmem_limit_bytes=VMEM_LIMIT),
        name="in_proj",
    )(x, *consts)


NA_BLOCK_ROWS = 16
NA_BLOCK_TOK = NA_BLOCK_ROWS * GRID_W
NA_WINDOW_TOK = NA_ROWS * GRID_W
NA_HALO_TOK = NA_BLOCK_TOK + 2 * NA_WINDOW_TOK


def _na_halo_start(i, n):
    return jnp.clip(i * NA_BLOCK_TOK - NA_WINDOW_TOK, 0, n - min(NA_HALO_TOK, n))


def _na_kernel(q_ref, kwin, vwin, bias_ref, o_ref, s_scr, *, rows):
    i = pl.program_id(1)
    halo0 = _na_halo_start(i, rows * GRID_W)
    low_half = lax.broadcasted_iota(jnp.int32, (GRID_W, LANES), 1) < NA_HEAD_DIM
    pairs = range(NA_HEADS // 2)
    cols = lambda p: slice(p * LANES, (p + 1) * LANES)
    qrows = lambda j: slice(j * GRID_W, (j + 1) * GRID_W)

    def window(j):
        r = i * NA_BLOCK_ROWS + j
        rs = jnp.clip(r - NA_ROWS // 2, 0, rows - NA_ROWS)
        start = pl.multiple_of(rs * GRID_W - halo0, GRID_W)
        return start, rs - r + NA_ROWS - 1

    def scores(j):
        start, bias_off = window(j)
        for p in pairs:
            qp = q_ref[qrows(j), cols(p)]
            zero = jnp.zeros_like(qp)
            q2 = jnp.concatenate([jnp.where(low_half, qp, zero), jnp.where(low_half, zero, qp)], axis=0)
            s = lax.dot_general(q2, kwin[pl.ds(start, NA_WINDOW_TOK), cols(p)], _NT, preferred_element_type=F32)
            s_scr[j % 2, p] = s + bias_ref[bias_off, p]

    def attend(j):
        start, _ = window(j)
        for p in pairs:
            s = s_scr[j % 2, p]
            e = jnp.exp2(s - jnp.max(s, axis=-1, keepdims=True))
            l = jnp.sum(e, axis=-1, keepdims=True)
            pv = _dot(e.astype(BF16), vwin[pl.ds(start, NA_WINDOW_TOK), cols(p)]) * (1.0 / l)
            o_ref[qrows(j), cols(p)] = jnp.where(low_half, pv[:GRID_W], pv[GRID_W:]).astype(BF16)

    scores(0)
    for j in range(NA_BLOCK_ROWS):
        if j + 1 < NA_BLOCK_ROWS:
            scores(j + 1)
        attend(j)


def _na(q, k, v, l, bias, batch, n):
    assert n % NA_BLOCK_TOK == 0, "sequence length must be a whole number of 8-row query blocks"
    rows = n // GRID_W
    nblk = rows // NA_BLOCK_ROWS
    cur = pl.BlockSpec((NA_BLOCK_TOK, NA_WIDTH), lambda b, i: (b * nblk + i, 0))
    halo = pl.BlockSpec((pl.Element(min(NA_HALO_TOK, n)), pl.Element(NA_WIDTH)),
                        lambda b, i: (pl.multiple_of(b * n + _na_halo_start(i, n), NA_WINDOW_TOK), 0))
    return pl.pallas_call(
        functools.partial(_na_kernel, rows=rows),
        grid=(batch, nblk),
        in_specs=[cur, halo, halo, _const_spec(bias, l)],
        out_specs=cur,
        out_shape=jax.ShapeDtypeStruct((batch * n, NA_WIDTH), BF16),
        scratch_shapes=[pltpu.VMEM((2, NA_HEADS // 2, 2 * GRID_W, NA_WINDOW_TOK), F32)],
        compiler_params=pltpu.CompilerParams(dimension_semantics=("parallel", "parallel"),
                                             vmem_limit_bytes=VMEM_LIMIT),
        name="na",
    )(q, k, v, bias)


def _na_bias_table(rpb):
    cols = np.arange(GRID_W)
    col_start = np.clip(cols - NA_COLS // 2, 0, GRID_W - NA_COLS)
    col_mask = (cols[None, :] >= col_start[:, None]) & (cols[None, :] < col_start[:, None] + NA_COLS)
    col_idx = np.clip(cols[None, :] - cols[:, None] + NA_COLS - 1, 0, RPB_COLS - 1)
    pick_col = (col_idx[None] == np.arange(RPB_COLS)[:, None, None]).astype(np.float32)
    row = np.arange(RPB_ROWS)[:, None, None]
    pick_row = (row == np.arange(NA_ROWS)[:, None] + np.arange(NA_ROWS)[None, :]).astype(np.float32)
    bias = jnp.einsum("lhrk,kqc->lhrqc", rpb.astype(F32) * LOG2E, pick_col, precision=lax.Precision.HIGHEST)
    bias = jnp.einsum("lhrqc,roj->lohqjc", bias, pick_row, precision=lax.Precision.HIGHEST)
    masked = jnp.where(col_mask[:, None, :], bias, NEG)
    return masked.reshape(rpb.shape[0], NA_ROWS, NA_HEADS // 2, 2 * GRID_W, NA_WINDOW_TOK)


def _gla_chunk_rows(s):
    return slice(s * GLA_CHUNK, (s + 1) * GLA_CHUNK)


def _gla_head_cols(h):
    return slice(h * GLA_DK, (h + 1) * GLA_DK)


def _gla_cumulate(d, g, nsub):
    tri = jnp.where(d["mask"], 1.0, 0.0).astype(BF16)
    ghi, glo = _split_bf16(g)
    return [_dot(tri, ghi[_gla_chunk_rows(s)]) + _dot(tri, glo[_gla_chunk_rows(s)]) for s in range(nsub)]


def _gla_rescale(d, b_chunks):
    mid, end = d["mid"], d["end"]
    scaled = []
    for s, b in enumerate(b_chunks):
        rows = _gla_chunk_rows(s)
        b_mid = b[mid:mid + 1, :]
        b_end = b[end:end + 1, :]
        qe = d["q"][rows, :] * jnp.exp(jnp.minimum(b - b_mid, EXP_CLAMP)).astype(BF16)
        ke = d["k"][rows, :] * jnp.exp(jnp.minimum(b_mid - b, EXP_CLAMP)).astype(BF16)
        d["qi"][rows, :] = qe * jnp.exp(b_mid).astype(BF16)
        d["dec"][s] = jnp.broadcast_to(jnp.exp(b_end), (8, GLA_WIDTH))
        k_out = ke * jnp.exp(b_end - b_mid).astype(BF16)
        scaled.append((qe, ke, k_out))
    return scaled


def _gla_products(d, scaled):
    pairs = [(s, h) for s in range(len(scaled)) for h in range(GLA_HEADS)]
    v = lambda s, h: d["v"][_gla_chunk_rows(s), _gla_head_cols(h)]
    a = [lax.dot_general(scaled[s][0][:, _gla_head_cols(h)], scaled[s][1][:, _gla_head_cols(h)], _NT,
                         preferred_element_type=F32) for s, h in pairs]
    for s, h in pairs:
        d["kv"][s, h] = lax.dot_general(v(s, h), scaled[s][2][:, _gla_head_cols(h)], _TN,
                                        preferred_element_type=F32)
    for (s, h), a_sh in zip(pairs, a):
        d["oi"][_gla_chunk_rows(s), _gla_head_cols(h)] = _dot(jnp.where(d["mask"], a_sh, 0.0).astype(BF16), v(s, h))


def _gla_scan_step(d, s):
    rows = _gla_chunk_rows(s)
    dec = d["dec"][s]
    states = [d["st"][h] for h in range(GLA_HEADS)]
    inter = [lax.dot_general(d["qi"][rows, _gla_head_cols(h)], states[h].astype(BF16), _NT,
                             preferred_element_type=F32) for h in range(GLA_HEADS)]
    for h in range(GLA_HEADS):
        cols = _gla_head_cols(h)
        d["st"][h] = states[h] * dec[0:1, cols] + d["kv"][s, h]
        d["o"][rows, cols] = (d["oi"][rows, cols] + inter[h]).astype(BF16)


def _gla_kernel(qf_ref, kf_ref, vf_ref, gf_ref, qr_ref, kr_ref, vr_ref, gr_ref,
                of_ref, ob_ref, stf_ref, stb_ref, oif_s, oib_s, qif_s, qib_s, kvf_s, kvb_s, decf_s, decb_s, *, nsub):
    @pl.when(pl.program_id(1) == 0)
    def _():
        stf_ref[...] = jnp.zeros_like(stf_ref)
        stb_ref[...] = jnp.zeros_like(stb_ref)

    c = GLA_CHUNK
    ri = lax.broadcasted_iota(jnp.int32, (c, c), 0)
    ci = lax.broadcasted_iota(jnp.int32, (c, c), 1)
    fwd = dict(q=qf_ref, k=kf_ref, v=vf_ref, g=gf_ref, o=of_ref, st=stf_ref, oi=oif_s,
               qi=qif_s, kv=kvf_s, dec=decf_s, mask=ci <= ri, mid=c // 2 - 1, end=c - 1)
    bwd = dict(q=qr_ref, k=kr_ref, v=vr_ref, g=gr_ref, o=ob_ref, st=stb_ref, oi=oib_s,
               qi=qib_s, kv=kvb_s, dec=decb_s, mask=ci >= ri, mid=c // 2, end=0)
    b = [_gla_cumulate(d, d["g"][...], nsub) for d in (fwd, bwd)]
    scaled = [_gla_rescale(d, b_d) for d, b_d in zip((fwd, bwd), b)]
    for d, scaled_d in zip((fwd, bwd), scaled):
        _gla_products(d, scaled_d)
    for s in range(nsub):
        _gla_scan_step(fwd, s)
        _gla_scan_step(bwd, nsub - 1 - s)


def _gla(q, k, v, gf, gr, batch, n, cb):
    assert n % cb == 0, "sequence length must be a whole number of GLA blocks"
    nb = n // cb
    nsub = cb // GLA_CHUNK
    fwd = lambda w: pl.BlockSpec((cb, w), lambda b, c: (b * nb + c, 0))
    bwd = lambda w: pl.BlockSpec((cb, w), lambda b, c: (b * nb + nb - 1 - c, 0))
    out = jax.ShapeDtypeStruct((batch * n, GLA_WIDTH), BF16)
    state = pltpu.VMEM((GLA_HEADS, GLA_DV, GLA_DK), F32)
    return pl.pallas_call(
        functools.partial(_gla_kernel, nsub=nsub),
        grid=(batch, nb),
        in_specs=[fwd(GLA_WIDTH)] * 4 + [bwd(GLA_WIDTH)] * 4,
        out_specs=[fwd(GLA_WIDTH), bwd(GLA_WIDTH)],
        out_shape=[out, out],
        scratch_shapes=[state, state,
                        pltpu.VMEM((cb, GLA_WIDTH), F32), pltpu.VMEM((cb, GLA_WIDTH), F32),
                        pltpu.VMEM((cb, GLA_WIDTH), BF16), pltpu.VMEM((cb, GLA_WIDTH), BF16),
                        pltpu.VMEM((nsub, GLA_HEADS, GLA_DV, GLA_DK), F32),
                        pltpu.VMEM((nsub, GLA_HEADS, GLA_DV, GLA_DK), F32),
                        pltpu.VMEM((nsub, 8, GLA_WIDTH), F32), pltpu.VMEM((nsub, 8, GLA_WIDTH), F32)],
        compiler_params=pltpu.CompilerParams(dimension_semantics=("parallel", "arbitrary"),
                                             vmem_limit_bytes=VMEM_LIMIT),
        name="gla",
    )(q, k, v, gf, q, k, v, gr)


def _route(logits):
    lane = lax.broadcasted_iota(jnp.int32, logits.shape, 1)
    lane_f = lane.astype(F32)
    ninf = -jnp.inf

    def first_argmax(vals, vmax):
        return jnp.min(jnp.where(vals == vmax, lane_f, float(LANES)), axis=-1, keepdims=True)

    lg = jnp.where(lane < N_GROUPS, logits, ninf)
    g_max = jnp.max(lg, axis=-1, keepdims=True)
    p_sel = 1.0 / jnp.sum(jnp.exp(lg - g_max), axis=-1, keepdims=True)
    grp = first_argmax(lg, g_max)
    e_lo = ROUTER_EXPERT_LANE0 + EXPERTS_PER_GROUP * grp
    le = jnp.where((lane_f >= e_lo) & (lane_f < e_lo + EXPERTS_PER_GROUP), logits, ninf)
    l1 = jnp.max(le, axis=-1, keepdims=True)
    i1 = first_argmax(le, l1)
    le2 = jnp.where(lane_f == i1, ninf, le)
    l2 = jnp.max(le2, axis=-1, keepdims=True)
    i2 = first_argmax(le2, l2)
    t = jnp.exp(l2 - l1)
    w1 = p_sel / (1.0 + t)
    w2 = w1 * t
    return lane_f == i1, lane_f == i2, w1, w2


def _local_sort(sel1, sel2, w1, w2):
    n = sel1.shape[0]
    ri = lax.broadcasted_iota(jnp.int32, (n, n), 0)
    ci = lax.broadcasted_iota(jnp.int32, (n, n), 1)
    before = jnp.where(ci < ri, 1.0, 0.0).astype(BF16)
    li = lax.broadcasted_iota(jnp.int32, (LANES, LANES), 0)
    lj = lax.broadcasted_iota(jnp.int32, (LANES, LANES), 1)
    lower_lanes = jnp.where(li < lj, 1.0, 0.0).astype(BF16)
    oh1 = jnp.where(sel1, 1.0, 0.0)
    oh2 = jnp.where(sel2, 1.0, 0.0)
    cnt1 = jnp.sum(oh1, axis=0, keepdims=True)
    cnt = cnt1 + jnp.sum(oh2, axis=0, keepdims=True)
    gran = jnp.floor((cnt + (MOE_GRAN - 1)) * (1.0 / MOE_GRAN))
    offg = _dot(jnp.broadcast_to(gran, (8, LANES)).astype(BF16), lower_lanes)[0:1]
    pick = lambda sel, vals: jnp.sum(jnp.where(sel, vals, 0.0), axis=-1, keepdims=True)
    pos1 = pick(sel1, _dot(before, oh1.astype(BF16)) + MOE_GRAN * offg)
    pos2 = pick(sel2, _dot(before, oh2.astype(BF16)) + cnt1 + MOE_GRAN * offg)
    lane = lax.broadcasted_iota(jnp.int32, (n, LANES), 1)
    route = jnp.where(lane == 0, pos1, jnp.where(lane == 1, pos2, jnp.where(lane == 2, w1,
                                                                             jnp.where(lane == 3, w2, 0.0))))
    to_off = jnp.where((li == lj + ROUTER_EXPERT_LANE0) & (lj < N_EXPERTS), 1.0, 0.0).astype(BF16)
    to_len = jnp.where(li == lj + ROUTER_EXPERT_LANE0 - N_EXPERTS, 1.0, 0.0).astype(BF16)
    meta = (_dot(jnp.broadcast_to(offg, (8, LANES)).astype(BF16), to_off)
            + _dot(jnp.broadcast_to(gran, (8, LANES)).astype(BF16), to_len))
    return route, meta.astype(jnp.int32)


def _post_kernel(x_ref, ya_ref, of_ref, ob_ref, rb_ref, ga_ref, gb_ref, woa_ref, wob_ref, wout_ref,
                 gn_ref, nf_ref, wr_ref, br_ref, x1_ref, xn_ref, route_ref, meta_ref):
    nsub = x_ref.shape[0] // MOE_SUB
    subs = [slice(u * MOE_SUB, (u + 1) * MOE_SUB) for u in range(nsub)]
    val = [dict() for _ in range(nsub)]

    def branches(u):
        rows = subs[u]
        o = of_ref[rows, :].astype(F32) + ob_ref[rows, :].astype(F32)
        parts = []
        for h in range(GLA_HEADS):
            oh = o[:, h * GLA_DV:(h + 1) * GLA_DV]
            ms = jnp.mean(oh * oh, axis=-1, keepdims=True)
            parts.append(oh * lax.rsqrt(ms + EPS) * gn_ref[...])
        on = (jnp.concatenate(parts, axis=-1) * rb_ref[rows, :].astype(F32)).astype(BF16)
        val[u]["ya"] = _dot(ya_ref[rows, :], woa_ref[...])
        val[u]["yb"] = _dot(on, wob_ref[...])

    def merge(u):
        rows = subs[u]
        y = ga_ref[rows, :] * val[u]["ya"].astype(BF16) + gb_ref[rows, :] * val[u]["yb"].astype(BF16)
        val[u]["x1"] = x_ref[rows, :] + _dot(y, wout_ref[...])

    def norm_logits(u):
        rows = subs[u]
        x1 = val[u]["x1"]
        x1_ref[rows, :] = x1
        ms = jnp.mean(x1 * x1, axis=-1, keepdims=True)
        thi, tlo = _split_bf16(x1 * lax.rsqrt(ms + EPS) * nf_ref[...])
        xn_ref[rows, :] = thi
        hi = _dot(thi, wr_ref[...])
        val[u]["logits"] = hi[:, :LANES] + hi[:, LANES:] + _dot(tlo, wr_ref[:, :LANES]) + br_ref[...]

    def route(u):
        route_ref[subs[u], :], meta_ref[u] = _local_sort(*_route(val[u]["logits"]))

    stages = [branches, merge, norm_logits, route]
    for step in range(nsub + len(stages) - 1):
        for u in range(nsub):
            if 0 <= step - u < len(stages):
                stages[step - u](u)


def _post(x, ya, of, ob, rb, ga, gb, l, woa, wob, wout, gn, nf, wr, br, tm):
    t = x.shape[0]
    row = lambda w: pl.BlockSpec((tm, w), lambda i: (i, 0))
    consts = [woa, wob, wout, gn, nf, wr, br]
    return pl.pallas_call(
        _post_kernel,
        grid=(t // tm,),
        in_specs=[row(D_MODEL), row(NA_WIDTH), row(GLA_WIDTH), row(GLA_WIDTH), row(GLA_WIDTH),
                  row(D_MODEL), row(D_MODEL)] + [_const_spec(c, l) for c in consts],
        out_specs=[row(D_MODEL), row(D_MODEL), row(LANES),
                   pl.BlockSpec((tm // MOE_SUB, 8, LANES), lambda i: (i, 0, 0))],
        out_shape=[jax.ShapeDtypeStruct((t, D_MODEL), F32), jax.ShapeDtypeStruct((t, D_MODEL), BF16),
                   jax.ShapeDtypeStruct((t, LANES), F32),
                   jax.ShapeDtypeStruct((t // MOE_SUB, 8, LANES), jnp.int32)],
        compiler_params=pltpu.CompilerParams(dimension_semantics=("parallel",), vmem_limit_bytes=VMEM_LIMIT),
        name="post",
    )(x, ya, of, ob, rb, ga, gb, *consts)


def _moe_kernel(meta_ref, xn_ref, route_ref, x1_ref, wg_ref, wu_ref, wd_ref, o_ref, loc, stage, *, nsub):
    i = pl.program_id(0)
    step = pl.program_id(1)
    stage_spare = stage.shape[0] // MOE_GRAN - 1
    sub_rows = lambda u: slice(u * MOE_SUB, (u + 1) * MOE_SUB)
    gran_rows = lambda g: pl.ds(pl.multiple_of(g * MOE_GRAN, MOE_GRAN), MOE_GRAN)

    @pl.when(step == 0)
    def _dispatch():
        stage[...] = jnp.zeros_like(stage)
        sel = lax.broadcasted_iota(jnp.int32, (8, LANES), 0) == lax.broadcasted_iota(jnp.int32, (8, LANES), 1)
        sel = jnp.where(sel, 1.0, 0.0).astype(BF16)
        row_id = lax.broadcasted_iota(jnp.int32, (MOE_LOC_ROWS, MOE_SUB), 0).astype(F32)
        for u in range(nsub):
            rhi, rlo = _split_bf16(route_ref[sub_rows(u), :])
            pos_t = (lax.dot_general(sel, rhi, _NT, preferred_element_type=F32)
                     + lax.dot_general(sel, rlo, _NT, preferred_element_type=F32))
            onehot = jnp.where(row_id == pos_t[0:1], 1.0, 0.0) + jnp.where(row_id == pos_t[1:2], 1.0, 0.0)
            loc[u, :MOE_LOC_ROWS, :] = _dot(onehot.astype(BF16), xn_ref[sub_rows(u), :]).astype(BF16)
            loc[u, MOE_LOC_ROWS:, :] = jnp.zeros((MOE_LOC_SLACK, D_MODEL), BF16)

    def move_granules(e, to_stage):
        meta = lambda u, k: meta_ref[((i * nsub + u) * 2 + k) * N_EXPERTS + e]

        def move(u, off, dst, g, live=None):
            if to_stage:
                to = dst + g if live is None else jnp.where(live, dst + g, stage_spare)
                stage[gran_rows(to), :] = loc[u, gran_rows(off + g), :]
            else:
                to = off + g if live is None else jnp.where(live, off + g, MOE_LOC_ROWS // MOE_GRAN)
                loc[u, gran_rows(to), :] = stage[gran_rows(dst + g), :]

        runs = []
        dst = 0
        for u in range(nsub):
            off, n = meta(u, 0), meta(u, 1)
            for g in range(MOE_FAST_GRAN):
                move(u, off, dst, g, live=g < n)
            runs.append((u, off, n, dst))
            dst = dst + n

        @pl.when(functools.reduce(jnp.maximum, [n for _, _, n, _ in runs]) > MOE_FAST_GRAN)
        def _long_runs():
            for u, off, n, dst_u in runs:
                lax.fori_loop(MOE_FAST_GRAN, n, lambda g, c, u=u, off=off, dst_u=dst_u: (move(u, off, dst_u, g), c)[1], 0)

        return dst

    @pl.when(step < MOE_EXPERT_STEPS)
    def _experts():
        for k in range(MOE_EXPERTS_PER_STEP):
            e = step * MOE_EXPERTS_PER_STEP + k
            total = move_granules(e, True)

            def block(b, carry, k=k):
                rows = pl.ds(pl.multiple_of(b * MOE_ROW_BLOCK, MOE_ROW_BLOCK), MOE_ROW_BLOCK)
                xs = stage[rows, :]
                hg = _dot(xs, wg_ref[k])
                h = (hg * _sigmoid(hg) * _dot(xs, wu_ref[k])).astype(BF16)
                stage[rows, :] = _dot(h, wd_ref[k]).astype(BF16)
                return carry

            lax.fori_loop(0, (total * MOE_GRAN + MOE_ROW_BLOCK - 1) // MOE_ROW_BLOCK, block, 0)
            move_granules(e, False)

    @pl.when(step >= MOE_EXPERT_STEPS)
    def _combine():
        col_id = lax.broadcasted_iota(jnp.int32, (MOE_SUB, MOE_LOC_ROWS), 1).astype(F32)
        per_slice = nsub // MOE_COMBINE_STEPS
        for j in range(per_slice):
            u = (step - MOE_EXPERT_STEPS) * per_slice + j
            r = route_ref[pl.ds(pl.multiple_of(u * MOE_SUB, MOE_SUB), MOE_SUB), :]
            pick = jnp.where(col_id == r[:, 0:1], r[:, 2:3], jnp.where(col_id == r[:, 1:2], r[:, 3:4], 0.0))
            o_ref[sub_rows(j), :] = x1_ref[sub_rows(j), :] + _dot(pick.astype(BF16), loc[u, :MOE_LOC_ROWS, :])


def _moe(meta, xn, route, x1, l, wg, wu, wd, tb):
    t = xn.shape[0]
    nsub = tb // MOE_SUB
    assert nsub % MOE_COMBINE_STEPS == 0
    row = lambda w, **kw: pl.BlockSpec((tb, w), lambda i, s, m: (i, 0), **kw)
    sliced = pl.BlockSpec((tb // MOE_COMBINE_STEPS, D_MODEL),
                          lambda i, s, m: (i * MOE_COMBINE_STEPS + jnp.maximum(s - MOE_EXPERT_STEPS, 0), 0))
    wspec = lambda a, b: pl.BlockSpec((None, MOE_EXPERTS_PER_STEP, a, b),
                                      lambda i, s, m: (l, jnp.minimum(s, MOE_EXPERT_STEPS - 1), 0, 0))
    return pl.pallas_call(
        functools.partial(_moe_kernel, nsub=nsub),
        grid_spec=pltpu.PrefetchScalarGridSpec(
            num_scalar_prefetch=1,
            grid=(t // tb, MOE_EXPERT_STEPS + MOE_COMBINE_STEPS),
            in_specs=[row(D_MODEL, pipeline_mode=pl.Buffered(1)), row(LANES), sliced,
                      wspec(D_MODEL, D_EXPERT), wspec(D_MODEL, D_EXPERT), wspec(D_EXPERT, D_MODEL)],
            out_specs=sliced,
            scratch_shapes=[pltpu.VMEM((nsub, MOE_LOC_ROWS + MOE_LOC_SLACK, D_MODEL), BF16),
                            pltpu.VMEM((pl.cdiv(tb + MOE_LOC_SLACK, MOE_ROW_BLOCK) * MOE_ROW_BLOCK, D_MODEL), BF16)]),
        out_shape=jax.ShapeDtypeStruct((t, D_MODEL), F32),
        compiler_params=pltpu.CompilerParams(dimension_semantics=("parallel", "arbitrary"),
                                             vmem_limit_bytes=VMEM_LIMIT),
        name="moe",
    )(meta, xn, route, x1, wg, wu, wd)


def _prep_params(norm_mix, w_in, q_norm_a, k_norm_a, rpb, w_decay, b_decay, gla_norm, w_o_a, w_o_b, w_out,
                 norm_ffn, w_router_g, b_router_g, w_router_e, b_router_e, w_gate, w_up, w_down):
    depth = w_in.shape[0]
    f32 = lambda a: a.astype(F32)
    pad_lanes = lambda a, before, total: jnp.pad(f32(a), [(0, 0)] * (a.ndim - 1) + [(before, total - before - a.shape[-1])])
    w_aligned = jnp.concatenate([w_in[:, :, :IN_Z_COL0 + 2 * GLA_RANK],
                                 jnp.zeros((depth, D_MODEL, IN_GATE_COL0 - IN_Z_COL0 - 2 * GLA_RANK), w_in.dtype),
                                 w_in[:, :, IN_Z_COL0 + 2 * GLA_RANK:]], axis=-1).astype(BF16)
    w_router = pad_lanes(jnp.concatenate([w_router_g, w_router_e], axis=-1), ROUTER_GROUP_LANE0, LANES)
    b_router = pad_lanes(jnp.concatenate([b_router_g, b_router_e], axis=-1), ROUTER_GROUP_LANE0, LANES)
    w_router = jnp.concatenate(_split_bf16(w_router), axis=-1)
    wdec = jnp.stack([jnp.pad(f32(w_decay[:, d]), ((0, 0), (d * GLA_RANK, LANES - (d + 1) * GLA_RANK), (0, 0)))
                      for d in range(2)], axis=1).astype(BF16)
    return dict(
        in_proj=(f32(norm_mix)[:, None, :], w_aligned,
                 jnp.tile(f32(q_norm_a), (1, NA_HEADS))[:, None, :] * (NA_HEAD_DIM ** -0.5 * LOG2E),
                 jnp.tile(f32(k_norm_a), (1, NA_HEADS))[:, None, :], wdec, f32(b_decay)[:, :, None, :]),
        na=(_na_bias_table(rpb),),
        post=(w_o_a.astype(BF16), w_o_b.astype(BF16), w_out.astype(BF16), f32(gla_norm)[:, None, :],
              f32(norm_ffn)[:, None, :], w_router, b_router[:, None, :]),
        moe=(w_gate.astype(BF16), w_up.astype(BF16), w_down.astype(BF16)),
    )


def _tile(t, want):
    while t % want:
        want //= 2
    return want


def _trunk(x, p, depth):
    batch, n, _ = x.shape
    t = batch * n
    x = x.reshape(t, D_MODEL)
    for l in range(depth):
        qa, ka, va, qb, kb, vb, rb, gf, gr, ga, gb = _in_proj(x, l, *p["in_proj"], _tile(t, 512))
        ya = _na(qa, ka, va, l, *p["na"], batch, n)
        of, ob = _gla(qb, kb, vb, gf, gr, batch, n, GLA_BLOCK_TOK)
        x1, xn, route, meta = _post(x, ya, of, ob, rb, ga, gb, l, *p["post"], _tile(t, 1024))
        meta = meta[:, 0, :2 * N_EXPERTS].reshape(-1)
        x = _moe(meta, xn, route, x1, l, *p["moe"], _tile(t, MOE_TILE))
    return x.reshape(batch, n, D_MODEL)


def kernel(x_prompt, x_sample, norm_mix, w_in, q_norm_a, k_norm_a, rpb, w_decay, b_decay, gla_norm, w_o_a, w_o_b,
           w_out, norm_ffn, w_router_g, b_router_g, w_router_e, b_router_e, w_gate, w_up, w_down):
    p = _prep_params(norm_mix, w_in, q_norm_a, k_norm_a, rpb, w_decay, b_decay, gla_norm, w_o_a, w_o_b, w_out,
                     norm_ffn, w_router_g, b_router_g, w_router_e, b_router_e, w_gate, w_up, w_down)
    depth = w_in.shape[0]
    return (_trunk(x_prompt, p, depth), _trunk(x_sample, p, depth))
```
